```python
import math
import jax
import jax.numpy as jnp
from jax import lax
import numpy as np

D_MODEL = 1024
BATCH = 4
SEQ = 4096
DEPTH = 2
DEC_BATCH = 128
DEC_SEQ = 4
PAST_LEN = 2048
PAGE_SIZE = 128

HEAD_DIM = 64
N_HEADS_MOBA = 4
N_HEADS_SB = 4
N_KV_HEADS = N_HEADS_MOBA + N_HEADS_SB
QKV_DIM = N_KV_HEADS * HEAD_DIM
ATT_DIM = N_HEADS_MOBA * HEAD_DIM
MOBA_BLOCK = 256
MOBA_TOPK = 3
Q_BLOCK = 128
SGU_GROUPS = 4
SGU_WIDTH = 256
SGU_CHUNK = 128
CONV_WIDTH = 3
CONV_DIM = 256
BRANCH_DIM = 256
N_BRANCH = 4
IN_DIM = 3 * QKV_DIM + 2 * SGU_WIDTH + 3 * CONV_DIM
SPLITS = (QKV_DIM, 2 * QKV_DIM, 3 * QKV_DIM, 3 * QKV_DIM + SGU_WIDTH, 3 * QKV_DIM + 2 * SGU_WIDTH,
          3 * QKV_DIM + 2 * SGU_WIDTH + CONV_DIM, 3 * QKV_DIM + 2 * SGU_WIDTH + 2 * CONV_DIM)
REL_BUCKETS = 32
REL_MAX_DIST = 128
D_FF = 2816
N_EXPERTS = 8
TOP_K = 2
D_FF_EXPERT = 3584
MOE_BLOCK = 128
N_DENSE = (DEPTH + 1) // 2
N_MOE = DEPTH // 2
NORM_EPS = 1e-6

kernel_name = 'hybrid_gated_parallel_decoder_step'


def rmsnorm(x, g):
    xf = x.astype(jnp.float32)
    y = xf * lax.rsqrt(jnp.mean(xf * xf, axis=-1, keepdims=True) + NORM_EPS)
    return (y * g.astype(jnp.float32)).astype(x.dtype)


def layernorm(x, g):
    xf = x.astype(jnp.float32)
    xc = xf - jnp.mean(xf, axis=-1, keepdims=True)
    y = xc * lax.rsqrt(jnp.mean(xc * xc, axis=-1, keepdims=True) + NORM_EPS)
    return (y * g.astype(jnp.float32)).astype(x.dtype)


def t5_bucket(dist):
    dist = jnp.maximum(dist, 0)
    max_exact = REL_BUCKETS // 2
    large = max_exact + (jnp.log(jnp.maximum(dist, 1).astype(jnp.float32) / max_exact)
                         / math.log(REL_MAX_DIST / max_exact) * (REL_BUCKETS - max_exact)).astype(jnp.int32)
    large = jnp.minimum(large, REL_BUCKETS - 1)
    return jnp.where(dist < max_exact, dist, large)


def moba_attend(q, k, v, q_pos0, rel_bias):
    B, Tq, H, dh = q.shape
    L = k.shape[1]
    nkb = -(-L // MOBA_BLOCK)
    pad = nkb * MOBA_BLOCK - L
    kbh = jnp.pad(k, ((0, 0), (0, pad), (0, 0), (0, 0))).reshape(B, nkb, MOBA_BLOCK, H, dh).transpose(0, 3, 1, 2, 4)
    vbh = jnp.pad(v, ((0, 0), (0, pad), (0, 0), (0, 0))).reshape(B, nkb, MOBA_BLOCK, H, dh).transpose(0, 3, 1, 2, 4)
    kmean = jnp.mean(kbh.astype(jnp.float32), axis=3)
    topk = min(MOBA_TOPK, nkb)
    qb = min(Q_BLOCK, Tq)
    nqb = Tq // qb
    bt = rel_bias.T.astype(jnp.float32)
    bi = jnp.arange(B)[:, None, None, None]
    hi = jnp.arange(H)[None, :, None, None]
    blk_idx = jnp.arange(nkb)
    offs = jnp.arange(MOBA_BLOCK)
    scale = HEAD_DIM ** -0.5

    def one_block(args):
        qblk, qstart = args
        pos = qstart + jnp.arange(qb)
        cur = pos // MOBA_BLOCK
        gate = jnp.einsum('bqhd,bhnd->bhqn', qblk.astype(jnp.float32), kmean)
        gate = jnp.where(blk_idx[None, None, None, :] < cur[None, None, :, None], gate, -jnp.inf)
        _, sel = lax.top_k(gate, topk)
        sel_ok = sel < cur[None, None, :, None]
        ks = kbh[bi, hi, sel]
        vs = vbh[bi, hi, sel]
        s_sel = jnp.einsum('bqhd,bhqjkd->bhqjk', qblk, ks).astype(jnp.float32) * scale
        dist_sel = pos[None, None, :, None, None] - (sel[..., None] * MOBA_BLOCK + offs)
        s_sel = s_sel + bt[hi[..., None], t5_bucket(dist_sel)]
        s_sel = jnp.where(sel_ok[..., None], s_sel, -jnp.inf)
        own = qstart // MOBA_BLOCK
        k_own = lax.dynamic_index_in_dim(kbh, own, axis=2, keepdims=False)
        v_own = lax.dynamic_index_in_dim(vbh, own, axis=2, keepdims=False)
        dist_own = pos[:, None] - (own * MOBA_BLOCK + offs)[None, :]
        s_own = jnp.einsum('bqhd,bhkd->bhqk', qblk, k_own).astype(jnp.float32) * scale + bt[:, t5_bucket(dist_own)][None]
        s_own = jnp.where((dist_own >= 0)[None, None], s_own, -jnp.inf)
        logits = jnp.concatenate([s_sel.reshape(B, H, qb, topk * MOBA_BLOCK), s_own], axis=-1)
        p = jax.nn.softmax(logits, axis=-1).astype(v.dtype)
        p_sel = p[..., :topk * MOBA_BLOCK].reshape(B, H, qb, topk, MOBA_BLOCK)
        p_own = p[..., topk * MOBA_BLOCK:]
        return (jnp.einsum('bhqjk,bhqjkd->bqhd', p_sel, vs)
                + jnp.einsum('bhqk,bhkd->bqhd', p_own, v_own))

    q_blocks = q.reshape(B, nqb, qb, H, dh).transpose(1, 0, 2, 3, 4)
    qstarts = q_pos0 + jnp.arange(nqb) * qb
    out = lax.map(one_block, (q_blocks, qstarts))
    return out.transpose(1, 0, 2, 3, 4).reshape(B, Tq, H, dh)


def stick_breaking_attend(q, k, v, q_pos0):
    B, Tq, H, dh = q.shape
    L = k.shape[1]
    kh = k.transpose(0, 2, 1, 3)
    vh = v.transpose(0, 2, 1, 3)
    key_pos = jnp.arange(L)
    qb = min(Q_BLOCK, Tq)
    nqb = Tq // qb
    scale = HEAD_DIM ** -0.5

    def one_block(args):
        qblk, qstart = args
        pos = qstart + jnp.arange(qb)
        z = jnp.einsum('bqhd,bhkd->bhqk', qblk, kh).astype(jnp.float32) * scale
        past = (key_pos[None, :] < pos[:, None])[None, None]
        log_keep = jnp.where(past, jax.nn.log_sigmoid(-z), 0.0)
        log_after = lax.cumsum(log_keep, axis=3, reverse=True) - log_keep
        a = jnp.where(past, jnp.exp(jax.nn.log_sigmoid(z) + log_after), 0.0)
        return jnp.einsum('bhqk,bhkd->bqhd', a.astype(v.dtype), vh)

    q_blocks = q.reshape(B, nqb, qb, H, dh).transpose(1, 0, 2, 3, 4)
    qstarts = q_pos0 + jnp.arange(nqb) * qb
    out = lax.map(one_block, (q_blocks, qstarts))
    return out.transpose(1, 0, 2, 3, 4).reshape(B, Tq, H, dh)


def spatial_gating(u, vn, w_s, b_s):
    B, T, _ = u.shape
    tc = min(SGU_CHUNK, T)
    nc = T // tc
    cg = SGU_WIDTH // SGU_GROUPS
    w = jnp.tril(w_s[:, :tc, :tc])
    vr = vn.reshape(B, nc, tc, SGU_GROUPS, cg)
    s = jnp.einsum('gts,bcsgd->bctgd', w, vr) + b_s[:, :tc].T[None, None, :, :, None]
    return u * s.reshape(B, T, SGU_WIDTH)


def short_conv(z, w, prev):
    T = z.shape[1]
    zp = jnp.concatenate([prev.astype(z.dtype), z], axis=1)
    y = w[0] * zp[:, 0:T]
    for i in range(1, CONV_WIDTH):
        y = y + w[i] * zp[:, i:i + T]
    return y, zp[:, -(CONV_WIDTH - 1):]


def token_mixers(h, past_k, past_v, conv_prev, q_pos0, rel_bias, w_in, w_gate, w_sgu, b_sgu, g_sgu, w_conv, w_branch, w_out):
    B, T, _ = h.shape
    proj = h @ w_in
    q, k, v, u, vg, bg, cg, hc = jnp.split(proj, SPLITS, axis=-1)
    q = q.reshape(B, T, N_KV_HEADS, HEAD_DIM)
    k = k.reshape(B, T, N_KV_HEADS, HEAD_DIM)
    v = v.reshape(B, T, N_KV_HEADS, HEAD_DIM)
    if past_k is None:
        k_all, v_all = k, v
    else:
        k_all = jnp.concatenate([past_k.astype(k.dtype), k], axis=1)
        v_all = jnp.concatenate([past_v.astype(v.dtype), v], axis=1)
    nm = N_HEADS_MOBA
    o_a = moba_attend(q[:, :, :nm], k_all[:, :, :nm], v_all[:, :, :nm], q_pos0, rel_bias)
    o_c = stick_breaking_attend(q[:, :, nm:], k_all[:, :, nm:], v_all[:, :, nm:], q_pos0)
    vn = layernorm(jax.nn.gelu(vg), g_sgu)
    o_b = spatial_gating(jax.nn.gelu(u), vn, w_sgu, b_sgu)
    conv_y, conv_state = short_conv(cg * hc, w_conv, conv_prev)
    o_d = bg * conv_y
    branches = jnp.stack([o_a.reshape(B, T, ATT_DIM), o_b, o_c.reshape(B, T, ATT_DIM), o_d], axis=2)
    up = jnp.einsum('btnc,ncd->btnd', branches, w_branch)
    gates = jax.nn.sigmoid(h @ w_gate).reshape(B, T, N_BRANCH, D_MODEL)
    merged = jnp.sum(gates * up, axis=2)
    return merged @ w_out, k, v, conv_state, vn


def swiglu(h, wg, wu, wd):
    return (jax.nn.silu(h @ wg) * (h @ wu)) @ wd


def moe_swiglu(h, w_router, wg, wu, wd):
    B, T, D = h.shape
    xt = h.reshape(-1, D)
    n_tok = xt.shape[0]
    logits = (xt @ w_router).astype(jnp.float32)
    top_val, top_idx = lax.top_k(logits, TOP_K)
    gate = jax.nn.softmax(top_val, axis=-1)
    e_flat = top_idx.reshape(-1)
    tok = jnp.repeat(jnp.arange(n_tok), TOP_K)
    g_flat = gate.reshape(-1)
    n_assign = n_tok * TOP_K
    order = jnp.argsort(e_flat)
    e_s, tok_s, g_s = e_flat[order], tok[order], g_flat[order]
    counts = jnp.bincount(e_flat, length=N_EXPERTS)
    starts = jnp.cumsum(counts) - counts
    padded = (counts + MOE_BLOCK - 1) // MOE_BLOCK * MOE_BLOCK
    pend = jnp.cumsum(padded)
    pstarts = pend - padded
    dest = pstarts[e_s] + (jnp.arange(n_assign) - starts[e_s])
    n_blocks = -(-n_assign // MOE_BLOCK) + N_EXPERTS
    rows = jnp.zeros((n_blocks * MOE_BLOCK, D), h.dtype).at[dest].set(xt[tok_s])
    block_e = jnp.minimum(jnp.searchsorted(pend, jnp.arange(n_blocks) * MOE_BLOCK, side='right'), N_EXPERTS - 1)

    def one_block(args):
        xb, e = args
        return swiglu(xb, wg[e], wu[e], wd[e])

    out = lax.map(one_block, (rows.reshape(n_blocks, MOE_BLOCK, D), block_e)).reshape(-1, D)
    y = jnp.zeros((n_tok, D), h.dtype).at[tok_s].add(out[dest] * g_s[:, None].astype(h.dtype))
    return y.reshape(B, T, D)


def channel_mixer(h, layer, w_ff_gate, w_ff_up, w_ff_down, w_router, w_exp_gate, w_exp_up, w_exp_down):
    i = layer // 2
    if layer % 2 == 0:
        return swiglu(h, w_ff_gate[i], w_ff_up[i], w_ff_down[i])
    return moe_swiglu(h, w_router[i], w_exp_gate[i], w_exp_up[i], w_exp_down[i])


def setup_inputs(seed: int = 0) -> dict:
    key = jax.random.key(seed)
    ks = jax.random.split(key, 32)
    f32 = jnp.float32

    def nrm(k, shape, scale):
        return jax.random.normal(k, shape, f32) * scale

    n_pages = PAST_LEN // PAGE_SIZE
    n_used = DEC_BATCH * n_pages
    n_phys = -((-n_used * 5) // 4)
    page_table = jax.random.permutation(ks[5], n_phys)[:n_used].reshape(DEC_BATCH, n_pages).astype(jnp.int32)
    dinv = D_MODEL ** -0.5
    return {
        'x_prompt': nrm(ks[0], (BATCH, SEQ, D_MODEL), 1.0),
        'x_sample': nrm(ks[1], (DEC_BATCH, DEC_SEQ, D_MODEL), 1.0),
        'cache_k': nrm(ks[2], (DEPTH, n_phys, PAGE_SIZE, N_KV_HEADS, HEAD_DIM), 1.0),
        'cache_v': nrm(ks[3], (DEPTH, n_phys, PAGE_SIZE, N_KV_HEADS, HEAD_DIM), 1.0),
        'state_conv': nrm(ks[4], (DEPTH, DEC_BATCH, CONV_WIDTH - 1, CONV_DIM), 1.0),
        'page_table': page_table,
        'rel_bias': nrm(ks[6], (REL_BUCKETS, N_HEADS_MOBA), 0.2),
        'norm_mix': 1.0 + nrm(ks[7], (DEPTH, D_MODEL), 0.02),
        'w_in': nrm(ks[8], (DEPTH, D_MODEL, IN_DIM), dinv),
        'w_gate': nrm(ks[9], (DEPTH, D_MODEL, N_BRANCH * D_MODEL), dinv),
        'w_sgu': nrm(ks[10], (DEPTH, SGU_GROUPS, SGU_CHUNK, SGU_CHUNK), SGU_CHUNK ** -0.5),
        'b_sgu': 1.0 + nrm(ks[11], (DEPTH, SGU_GROUPS, SGU_CHUNK), 0.02),
        'g_sgu': 1.0 + nrm(ks[12], (DEPTH, SGU_WIDTH), 0.02),
        'w_conv': nrm(ks[13], (DEPTH, CONV_WIDTH, CONV_DIM), CONV_WIDTH ** -0.5),
        'w_branch': nrm(ks[14], (DEPTH, N_BRANCH, BRANCH_DIM, D_MODEL), BRANCH_DIM ** -0.5),
        'w_out': nrm(ks[15], (DEPTH, D_MODEL, D_MODEL), dinv),
        'norm_ffn': 1.0 + nrm(ks[16], (DEPTH, D_MODEL), 0.02),
        'w_ff_gate': nrm(ks[17], (N_DENSE, D_MODEL, D_FF), dinv),
        'w_ff_up': nrm(ks[18], (N_DENSE, D_MODEL, D_FF), dinv),
        'w_ff_down': nrm(ks[19], (N_DENSE, D_FF, D_MODEL), D_FF ** -0.5),
        'w_router': nrm(ks[20], (N_MOE, D_MODEL, N_EXPERTS), dinv),
        'w_exp_gate': nrm(ks[21], (N_MOE, N_EXPERTS, D_MODEL, D_FF_EXPERT), dinv),
        'w_exp_up': nrm(ks[22], (N_MOE, N_EXPERTS, D_MODEL, D_FF_EXPERT), dinv),
        'w_exp_down': nrm(ks[23], (N_MOE, N_EXPERTS, D_FF_EXPERT, D_MODEL), D_FF_EXPERT ** -0.5),
        'norm_final': 1.0 + nrm(ks[24], (D_MODEL,), 0.02),
    }


def reference(x_prompt, x_sample, cache_k, cache_v, state_conv, page_table, rel_bias, norm_mix, w_in, w_gate,
              w_sgu, b_sgu, g_sgu, w_conv, w_branch, w_out, norm_ffn, w_ff_gate, w_ff_up, w_ff_down,
              w_router, w_exp_gate, w_exp_up, w_exp_down, norm_final):
    n_seq, n_pages = page_table.shape
    past_len = n_pages * PAGE_SIZE
    yp, ys = x_prompt, x_sample
    kp_l, vp_l, ks_l, vs_l, cp_l, cs_l, sv_l = [], [], [], [], [], [], []
    for l in range(DEPTH):
        conv0 = jnp.zeros((yp.shape[0], CONV_WIDTH - 1, CONV_DIM), yp.dtype)
        mp, kp, vp, cp, _ = token_mixers(rmsnorm(yp, norm_mix[l]), None, None, conv0, 0, rel_bias,
                                         w_in[l], w_gate[l], w_sgu[l], b_sgu[l], g_sgu[l], w_conv[l], w_branch[l], w_out[l])
        past_k = cache_k[l][page_table].reshape(n_seq, past_len, N_KV_HEADS, HEAD_DIM)
        past_v = cache_v[l][page_table].reshape(n_seq, past_len, N_KV_HEADS, HEAD_DIM)
        ms, ksn, vsn, cs, vns = token_mixers(rmsnorm(ys, norm_mix[l]), past_k, past_v, state_conv[l], past_len, rel_bias,
                                             w_in[l], w_gate[l], w_sgu[l], b_sgu[l], g_sgu[l], w_conv[l], w_branch[l], w_out[l])
        yp = yp + mp
        ys = ys + ms
        yp = yp + channel_mixer(rmsnorm(yp, norm_ffn[l]), l, w_ff_gate, w_ff_up, w_ff_down,
                                w_router, w_exp_gate, w_exp_up, w_exp_down)
        ys = ys + channel_mixer(rmsnorm(ys, norm_ffn[l]), l, w_ff_gate, w_ff_up, w_ff_down,
                                w_router, w_exp_gate, w_exp_up, w_exp_down)
        kp_l.append(kp)
        vp_l.append(vp)
        ks_l.append(ksn)
        vs_l.append(vsn)
        cp_l.append(cp)
        cs_l.append(cs)
        sv_l.append(vns)
    y_prompt = rmsnorm(yp, norm_final)
    y_sample = rmsnorm(ys, norm_final)
    k_prompt = jnp.stack(kp_l)
    v_prompt = jnp.stack(vp_l)
    k_sample = jnp.stack(ks_l)
    v_sample = jnp.stack(vs_l)
    conv_prompt = jnp.stack(cp_l)
    conv_sample = jnp.stack(cs_l)
    sgu_v_sample = jnp.stack(sv_l)
    return (y_prompt, y_sample, k_prompt, v_prompt, k_sample, v_sample, conv_prompt, conv_sample, sgu_v_sample)
```

```python
import functools
import math

import numpy as np
import jax
import jax.numpy as jnp
from jax import lax
from jax.experimental import pallas as pl
from jax.experimental.pallas import tpu as pltpu

F32 = jnp.float32
BF16 = jnp.bfloat16

NORM_EPS = 1e-6
HEAD_DIM = 64
N_HEADS = 8
QKV = N_HEADS * HEAD_DIM
MOBA_BLOCK = 256
MOBA_TOPK = 3
Q_BLOCK = 128
SGU_CHUNK = 128
SGU_WIDTH = 256
SGU_GROUPS = 4
CONV_DIM = 256
CONV_WIDTH = 3
N_BRANCH = 4
BRANCH_DIM = 256
PAGE = 128
REL_BUCKETS = 32
REL_MAX_DIST = 128
TOP_K = 2
ATT_SCALE = HEAD_DIM ** -0.5
MASKED = -1e30
LANES = 128
ROW_TILE = 256
MOE_ROWS = 512
MOE_FF_TILE = 512
FFN_ROWS = 512
FFN_FF_TILE = 256
GATHER_ROWS = 256
V7X_VMEM_LIMIT = 56 * 1024 * 1024


def _params(*sem):
    return pltpu.CompilerParams(dimension_semantics=sem, vmem_limit_bytes=V7X_VMEM_LIMIT)


def _dot(a, b):
    return jnp.dot(a, b, preferred_element_type=F32)


def _dot_nt(a, b):
    return lax.dot_general(a, b, (((1,), (1,)), ((), ())), preferred_element_type=F32)


def _split_bf16(x):
    hi = x.astype(BF16)
    lo = (x - hi.astype(F32)).astype(BF16)
    return hi, lo


def _dot_nt_f32(a, b):
    ah, al = _split_bf16(a)
    bh, bl = _split_bf16(b)
    return _dot_nt(ah, bh) + _dot_nt(ah, bl) + _dot_nt(al, bh)


def _dot_f32(a, b):
    ah, al = _split_bf16(a)
    bh, bl = _split_bf16(b)
    return _dot(ah, bh) + _dot(ah, bl) + _dot(al, bh)


def _rms(x, g):
    return x * lax.rsqrt(jnp.mean(x * x, axis=-1, keepdims=True) + NORM_EPS) * g


def _gelu_tanh(x):
    return 0.5 * x * (1.0 + jnp.tanh(math.sqrt(2.0 / math.pi) * (x + 0.044715 * (x * x * x))))


def _sigmoid(x):
    return 1.0 / (1.0 + jnp.exp(-x))


def _softplus(z):
    return jnp.maximum(z, 0.0) + jnp.log1p(jnp.exp(-jnp.abs(z)))


def _suffix_sum(lk, u):
    n = lk.shape[0]
    hi, lo = _split_bf16(lk)
    r = _dot(jnp.concatenate([hi, lo], axis=0), u)
    return r[:n] + r[n:]


def _strict_upper(n):
    j = np.arange(n)[:, None]
    s = np.arange(n)[None, :]
    return jnp.asarray((j > s).astype(np.float32), dtype=BF16)


def _t5_bucket_table(max_dist):
    d = np.arange(max_dist + 1)
    max_exact = REL_BUCKETS // 2
    large = max_exact + (np.log(np.maximum(d, 1).astype(np.float32) / np.float32(max_exact))
                         / np.float32(math.log(REL_MAX_DIST / max_exact))
                         * np.float32(REL_BUCKETS - max_exact)).astype(np.int32)
    large = np.minimum(large, REL_BUCKETS - 1)
    return np.where(d < max_exact, d, large).astype(np.int32)


def _bias_by_dist(bt_rows, dist, valid):
    table = _t5_bucket_table(int(max(dist.max(), 1)))
    bucket = jnp.asarray(table[np.maximum(dist, 0)])
    b = jnp.take_along_axis(bt_rows, bucket, axis=1)
    return jnp.where(jnp.asarray(valid), b, MASKED)


def _proj_body(x_ref, g_ref, w_ref, q_ref, k_ref, v_ref, uv_ref, bch_ref, kb_ref, vb_ref, km_ref):
    h = _rms(x_ref[...], g_ref[...]).astype(BF16)

    def mm(lo, hi):
        return _dot(h, w_ref[:, lo:hi])

    q_ref[...] = mm(0, QKV)
    k = mm(QKV, 2 * QKV)
    k_ref[...] = k
    kb_ref[...] = k.astype(BF16)
    km_ref[...] = jnp.broadcast_to(jnp.mean(k, axis=0, keepdims=True), km_ref.shape)
    v = mm(2 * QKV, 3 * QKV)
    v_ref[...] = v
    vb_ref[...] = v.astype(BF16)
    uv_ref[...] = mm(3 * QKV, 3 * QKV + 2 * SGU_WIDTH)
    bch_ref[...] = mm(3 * QKV + 2 * SGU_WIDTH, 3 * QKV + 2 * SGU_WIDTH + 3 * CONV_DIM)


def _project(x, g, w_bf16):
    nt, d = x.shape
    in_dim = w_bf16.shape[1]
    tm = MOBA_BLOCK
    nb = nt // tm
    row = lambda i: (i, 0)
    return pl.pallas_call(
        _proj_body,
        grid=(nb,),
        in_specs=[pl.BlockSpec((tm, d), row),
                  pl.BlockSpec((1, d), lambda i: (0, 0)),
                  pl.BlockSpec((d, in_dim), lambda i: (0, 0))],
        out_specs=[pl.BlockSpec((tm, QKV), row), pl.BlockSpec((tm, QKV), row), pl.BlockSpec((tm, QKV), row),
                   pl.BlockSpec((tm, 2 * SGU_WIDTH), row), pl.BlockSpec((tm, 3 * CONV_DIM), row),
                   pl.BlockSpec((tm, QKV), row), pl.BlockSpec((tm, QKV), row),
                   pl.BlockSpec((None, 8, QKV), lambda i: (i, 0, 0))],
        out_shape=[jax.ShapeDtypeStruct((nt, QKV), F32), jax.ShapeDtypeStruct((nt, QKV), F32),
                   jax.ShapeDtypeStruct((nt, QKV), F32), jax.ShapeDtypeStruct((nt, 2 * SGU_WIDTH), F32),
                   jax.ShapeDtypeStruct((nt, 3 * CONV_DIM), F32),
                   jax.ShapeDtypeStruct((nt, QKV), BF16), jax.ShapeDtypeStruct((nt, QKV), BF16),
                   jax.ShapeDtypeStruct((nb, 8, QKV), F32)],
        compiler_params=_params("parallel"),
        name="proj",
    )(x, g.reshape(1, d), w_bf16)


def _top_blocks(gate, lane, lane_mask):
    lane = lane.astype(F32)
    g = jnp.where(lane_mask, gate, -jnp.inf)
    picks = []
    for _ in range(MOBA_TOPK):
        m = jnp.max(g, axis=-1, keepdims=True)
        idx = jnp.min(jnp.where(g == m, lane, 1e9), axis=-1, keepdims=True)
        picks.append((idx, m > -jnp.inf))
        g = jnp.where(lane == idx, -jnp.inf, g)
    return picks


def _is_picked(picks, j):
    j = jnp.asarray(j).astype(F32)
    c = (picks[0][0] == j) & picks[0][1]
    for idx, ok in picks[1:]:
        c = c | ((idx == j) & ok)
    return c


def _moba_prompt_body(fb_ref, q_ref, k_ref, v_ref, km_ref, bias_ref, o_ref):
    pr = pl.program_id(1)
    qi = pl.program_id(2)
    cur = qi // 2
    half = qi % 2
    lane = lax.broadcasted_iota(jnp.int32, (Q_BLOCK, LANES), 1)
    q = q_ref[...] * ATT_SCALE
    km = km_ref[...]
    outs = []
    for hh in range(2):
        head_lanes = (lane >= HEAD_DIM * hh) & (lane < HEAD_DIM * (hh + 1))
        qm = jnp.where(head_lanes, q, 0.0)
        qmb = qm.astype(BF16)
        picks = _top_blocks(_dot_nt_f32(qm, km), lane, lane < cur)
        far_bias = fb_ref[2 * pr + hh]

        def scores(j):
            start = pl.multiple_of(j * MOBA_BLOCK, MOBA_BLOCK)
            return _dot_nt(qmb, k_ref[pl.ds(start, MOBA_BLOCK), :]), v_ref[pl.ds(start, MOBA_BLOCK), :]

        s, vj = scores(cur)
        s = s + bias_ref[hh, half]
        m = jnp.max(s, axis=-1, keepdims=True)
        p = jnp.exp(s - m)
        l = jnp.sum(p, axis=-1, keepdims=True)
        acc = _dot(p.astype(BF16), vj)

        def step(j_load, j_pick, bias, carry):
            m, l, acc = carry
            s, vj = scores(j_load)
            s = jnp.where(_is_picked(picks, j_pick), s + bias, MASKED)
            m_new = jnp.maximum(m, jnp.max(s, axis=-1, keepdims=True))
            alpha = jnp.exp(m - m_new)
            p = jnp.exp(s - m_new)
            l = alpha * l + jnp.sum(p, axis=-1, keepdims=True)
            acc = alpha * acc + _dot(p.astype(BF16), vj)
            return m_new, l, acc

        carry = step(jnp.maximum(cur - 1, 0), cur - 1, bias_ref[hh, 2 + half], (m, l, acc))
        m, l, acc = lax.fori_loop(0, jnp.maximum(cur - 1, 0), lambda j, c: step(j, j, far_bias, c), carry)
        outs.append(acc / l)
    o_ref[...] = jnp.where(lane < HEAD_DIM, outs[0], outs[1])


def _moba_bias_tiles(bt):
    i = np.arange(Q_BLOCK)[:, None]
    c = np.arange(MOBA_BLOCK)[None, :]
    n_h = bt.shape[0]
    tiles = []
    for delta in (0, Q_BLOCK, MOBA_BLOCK, MOBA_BLOCK + Q_BLOCK):
        dist = np.tile(delta + i - c, (n_h, 1))
        t = _bias_by_dist(jnp.repeat(bt, Q_BLOCK, axis=0), dist, dist >= 0)
        tiles.append(t.reshape(n_h, Q_BLOCK, MOBA_BLOCK))
    return jnp.stack(tiles, axis=1)


def _moba_prompt(q, kb, vb, kmean, bias_tiles, far_bias, batch, seq):
    nq = seq // Q_BLOCK
    return pl.pallas_call(
        _moba_prompt_body,
        grid_spec=pltpu.PrefetchScalarGridSpec(
            num_scalar_prefetch=1,
            grid=(batch, 2, nq),
            in_specs=[pl.BlockSpec((Q_BLOCK, LANES), lambda b, p, i, fb: (b * nq + i, p)),
                      pl.BlockSpec((seq, LANES), lambda b, p, i, fb: (b, p)),
                      pl.BlockSpec((seq, LANES), lambda b, p, i, fb: (b, p)),
                      pl.BlockSpec((None, LANES, LANES), lambda b, p, i, fb: (b, 0, p)),
                      pl.BlockSpec((2, 4, Q_BLOCK, MOBA_BLOCK), lambda b, p, i, fb: (p, 0, 0, 0))],
            out_specs=pl.BlockSpec((Q_BLOCK, LANES), lambda b, p, i, fb: (b * nq + i, p)),
        ),
        out_shape=jax.ShapeDtypeStruct((batch * seq, 2 * LANES), F32),
        compiler_params=_params("parallel", "parallel", "arbitrary"),
        name="moba_prompt",
    )(far_bias, q, kb, vb, kmean, bias_tiles)


def _sb_prompt_body(q_ref, k_ref, v_ref, u_ref, o_ref):
    qi = pl.program_id(2)
    lane = lax.broadcasted_iota(jnp.int32, (Q_BLOCK, LANES), 1)
    row = lax.broadcasted_iota(jnp.int32, (Q_BLOCK, LANES), 0)
    q = q_ref[...] * ATT_SCALE
    u = u_ref[...]
    outs = []
    for hh in range(2):
        head_lanes = (lane >= HEAD_DIM * hh) & (lane < HEAD_DIM * (hh + 1))
        qmb = jnp.where(head_lanes, q, 0.0).astype(BF16)

        def tile(j, r_acc, acc, keep):
            start = pl.multiple_of(j * Q_BLOCK, Q_BLOCK)
            z = _dot_nt(qmb, k_ref[pl.ds(start, Q_BLOCK), :])
            sp = _softplus(z)
            lk = -sp if keep is None else jnp.where(keep, -sp, 0.0)
            a = jnp.exp(z - sp + _suffix_sum(lk, u) + r_acc)
            if keep is not None:
                a = jnp.where(keep, a, 0.0)
            acc = acc + _dot(a.astype(BF16), v_ref[pl.ds(start, Q_BLOCK), :])
            return r_acc + jnp.sum(lk, axis=-1, keepdims=True), acc

        carry = tile(qi, jnp.zeros((Q_BLOCK, 1), F32), jnp.zeros((Q_BLOCK, LANES), F32), lane < row)
        _, acc = lax.fori_loop(0, qi, lambda t, c: tile(qi - 1 - t, c[0], c[1], None), carry)
        outs.append(acc)
    o_ref[...] = jnp.where(lane < HEAD_DIM, outs[0], outs[1])


def _sb_prompt(q, kb, vb, batch, seq):
    nq = seq // Q_BLOCK
    return pl.pallas_call(
        _sb_prompt_body,
        grid=(batch, 2, nq),
        in_specs=[pl.BlockSpec((Q_BLOCK, LANES), lambda b, p, i: (b * nq + i, 2 + p)),
                  pl.BlockSpec((seq, LANES), lambda b, p, i: (b, 2 + p)),
                  pl.BlockSpec((seq, LANES), lambda b, p, i: (b, 2 + p)),
                  pl.BlockSpec((Q_BLOCK, Q_BLOCK), lambda b, p, i: (0, 0))],
        out_specs=pl.BlockSpec((Q_BLOCK, LANES), lambda b, p, i: (b * nq + i, p)),
        out_shape=jax.ShapeDtypeStruct((batch * seq, 2 * LANES), F32),
        compiler_params=_params("parallel", "parallel", "arbitrary"),
        name="sb_prompt",
    )(q, kb, vb, _strict_upper(Q_BLOCK))


T_NEW = 4
SOFTMAX_ROWS = 4 * T_NEW


def _lane_head(lane):
    return lane & (N_HEADS - 1)


def _lane_key(lane):
    return lane >> 3


def _row_head(row):
    return row >> 2


def _row_query(row):
    return row & (T_NEW - 1)


def _attn_sample_body(pt_ref, fb_ref, qm_ref, knew_ref, vnew_ref, bias_last_ref, bias_new_ref, u_ref,
                      *rest, n_pages):
    del pt_ref
    k_refs = rest[:n_pages]
    v_refs = rest[n_pages:2 * n_pages]
    o_ref = rest[2 * n_pages]
    n_blocks = n_pages // 2
    sr = SOFTMAX_ROWS
    wide = PAGE * N_HEADS

    qm = qm_ref[...]
    qmb = qm.astype(BF16)

    means = []
    for n in range(n_blocks):
        tot = (jnp.sum(k_refs[2 * n][...].reshape(PAGE, N_HEADS, HEAD_DIM), axis=0)
               + jnp.sum(k_refs[2 * n + 1][...].reshape(PAGE, N_HEADS, HEAD_DIM), axis=0))
        means.append(tot * (1.0 / MOBA_BLOCK))
    means.append(jnp.zeros((LANES - n_blocks * N_HEADS, HEAD_DIM), F32))
    km = jnp.concatenate(means, axis=0)
    lane_s = lax.broadcasted_iota(jnp.int32, (sr, LANES), 1)
    row_s = lax.broadcasted_iota(jnp.int32, (sr, LANES), 0)
    gate = _dot_nt_f32(qm[:sr], km)
    picks = _top_blocks(gate, lane_s, (lane_s < n_blocks * N_HEADS) & (_lane_head(lane_s) == _row_head(row_s)))
    picks = [(jnp.floor(idx * (1.0 / N_HEADS)), ok) for idx, ok in picks]

    head_col = _row_head(lax.broadcasted_iota(jnp.int32, (sr, 1), 0))
    far_bias = jnp.where(head_col == 0, fb_ref[0],
                         jnp.where(head_col == 1, fb_ref[1], jnp.where(head_col == 2, fb_ref[2], fb_ref[3])))

    s = _dot_nt(qmb, knew_ref[...].astype(BF16))
    la = s[:sr] + bias_new_ref[...]
    m = jnp.max(la, axis=-1, keepdims=True)
    pa = jnp.exp(la - m)
    l = jnp.sum(pa, axis=-1, keepdims=True)
    zb = s[sr:]
    keep = ((_lane_head(lane_s) == 4 + _row_head(row_s)) & (_lane_key(lane_s) < _row_query(row_s))
            & (lane_s < T_NEW * N_HEADS))
    sp = _softplus(zb)
    lk = jnp.where(keep, -sp, 0.0)
    a = jnp.where(keep, jnp.exp(zb - sp + _suffix_sum(lk, u_ref[0:LANES, 0:LANES])), 0.0)
    r_acc = jnp.sum(lk, axis=-1, keepdims=True)
    o = _dot(jnp.concatenate([pa, a], axis=0).astype(BF16), vnew_ref[...].astype(BF16))
    acc_a = o[:sr]
    acc_b = o[sr:]

    lane_w = lax.broadcasted_iota(jnp.int32, (sr, wide), 1)
    row_w = lax.broadcasted_iota(jnp.int32, (sr, wide), 0)
    own_a = _lane_head(lane_w) == _row_head(row_w)
    own_b = _lane_head(lane_w) == 4 + _row_head(row_w)
    u = u_ref[...]
    for p in range(n_pages - 1, -1, -1):
        kp = k_refs[p][...].astype(BF16)
        vp = v_refs[p][...].astype(BF16)
        s = _dot_nt(qmb, kp)
        chosen = _is_picked(picks, p // 2)
        if p == n_pages - 1:
            la = jnp.where(chosen, s[:sr] + bias_last_ref[...], MASKED)
        else:
            la = jnp.where(chosen & own_a, s[:sr] + far_bias, MASKED)
        m_new = jnp.maximum(m, jnp.max(la, axis=-1, keepdims=True))
        alpha = jnp.exp(m - m_new)
        pa = jnp.exp(la - m_new)
        l = alpha * l + jnp.sum(pa, axis=-1, keepdims=True)
        m = m_new
        zb = s[sr:]
        sp = _softplus(zb)
        lk = jnp.where(own_b, -sp, 0.0)
        a = jnp.where(own_b, jnp.exp(zb - sp + _suffix_sum(lk, u) + r_acc), 0.0)
        r_acc = r_acc + jnp.sum(lk, axis=-1, keepdims=True)
        o = _dot(jnp.concatenate([pa, a], axis=0).astype(BF16), vp)
        acc_a = alpha * acc_a + o[:sr]
        acc_b = acc_b + o[sr:]
    o_ref[...] = jnp.concatenate([acc_a / l, acc_b], axis=0)


def _attn_sample(q_s, k_s, v_s, cache_k, cache_v, layer, page_table, bt):
    n_seq, n_pages = page_table.shape
    assert q_s.shape[0] == n_seq * T_NEW and n_pages % 2 == 0
    past = n_pages * PAGE
    n_phys = cache_k.shape[1]
    wide = PAGE * N_HEADS
    ck = cache_k.reshape(cache_k.shape[0], n_phys, wide, HEAD_DIM)
    cv = cache_v.reshape(cache_v.shape[0], n_phys, wide, HEAD_DIM)
    qm = (q_s * ATT_SCALE).reshape(n_seq, T_NEW, N_HEADS, HEAD_DIM).transpose(0, 2, 1, 3)
    qm = qm.reshape(n_seq, N_HEADS * T_NEW, HEAD_DIM)
    pad = ((0, 0), (0, LANES - T_NEW * N_HEADS), (0, 0))
    knew = jnp.pad(k_s.reshape(n_seq, T_NEW * N_HEADS, HEAD_DIM), pad)
    vnew = jnp.pad(v_s.reshape(n_seq, T_NEW * N_HEADS, HEAD_DIM), pad)

    r = np.arange(SOFTMAX_ROWS)[:, None]
    hrow, irow = r // T_NEW, r % T_NEW
    bt_rows = jnp.repeat(bt[:4], T_NEW, axis=0)
    f = np.arange(wide)[None, :]
    dist = np.broadcast_to(past + irow - (past - PAGE + f // N_HEADS), (SOFTMAX_ROWS, wide))
    bias_last = _bias_by_dist(bt_rows, dist, np.broadcast_to(f % N_HEADS == hrow, dist.shape))
    f = np.arange(LANES)[None, :]
    dist = np.broadcast_to(irow - f // N_HEADS, (SOFTMAX_ROWS, LANES))
    valid = (f % N_HEADS == hrow) & (dist >= 0) & (f < T_NEW * N_HEADS)
    bias_new = _bias_by_dist(bt_rows, dist, valid)
    far_bias = bt[:4, REL_BUCKETS - 1]

    def page_spec(p):
        return pl.BlockSpec((None, None, wide, HEAD_DIM),
                            lambda b, pt, fb, p=p: (layer, pt[b * n_pages + p], 0, 0))

    seq3 = lambda b, pt, fb: (b, 0, 0)
    const2 = lambda b, pt, fb: (0, 0)
    out = pl.pallas_call(
        functools.partial(_attn_sample_body, n_pages=n_pages),
        grid_spec=pltpu.PrefetchScalarGridSpec(
            num_scalar_prefetch=2,
            grid=(n_seq,),
            in_specs=[pl.BlockSpec((None, N_HEADS * T_NEW, HEAD_DIM), seq3),
                      pl.BlockSpec((None, LANES, HEAD_DIM), seq3),
                      pl.BlockSpec((None, LANES, HEAD_DIM), seq3),
                      pl.BlockSpec((SOFTMAX_ROWS, wide), const2),
                      pl.BlockSpec((SOFTMAX_ROWS, LANES), const2),
                      pl.BlockSpec((wide, wide), const2)]
                     + [page_spec(p) for p in range(n_pages)] * 2,
            out_specs=pl.BlockSpec((None, N_HEADS * T_NEW, HEAD_DIM), seq3),
        ),
        out_shape=jax.ShapeDtypeStruct((n_seq, N_HEADS * T_NEW, HEAD_DIM), F32),
        compiler_params=_params("parallel"),
        name="attn_sample",
    )(page_table.reshape(-1), far_bias, qm, knew, vnew, bias_last, bias_new, _strict_upper(wide),
      *([ck] * n_pages), *([cv] * n_pages))
    return out.reshape(n_seq, N_HEADS, T_NEW, HEAD_DIM).transpose(0, 2, 1, 3).reshape(n_seq * T_NEW, QKV)


def _mix_body(*refs, chunks_per_seq):
    if chunks_per_seq is None:
        uv_ref, bch_ref, p1_ref, p2_ref, ws_ref, bs_ref, gs_ref, wc_ref, ob_ref, od_ref, z_ref, vn_ref = refs
    else:
        uv_ref, bch_ref, prev_ref, ws_ref, bs_ref, gs_ref, wc_ref, ob_ref, od_ref, z_ref, vn_ref = refs
    uv = uv_ref[...]
    x = _gelu_tanh(uv[:, SGU_WIDTH:])
    xc = x - jnp.mean(x, axis=-1, keepdims=True)
    vn = xc * lax.rsqrt(jnp.mean(xc * xc, axis=-1, keepdims=True) + NORM_EPS) * gs_ref[...]
    vn_ref[...] = vn
    vnb = vn.astype(BF16)
    group = lax.broadcasted_iota(jnp.int32, (SGU_CHUNK, SGU_WIDTH), 1) >> 6
    s = jnp.zeros((SGU_CHUNK, SGU_WIDTH), F32)
    for g in range(SGU_GROUPS):
        s = jnp.where(group == g, _dot(ws_ref[g], vnb), s)
    ob_ref[...] = _gelu_tanh(uv[:, :SGU_WIDTH]) * (s + bs_ref[...])

    bch = bch_ref[...]
    z = bch[:, CONV_DIM:2 * CONV_DIM] * bch[:, 2 * CONV_DIM:]
    z_ref[...] = z
    row = lax.broadcasted_iota(jnp.int32, (SGU_CHUNK, CONV_DIM), 0)
    r1 = pltpu.roll(z, 1, 0)
    r2 = pltpu.roll(z, 2, 0)
    if chunks_per_seq is None:
        rr = row & (T_NEW - 1)
        z1 = jnp.where(rr >= 1, r1, p1_ref[...])
        z2 = jnp.where(rr >= 2, r2, p2_ref[...])
    else:
        pb = prev_ref[...]
        zp = pb[:, CONV_DIM:2 * CONV_DIM] * pb[:, 2 * CONV_DIM:]
        zp = jnp.where(pl.program_id(0) % chunks_per_seq == 0, 0.0, zp)
        z1 = jnp.where(row >= 1, r1, zp[7:8, :])
        z2 = jnp.where(row >= 2, r2, jnp.where(row == 1, zp[7:8, :], zp[6:7, :]))
    wc = wc_ref[...]
    od_ref[...] = bch[:, :CONV_DIM] * (wc[0:1, :] * z2 + wc[1:2, :] * z1 + wc[2:3, :] * z)


def _mix(uv, bch, history, ws_bf16, bs, gs, wc, chunks_per_seq, n_rows):
    nc = n_rows // SGU_CHUNK
    row = lambda i: (i, 0)
    const = lambda i: (0, 0)
    if chunks_per_seq is None:
        hist_specs = [pl.BlockSpec((SGU_CHUNK, CONV_DIM), row)] * 2
        hist = list(history)
    else:
        sub = SGU_CHUNK // 8
        hist_specs = [pl.BlockSpec((8, 3 * CONV_DIM), lambda i: (jnp.maximum(i * sub - 1, 0), 0))]
        hist = [bch]
    return pl.pallas_call(
        functools.partial(_mix_body, chunks_per_seq=chunks_per_seq),
        grid=(nc,),
        in_specs=[pl.BlockSpec((SGU_CHUNK, 2 * SGU_WIDTH), row), pl.BlockSpec((SGU_CHUNK, 3 * CONV_DIM), row)]
                 + hist_specs
                 + [pl.BlockSpec((SGU_GROUPS, SGU_CHUNK, SGU_CHUNK), lambda i: (0, 0, 0)),
                    pl.BlockSpec((SGU_CHUNK, SGU_WIDTH), const),
                    pl.BlockSpec((1, SGU_WIDTH), const),
                    pl.BlockSpec((8, CONV_DIM), const)],
        out_specs=[pl.BlockSpec((SGU_CHUNK, SGU_WIDTH), row)] * 4,
        out_shape=[jax.ShapeDtypeStruct((n_rows, SGU_WIDTH), F32)] * 4,
        compiler_params=_params("parallel"),
        name="mix_sample" if chunks_per_seq is None else "mix_prompt",
    )(uv, bch, *hist, ws_bf16, bs, gs.reshape(1, SGU_WIDTH), jnp.pad(wc, ((0, 8 - CONV_WIDTH), (0, 0))))


def _merge_body(x_ref, g_ref, pa_ref, pb_ref, pc_ref, pd_ref, sa_ref, sb_ref, sc_ref, sd_ref,
                wg_ref, wb_ref, wo_ref, o_ref, *, n_prompt_tiles):
    x = x_ref[...]
    d = x.shape[1]
    h = _rms(x, g_ref[...]).astype(BF16)
    is_prompt = pl.program_id(0) < n_prompt_tiles
    merged = jnp.zeros(x.shape, F32)
    for n, (p_ref, s_ref) in enumerate(((pa_ref, sa_ref), (pb_ref, sb_ref), (pc_ref, sc_ref), (pd_ref, sd_ref))):
        br = jnp.where(is_prompt, p_ref[...], s_ref[...]).astype(BF16)
        up = _dot(br, wb_ref[n])
        gate = _sigmoid(_dot(h, wg_ref[:, n * d:(n + 1) * d]))
        merged = merged + gate * up
    o_ref[...] = x + _dot(merged.astype(BF16), wo_ref[...])


def _merge(x, g, prompt_br, sample_br, wg_bf16, wb_bf16, wo_bf16, n_prompt):
    nt, d = x.shape
    tm = ROW_TILE
    npt = n_prompt // tm
    row = lambda i: (i, 0)
    p_spec = pl.BlockSpec((tm, BRANCH_DIM), lambda i: (jnp.minimum(i, npt - 1), 0))
    s_spec = pl.BlockSpec((tm, BRANCH_DIM), lambda i: (jnp.maximum(i - npt, 0), 0))
    return pl.pallas_call(
        functools.partial(_merge_body, n_prompt_tiles=npt),
        grid=(nt // tm,),
        in_specs=[pl.BlockSpec((tm, d), row), pl.BlockSpec((1, d), lambda i: (0, 0))]
                 + [p_spec] * N_BRANCH + [s_spec] * N_BRANCH
                 + [pl.BlockSpec((d, N_BRANCH * d), lambda i: (0, 0)),
                    pl.BlockSpec((N_BRANCH, BRANCH_DIM, d), lambda i: (0, 0, 0)),
                    pl.BlockSpec((d, d), lambda i: (0, 0))],
        out_specs=pl.BlockSpec((tm, d), row),
        out_shape=jax.ShapeDtypeStruct((nt, d), F32),
        compiler_params=_params("parallel"),
        name="merge",
    )(x, g.reshape(1, d), *prompt_br, *sample_br, wg_bf16, wb_bf16, wo_bf16)


def _silu(x):
    return x * _sigmoid(x)


def _ffn_body(x_ref, g_ref, wg_ref, wu_ref, wd_ref, o_ref, h_ref, acc_ref):
    j = pl.program_id(1)

    @pl.when(j == 0)
    def _():
        h_ref[...] = _rms(x_ref[...], g_ref[...]).astype(BF16)
        acc_ref[...] = jnp.zeros_like(acc_ref)

    h = h_ref[...]
    a = _silu(_dot(h, wg_ref[...])) * _dot(h, wu_ref[...])
    acc_ref[...] += _dot(a.astype(BF16), wd_ref[...])

    @pl.when(j == pl.num_programs(1) - 1)
    def _():
        o_ref[...] = x_ref[...] + acc_ref[...]


def _ffn(x, g, wg_bf16, wu_bf16, wd_bf16):
    nt, d = x.shape
    dff = wg_bf16.shape[1]
    tm, tf = FFN_ROWS, FFN_FF_TILE
    return pl.pallas_call(
        _ffn_body,
        grid=(nt // tm, dff // tf),
        in_specs=[pl.BlockSpec((tm, d), lambda i, j: (i, 0)), pl.BlockSpec((1, d), lambda i, j: (0, 0)),
                  pl.BlockSpec((d, tf), lambda i, j: (0, j)), pl.BlockSpec((d, tf), lambda i, j: (0, j)),
                  pl.BlockSpec((tf, d), lambda i, j: (j, 0))],
        out_specs=pl.BlockSpec((tm, d), lambda i, j: (i, 0)),
        out_shape=jax.ShapeDtypeStruct((nt, d), F32),
        scratch_shapes=[pltpu.VMEM((tm, d), BF16), pltpu.VMEM((tm, d), F32)],
        compiler_params=_params("parallel", "arbitrary"),
        name="ffn",
    )(x, g.reshape(1, d), wg_bf16, wu_bf16, wd_bf16)


def _route_body(x_ref, g_ref, wr_ref, idx_ref, gate_ref, *, n_exp):
    h = _rms(x_ref[...], g_ref[...])
    logits = _dot_nt_f32(h, wr_ref[...])
    lane_i = lax.broadcasted_iota(jnp.int32, logits.shape, 1)
    lane = lane_i.astype(F32)
    lg = jnp.where(lane_i < n_exp, logits, -jnp.inf)
    m1 = jnp.max(lg, axis=-1, keepdims=True)
    i1 = jnp.min(jnp.where(lg == m1, lane, float(LANES)), axis=-1, keepdims=True)
    lg2 = jnp.where(lane == i1, -jnp.inf, lg)
    m2 = jnp.max(lg2, axis=-1, keepdims=True)
    i2 = jnp.min(jnp.where(lg2 == m2, lane, float(LANES)), axis=-1, keepdims=True)
    e2 = jnp.exp(m2 - m1)
    idx_ref[...] = jnp.where(lane_i == 0, i1, jnp.where(lane_i == 1, i2, 0.0)).astype(jnp.int32)
    gate_ref[...] = jnp.where(lane_i == 0, 1.0 / (1.0 + e2), jnp.where(lane_i == 1, e2 / (1.0 + e2), 0.0))


def _route(x, g, w_router):
    nt, d = x.shape
    n_exp = w_router.shape[1]
    assert n_exp <= LANES
    wr = jnp.pad(w_router.T, ((0, LANES - n_exp), (0, 0)))
    tm = ROW_TILE
    row = lambda i: (i, 0)
    return pl.pallas_call(
        functools.partial(_route_body, n_exp=n_exp),
        grid=(nt // tm,),
        in_specs=[pl.BlockSpec((tm, d), row), pl.BlockSpec((1, d), lambda i: (0, 0)),
                  pl.BlockSpec((LANES, d), lambda i: (0, 0))],
        out_specs=[pl.BlockSpec((tm, LANES), row)] * 2,
        out_shape=[jax.ShapeDtypeStruct((nt, LANES), jnp.int32), jax.ShapeDtypeStruct((nt, LANES), F32)],
        compiler_params=_params("parallel"),
        name="route",
    )(x, g.reshape(1, d), wr)


def _gather_body(idx_ref, src_ref, o_ref, sem):
    base = pl.program_id(0) * GATHER_ROWS

    def copy(r):
        return pltpu.make_async_copy(src_ref.at[pl.ds(idx_ref[base + r], 1), :], o_ref.at[pl.ds(r, 1), :], sem)

    def start(r, c):
        copy(r).start()
        return c

    def wait(r, c):
        copy(r).wait()
        return c

    lax.fori_loop(0, GATHER_ROWS, start, 0)
    lax.fori_loop(0, GATHER_ROWS, wait, 0)


def _gather_rows(src, idx):
    n = idx.shape[0]
    d = src.shape[1]
    return pl.pallas_call(
        _gather_body,
        grid_spec=pltpu.PrefetchScalarGridSpec(
            num_scalar_prefetch=1,
            grid=(n // GATHER_ROWS,),
            in_specs=[pl.BlockSpec(memory_space=pl.ANY)],
            out_specs=pl.BlockSpec((GATHER_ROWS, d), lambda i, idx: (i, 0)),
            scratch_shapes=[pltpu.SemaphoreType.DMA],
        ),
        out_shape=jax.ShapeDtypeStruct((n, d), src.dtype),
        compiler_params=_params("arbitrary"),
        name="gather_rows",
    )(idx, src)


def _expert_body(be_ref, x_ref, g_ref, rg_ref, wg_ref, wu_ref, wd_ref, o_ref, h_ref, acc_ref):
    del be_ref
    j = pl.program_id(1)

    @pl.when(j == 0)
    def _():
        h_ref[...] = _rms(x_ref[...], g_ref[...]).astype(BF16)
        acc_ref[...] = jnp.zeros_like(acc_ref)

    h = h_ref[...]
    a = _silu(_dot(h, wg_ref[...])) * _dot(h, wu_ref[...])
    acc_ref[...] += _dot(a.astype(BF16), wd_ref[...])

    @pl.when(j == pl.num_programs(1) - 1)
    def _():
        o_ref[...] = acc_ref[...] * rg_ref[...]


def _experts(x_rows, g, row_gate, block_expert, wg_bf16, wu_bf16, wd_bf16):
    r, d = x_rows.shape
    dff = wg_bf16.shape[2]
    tm, tf = MOE_ROWS, MOE_FF_TILE
    return pl.pallas_call(
        _expert_body,
        grid_spec=pltpu.PrefetchScalarGridSpec(
            num_scalar_prefetch=1,
            grid=(r // tm, dff // tf),
            in_specs=[pl.BlockSpec((tm, d), lambda i, j, be: (i, 0)),
                      pl.BlockSpec((1, d), lambda i, j, be: (0, 0)),
                      pl.BlockSpec((tm, 1), lambda i, j, be: (i, 0)),
                      pl.BlockSpec((None, d, tf), lambda i, j, be: (be[i], 0, j)),
                      pl.BlockSpec((None, d, tf), lambda i, j, be: (be[i], 0, j)),
                      pl.BlockSpec((None, tf, d), lambda i, j, be: (be[i], j, 0))],
            out_specs=pl.BlockSpec((tm, d), lambda i, j, be: (i, 0)),
            scratch_shapes=[pltpu.VMEM((tm, d), BF16), pltpu.VMEM((tm, d), F32)],
        ),
        out_shape=jax.ShapeDtypeStruct((r, d), F32),
        compiler_params=_params("parallel", "arbitrary"),
        name="experts",
    )(block_expert, x_rows, g.reshape(1, d), row_gate, wg_bf16, wu_bf16, wd_bf16)


def _combine_body(d0_ref, d1_ref, x_ref, src_ref, o_ref, a_ref, b_ref, sem):
    base = pl.program_id(0) * GATHER_ROWS

    def copies(r):
        return (pltpu.make_async_copy(src_ref.at[pl.ds(d0_ref[base + r], 1), :], a_ref.at[pl.ds(r, 1), :], sem),
                pltpu.make_async_copy(src_ref.at[pl.ds(d1_ref[base + r], 1), :], b_ref.at[pl.ds(r, 1), :], sem))

    def start(r, c):
        for cp in copies(r):
            cp.start()
        return c

    def wait(r, c):
        for cp in copies(r):
            cp.wait()
        return c

    lax.fori_loop(0, GATHER_ROWS, start, 0)
    lax.fori_loop(0, GATHER_ROWS, wait, 0)
    o_ref[...] = x_ref[...] + a_ref[...] + b_ref[...]


def _combine(x, expert_rows, d0, d1):
    nt, d = x.shape
    return pl.pallas_call(
        _combine_body,
        grid_spec=pltpu.PrefetchScalarGridSpec(
            num_scalar_prefetch=2,
            grid=(nt // GATHER_ROWS,),
            in_specs=[pl.BlockSpec((GATHER_ROWS, d), lambda i, a, b: (i, 0)),
                      pl.BlockSpec(memory_space=pl.ANY)],
            out_specs=pl.BlockSpec((GATHER_ROWS, d), lambda i, a, b: (i, 0)),
            scratch_shapes=[pltpu.VMEM((GATHER_ROWS, d), F32), pltpu.VMEM((GATHER_ROWS, d), F32),
                            pltpu.SemaphoreType.DMA],
        ),
        out_shape=jax.ShapeDtypeStruct((nt, d), F32),
        compiler_params=_params("arbitrary"),
        name="combine",
    )(d0, d1, x, expert_rows)


def _moe(x, g, w_router, wg_bf16, wu_bf16, wd_bf16):
    nt, d = x.shape
    n_exp = w_router.shape[1]
    tm = MOE_ROWS
    idx, gate = _route(x, g, w_router)
    e_flat = idx[:, :TOP_K].reshape(-1)
    g_flat = gate[:, :TOP_K].reshape(-1)
    n_assign = nt * TOP_K
    onehot = (e_flat[:, None] == jnp.arange(n_exp, dtype=jnp.int32)[None, :]).astype(jnp.int32)
    rank = jnp.cumsum(onehot, axis=0) - onehot
    pos = jnp.sum(rank * onehot, axis=1)
    counts = jnp.sum(onehot, axis=0)
    padded = (counts + tm - 1) // tm * tm
    pend = jnp.cumsum(padded)
    dest = ((pend - padded)[e_flat] + pos).astype(jnp.int32)
    n_rows = (n_assign // tm + n_exp) * tm
    tok = jnp.arange(n_assign, dtype=jnp.int32) // TOP_K
    src_tok = jnp.zeros((n_rows,), jnp.int32).at[dest].set(tok, unique_indices=True)
    row_gate = jnp.zeros((n_rows,), F32).at[dest].set(g_flat, unique_indices=True)
    block_expert = jnp.minimum(
        jnp.searchsorted(pend, jnp.arange(n_rows // tm, dtype=jnp.int32) * tm, side='right'), n_exp - 1
    ).astype(jnp.int32)
    x_rows = _gather_rows(x, src_tok)
    y_rows = _experts(x_rows, g, row_gate.reshape(n_rows, 1), block_expert, wg_bf16, wu_bf16, wd_bf16)
    dest2 = dest.reshape(nt, TOP_K)
    return _combine(x, y_rows, dest2[:, 0], dest2[:, 1])


def _norm_body(x_ref, g_ref, o_ref):
    o_ref[...] = _rms(x_ref[...], g_ref[...])


def _final_norm(x, g):
    nt, d = x.shape
    tm = ROW_TILE
    return pl.pallas_call(
        _norm_body,
        grid=(nt // tm,),
        in_specs=[pl.BlockSpec((tm, d), lambda i: (i, 0)), pl.BlockSpec((1, d), lambda i: (0, 0))],
        out_specs=pl.BlockSpec((tm, d), lambda i: (i, 0)),
        out_shape=jax.ShapeDtypeStruct((nt, d), F32),
        compiler_params=_params("parallel"),
        name="final_norm",
    )(x, g.reshape(1, d))


def kernel(x_prompt, x_sample, cache_k, cache_v, state_conv, page_table, rel_bias, norm_mix, w_in, w_gate,
           w_sgu, b_sgu, g_sgu, w_conv, w_branch, w_out, norm_ffn, w_ff_gate, w_ff_up, w_ff_down,
           w_router, w_exp_gate, w_exp_up, w_exp_down, norm_final):
    batch, seq, d = x_prompt.shape
    n_seq, t_new, _ = x_sample.shape
    depth = w_in.shape[0]
    n_p = batch * seq
    n_s = n_seq * t_new
    assert t_new == T_NEW and seq % MOBA_BLOCK == 0 and seq // MOBA_BLOCK <= LANES
    assert n_p % FFN_ROWS == 0 and n_s % ROW_TILE == 0 and (n_p + n_s) % FFN_ROWS == 0
    n_kb = seq // MOBA_BLOCK
    group_w = SGU_WIDTH // SGU_GROUPS

    x = jnp.concatenate([x_prompt.reshape(n_p, d), x_sample.reshape(n_s, d)], axis=0)
    bt = rel_bias.T.astype(F32)
    moba_tiles = _moba_bias_tiles(bt)
    far_bias = bt[:, REL_BUCKETS - 1]

    outs = {k: [] for k in ("kp", "vp", "ks", "vs", "cp", "cs", "sv")}
    for l in range(depth):
        q, k, v, uv, bch, kb, vb, km = _project(x, norm_mix[l], w_in[l].astype(BF16))

        kmean = jnp.pad(km[:n_p // MOBA_BLOCK, 0, :].reshape(batch, n_kb, QKV), ((0, 0), (0, LANES - n_kb), (0, 0)))
        o_a = _moba_prompt(q, kb, vb, kmean, moba_tiles, far_bias, batch, seq)
        o_c = _sb_prompt(q, kb, vb, batch, seq)
        ws = jnp.tril(w_sgu[l]).astype(BF16)
        bs = jnp.repeat(b_sgu[l].T, group_w, axis=1)
        o_b, o_d, z_p, _ = _mix(uv, bch, None, ws, bs, g_sgu[l], w_conv[l], seq // SGU_CHUNK, n_p)

        o_att = _attn_sample(q[n_p:], k[n_p:], v[n_p:], cache_k, cache_v, l, page_table, bt)
        per_chunk = SGU_CHUNK // T_NEW
        w4 = jnp.tril(w_sgu[l][:, :T_NEW, :T_NEW])
        ws_s = jnp.einsum('ab,gts->gatbs', jnp.eye(per_chunk, dtype=F32), w4)
        ws_s = ws_s.reshape(SGU_GROUPS, SGU_CHUNK, SGU_CHUNK).astype(BF16)
        bs_s = jnp.tile(jnp.repeat(b_sgu[l][:, :T_NEW].T, group_w, axis=1), (per_chunk, 1))
        prev = state_conv[l]
        zero = jnp.zeros((n_seq, 1, CONV_DIM), F32)
        p1 = jnp.concatenate([prev[:, 1:2], zero, zero, zero], axis=1).reshape(n_s, CONV_DIM)
        p2 = jnp.concatenate([prev[:, 0:1], prev[:, 1:2], zero, zero], axis=1).reshape(n_s, CONV_DIM)
        s_b, s_d, z_s, vn_s = _mix(uv[n_p:], bch[n_p:], (p1, p2), ws_s, bs_s, g_sgu[l], w_conv[l], None, n_s)

        prompt_br = (o_a, o_b, o_c, o_d)
        sample_br = (o_att[:, :QKV // 2], s_b, o_att[:, QKV // 2:], s_d)
        x = _merge(x, norm_mix[l], prompt_br, sample_br, w_gate[l].astype(BF16), w_branch[l].astype(BF16),
                   w_out[l].astype(BF16), n_p)

        i = l // 2
        if l % 2 == 0:
            x = _ffn(x, norm_ffn[l], w_ff_gate[i].astype(BF16), w_ff_up[i].astype(BF16), w_ff_down[i].astype(BF16))
        else:
            x = _moe(x, norm_ffn[l], w_router[i], w_exp_gate[i].astype(BF16), w_exp_up[i].astype(BF16),
                     w_exp_down[i].astype(BF16))

        outs["kp"].append(k[:n_p].reshape(batch, seq, N_HEADS, HEAD_DIM))
        outs["vp"].append(v[:n_p].reshape(batch, seq, N_HEADS, HEAD_DIM))
        outs["ks"].append(k[n_p:].reshape(n_seq, t_new, N_HEADS, HEAD_DIM))
        outs["vs"].append(v[n_p:].reshape(n_seq, t_new, N_HEADS, HEAD_DIM))
        outs["cp"].append(z_p.reshape(batch, seq, CONV_DIM)[:, seq - (CONV_WIDTH - 1):])
        outs["cs"].append(z_s.reshape(n_seq, t_new, CONV_DIM)[:, t_new - (CONV_WIDTH - 1):])
        outs["sv"].append(vn_s.reshape(n_seq, t_new, SGU_WIDTH))

    y = _final_norm(x, norm_final)
    return (y[:n_p].reshape(batch, seq, d), y[n_p:].reshape(n_seq, t_new, d),
            jnp.stack(outs["kp"]), jnp.stack(outs["vp"]), jnp.stack(outs["ks"]), jnp.stack(outs["vs"]),
            jnp.stack(outs["cp"]), jnp.stack(outs["cs"]), jnp.stack(outs["sv"]))
```

```python
import functools
import math

import numpy as np
import jax
import jax.numpy as jnp
from jax import lax
from jax.experimental import pallas as pl
from jax.experimental.pallas import tpu as pltpu

F32 = jnp.float32
BF16 = jnp.bfloat16

NORM_EPS = 1e-6
HEAD_DIM = 64
N_HEADS = 8
QKV = N_HEADS * HEAD_DIM
MOBA_BLOCK = 256
MOBA_TOPK = 3
Q_BLOCK = 128
SGU_CHUNK = 128
SGU_WIDTH = 256
SGU_GROUPS = 4
CONV_DIM = 256
CONV_WIDTH = 3
N_BRANCH = 4
BRANCH_DIM = 256
PAGE = 128
REL_BUCKETS = 32
REL_MAX_DIST = 128
TOP_K = 2
ATT_SCALE = HEAD_DIM ** -0.5
MASKED = -1e30
SB_DROP = -105.0
LANES = 128
ROW_TILE = 256
MOE_ROWS = 512
MOE_FF_TILE = 512
FFN_ROWS = 512
FFN_FF_TILE = 256
GATHER_ROWS = 256
V7X_VMEM_LIMIT = 56 * 1024 * 1024


def _params(*sem):
    return pltpu.CompilerParams(dimension_semantics=sem, vmem_limit_bytes=V7X_VMEM_LIMIT)


def _dot(a, b):
    return jnp.dot(a, b, preferred_element_type=F32)


def _dot_nt(a, b):
    return lax.dot_general(a, b, (((1,), (1,)), ((), ())), preferred_element_type=F32)


def _split_bf16(x):
    hi = x.astype(BF16)
    lo = (x - hi.astype(F32)).astype(BF16)
    return hi, lo


def _dot_nt_f32(a, b):
    ah, al = _split_bf16(a)
    bh, bl = _split_bf16(b)
    return _dot_nt(ah, bh) + _dot_nt(ah, bl) + _dot_nt(al, bh)


def _dot_f32(a, b):
    ah, al = _split_bf16(a)
    bh, bl = _split_bf16(b)
    return _dot(ah, bh) + _dot(ah, bl) + _dot(al, bh)


def _rms(x, g):
    return x * lax.rsqrt(jnp.mean(x * x, axis=-1, keepdims=True) + NORM_EPS) * g


def _gelu_tanh(x):
    return 0.5 * x * (1.0 + jnp.tanh(math.sqrt(2.0 / math.pi) * (x + 0.044715 * (x * x * x))))


def _sigmoid(x):
    return 1.0 / (1.0 + jnp.exp(-x))


def _softplus(z):
    return jnp.maximum(z, 0.0) + jnp.log1p(jnp.exp(-jnp.abs(z)))


def _suffix_sum(lk, u):
    n = lk.shape[0]
    hi, lo = _split_bf16(lk)
    r = _dot(jnp.concatenate([hi, lo], axis=0), u)
    return r[:n] + r[n:]


def _strict_upper(n):
    j = np.arange(n)[:, None]
    s = np.arange(n)[None, :]
    return jnp.asarray((j > s).astype(np.float32), dtype=BF16)


def _t5_bucket_table(max_dist):
    d = np.arange(max_dist + 1)
    max_exact = REL_BUCKETS // 2
    large = max_exact + (np.log(np.maximum(d, 1).astype(np.float32) / np.float32(max_exact))
                         / np.float32(math.log(REL_MAX_DIST / max_exact))
                         * np.float32(REL_BUCKETS - max_exact)).astype(np.int32)
    large = np.minimum(large, REL_BUCKETS - 1)
    return np.where(d < max_exact, d, large).astype(np.int32)


def _bias_by_dist(bt_rows, dist, valid):
    table = _t5_bucket_table(2 * REL_MAX_DIST)
    b = jnp.broadcast_to(bt_rows[:, 0:1], dist.shape)
    for k in range(1, REL_BUCKETS):
        first = int(np.argmax(table >= k))
        b = jnp.where(dist >= first, bt_rows[:, k:k + 1], b)
    return jnp.where(valid, b, MASKED)


def _proj_body(x_ref, g_ref, w_ref, q_ref, k_ref, v_ref, uv_ref, bch_ref, kb_ref, vb_ref, km_ref):
    h = _rms(x_ref[...], g_ref[...]).astype(BF16)

    def mm(lo, hi):
        return _dot(h, w_ref[:, lo:hi])

    q_ref[...] = mm(0, QKV)
    k = mm(QKV, 2 * QKV)
    k_ref[...] = k
    kb_ref[...] = k.astype(BF16)
    km_ref[...] = jnp.broadcast_to(jnp.mean(k, axis=0, keepdims=True), km_ref.shape)
    v = mm(2 * QKV, 3 * QKV)
    v_ref[...] = v
    vb_ref[...] = v.astype(BF16)
    uv_ref[...] = mm(3 * QKV, 3 * QKV + 2 * SGU_WIDTH)
    bch_ref[...] = mm(3 * QKV + 2 * SGU_WIDTH, 3 * QKV + 2 * SGU_WIDTH + 3 * CONV_DIM)


def _project(x, g, w_bf16):
    nt, d = x.shape
    in_dim = w_bf16.shape[1]
    tm = MOBA_BLOCK
    nb = nt // tm
    row = lambda i: (i, 0)
    return pl.pallas_call(
        _proj_body,
        grid=(nb,),
        in_specs=[pl.BlockSpec((tm, d), row),
                  pl.BlockSpec((1, d), lambda i: (0, 0)),
                  pl.BlockSpec((d, in_dim), lambda i: (0, 0))],
        out_specs=[pl.BlockSpec((tm, QKV), row), pl.BlockSpec((tm, QKV), row), pl.BlockSpec((tm, QKV), row),
                   pl.BlockSpec((tm, 2 * SGU_WIDTH), row), pl.BlockSpec((tm, 3 * CONV_DIM), row),
                   pl.BlockSpec((tm, QKV), row), pl.BlockSpec((tm, QKV), row),
                   pl.BlockSpec((None, 8, QKV), lambda i: (i, 0, 0))],
        out_shape=[jax.ShapeDtypeStruct((nt, QKV), F32), jax.ShapeDtypeStruct((nt, QKV), F32),
                   jax.ShapeDtypeStruct((nt, QKV), F32), jax.ShapeDtypeStruct((nt, 2 * SGU_WIDTH), F32),
                   jax.ShapeDtypeStruct((nt, 3 * CONV_DIM), F32),
                   jax.ShapeDtypeStruct((nt, QKV), BF16), jax.ShapeDtypeStruct((nt, QKV), BF16),
                   jax.ShapeDtypeStruct((nb, 8, QKV), F32)],
        compiler_params=_params("parallel"),
        name="proj",
    )(x, g.reshape(1, d), w_bf16)


def _top_blocks(gate, lane, lane_mask):
    lane = lane.astype(F32)
    g = jnp.where(lane_mask, gate, -jnp.inf)
    picks = []
    for _ in range(MOBA_TOPK):
        m = jnp.max(g, axis=-1, keepdims=True)
        idx = jnp.min(jnp.where(g == m, lane, 1e9), axis=-1, keepdims=True)
        picks.append((idx, m > -jnp.inf))
        g = jnp.where(lane == idx, -jnp.inf, g)
    return picks


def _is_picked(picks, j):
    j = jnp.asarray(j).astype(F32)
    c = (picks[0][0] == j) & picks[0][1]
    for idx, ok in picks[1:]:
        c = c | ((idx == j) & ok)
    return c


def _moba_prompt_body(fb_ref, q_ref, k_ref, v_ref, km_ref, bias_ref, o_ref, *, chunk):
    pr = pl.program_id(1)
    qi = pl.program_id(2)
    cur = qi // 2
    half = qi % 2
    rows = 2 * Q_BLOCK
    lane = lax.broadcasted_iota(jnp.int32, (Q_BLOCK, LANES), 1)
    q = q_ref[...] * ATT_SCALE
    qs = jnp.concatenate([jnp.where(lane < HEAD_DIM, q, 0.0), jnp.where(lane >= HEAD_DIM, q, 0.0)], axis=0)
    qsb = qs.astype(BF16)
    lane2 = lax.broadcasted_iota(jnp.int32, (rows, LANES), 1)
    picks = _top_blocks(_dot_nt_f32(qs, km_ref[...]), lane2, lane2 < cur)
    first_head = lax.broadcasted_iota(jnp.int32, (rows, 1), 0) < Q_BLOCK
    far_bias = jnp.where(first_head, fb_ref[2 * pr], fb_ref[2 * pr + 1])

    start = pl.multiple_of(cur * MOBA_BLOCK, MOBA_BLOCK)
    s = _dot_nt(qsb, k_ref[pl.ds(start, MOBA_BLOCK), :]) + bias_ref[half]
    m = jnp.max(s, axis=-1, keepdims=True)
    p = jnp.exp(s - m)
    l = jnp.sum(p, axis=-1, keepdims=True)
    acc = _dot(p.astype(BF16), v_ref[pl.ds(start, MOBA_BLOCK), :])

    def chunk_step(c, carry):
        m, l, acc = carry
        start = pl.multiple_of(c * (chunk * MOBA_BLOCK), chunk * MOBA_BLOCK)
        s = _dot_nt(qsb, k_ref[pl.ds(start, chunk * MOBA_BLOCK), :])
        parts = []
        for jb in range(chunk):
            j = c * chunk + jb
            bias = jnp.where(j == cur - 1, bias_ref[2 + half], far_bias)
            sj = s[:, jb * MOBA_BLOCK:(jb + 1) * MOBA_BLOCK]
            parts.append(jnp.where(_is_picked(picks, j), sj + bias, MASKED))
        s = jnp.concatenate(parts, axis=1)
        m_new = jnp.maximum(m, jnp.max(s, axis=-1, keepdims=True))
        alpha = jnp.exp(m - m_new)
        p = jnp.exp(s - m_new)
        l = alpha * l + jnp.sum(p, axis=-1, keepdims=True)
        acc = alpha * acc + _dot(p.astype(BF16), v_ref[pl.ds(start, chunk * MOBA_BLOCK), :])
        return m_new, l, acc

    m, l, acc = lax.fori_loop(0, (cur + chunk - 1) // chunk, chunk_step, (m, l, acc))
    out = acc / l
    o_ref[...] = jnp.where(lane < HEAD_DIM, out[:Q_BLOCK], out[Q_BLOCK:])


def _moba_bias_tiles(bt):
    n_pairs = bt.shape[0] // 2
    shape = (n_pairs * 2 * Q_BLOCK, MOBA_BLOCK)
    i = lax.broadcasted_iota(jnp.int32, shape, 0) & (Q_BLOCK - 1)
    c = lax.broadcasted_iota(jnp.int32, shape, 1)
    bt_rows = jnp.repeat(bt, Q_BLOCK, axis=0)
    tiles = []
    for delta in (0, Q_BLOCK, MOBA_BLOCK, MOBA_BLOCK + Q_BLOCK):
        dist = delta + i - c
        tiles.append(_bias_by_dist(bt_rows, dist, dist >= 0).reshape(n_pairs, 2 * Q_BLOCK, MOBA_BLOCK))
    return jnp.stack(tiles, axis=1)


def _moba_prompt(q, kb, vb, kmean, bias_tiles, far_bias, batch, seq):
    nq = seq // Q_BLOCK
    n_kb = seq // MOBA_BLOCK
    chunk = next(c for c in (4, 2, 1) if n_kb % c == 0)
    return pl.pallas_call(
        functools.partial(_moba_prompt_body, chunk=chunk),
        grid_spec=pltpu.PrefetchScalarGridSpec(
            num_scalar_prefetch=1,
            grid=(batch, 2, nq),
            in_specs=[pl.BlockSpec((Q_BLOCK, LANES), lambda b, p, i, fb: (b * nq + i, p)),
                      pl.BlockSpec((seq, LANES), lambda b, p, i, fb: (b, p)),
                      pl.BlockSpec((seq, LANES), lambda b, p, i, fb: (b, p)),
                      pl.BlockSpec((None, LANES, LANES), lambda b, p, i, fb: (b, 0, p)),
                      pl.BlockSpec((None, 4, 2 * Q_BLOCK, MOBA_BLOCK), lambda b, p, i, fb: (p, 0, 0, 0))],
            out_specs=pl.BlockSpec((Q_BLOCK, LANES), lambda b, p, i, fb: (b * nq + i, p)),
        ),
        out_shape=jax.ShapeDtypeStruct((batch * seq, 2 * LANES), F32),
        compiler_params=_params("parallel", "parallel", "arbitrary"),
        name="moba_prompt",
    )(far_bias, q, kb, vb, kmean, bias_tiles)


def _sb_prompt_body(q_ref, k_ref, v_ref, u_ref, o_ref):
    qi = pl.program_id(2)
    rows = 2 * Q_BLOCK
    lane = lax.broadcasted_iota(jnp.int32, (Q_BLOCK, LANES), 1)
    q = q_ref[...] * ATT_SCALE
    qsb = jnp.concatenate([jnp.where(lane < HEAD_DIM, q, 0.0), jnp.where(lane >= HEAD_DIM, q, 0.0)],
                          axis=0).astype(BF16)
    u = u_ref[...]
    col = lax.broadcasted_iota(jnp.int32, (rows, Q_BLOCK), 1)
    qrow = lax.broadcasted_iota(jnp.int32, (rows, Q_BLOCK), 0) & (Q_BLOCK - 1)

    def tile(j, keep, r_acc, acc):
        start = pl.multiple_of(jnp.maximum(j, 0) * Q_BLOCK, Q_BLOCK)
        z = _dot_nt(qsb, k_ref[pl.ds(start, Q_BLOCK), :])
        sp = _softplus(z)
        lk = -sp if keep is None else jnp.where(keep, -sp, 0.0)
        a = jnp.exp(z - sp + _suffix_sum(lk, u) + r_acc)
        if keep is not None:
            a = jnp.where(keep, a, 0.0)
        acc = acc + _dot(a.astype(BF16), v_ref[pl.ds(start, Q_BLOCK), :])
        return r_acc + jnp.sum(lk, axis=-1, keepdims=True), acc

    def exists(j):
        return jnp.broadcast_to(j >= 0, (rows, Q_BLOCK))

    r_acc, acc = tile(qi, col < qrow, jnp.zeros((rows, 1), F32), jnp.zeros((rows, LANES), F32))
    r_acc, acc = tile(qi - 1, exists(qi - 1), r_acc, acc)
    r_acc, acc = tile(qi - 2, exists(qi - 2), r_acc, acc)

    def more(state):
        it, go, _, _ = state
        return (qi - 3 - 2 * it >= 0) & (go > 0)

    def older(state):
        it, _, r_acc, acc = state
        j = qi - 3 - 2 * it
        r_acc, acc = tile(j, None, r_acc, acc)
        r_acc, acc = tile(j - 1, exists(j - 1), r_acc, acc)
        return it + 1, (jnp.max(r_acc) > SB_DROP).astype(jnp.int32), r_acc, acc

    go = (jnp.max(r_acc) > SB_DROP).astype(jnp.int32)
    _, _, _, acc = lax.while_loop(more, older, (jnp.int32(0), go, r_acc, acc))
    o_ref[...] = jnp.where(lane < HEAD_DIM, acc[:Q_BLOCK], acc[Q_BLOCK:])


def _sb_prompt(q, kb, vb, batch, seq):
    nq = seq // Q_BLOCK
    return pl.pallas_call(
        _sb_prompt_body,
        grid=(batch, 2, nq),
        in_specs=[pl.BlockSpec((Q_BLOCK, LANES), lambda b, p, i: (b * nq + i, 2 + p)),
                  pl.BlockSpec((seq, LANES), lambda b, p, i: (b, 2 + p)),
                  pl.BlockSpec((seq, LANES), lambda b, p, i: (b, 2 + p)),
                  pl.BlockSpec((Q_BLOCK, Q_BLOCK), lambda b, p, i: (0, 0))],
        out_specs=pl.BlockSpec((Q_BLOCK, LANES), lambda b, p, i: (b * nq + i, p)),
        out_shape=jax.ShapeDtypeStruct((batch * seq, 2 * LANES), F32),
        compiler_params=_params("parallel", "parallel", "arbitrary"),
        name="sb_prompt",
    )(q, kb, vb, _strict_upper(Q_BLOCK))


T_NEW = 4
SOFTMAX_ROWS = 4 * T_NEW


def _lane_head(lane):
    return lane & (N_HEADS - 1)


def _lane_key(lane):
    return lane >> 3


def _row_head(row):
    return row >> 2


def _row_query(row):
    return row & (T_NEW - 1)


def _attn_sample_body(pt_ref, fb_ref, qm_ref, knew_ref, vnew_ref, bias_last_ref, bias_new_ref, u_ref,
                      *rest, n_pages):
    del pt_ref
    k_refs = rest[:n_pages]
    v_refs = rest[n_pages:2 * n_pages]
    o_ref = rest[2 * n_pages]
    n_blocks = n_pages // 2
    sr = SOFTMAX_ROWS
    wide = PAGE * N_HEADS

    qm = qm_ref[...]
    qmb = qm.astype(BF16)

    means = []
    for n in range(n_blocks):
        tot = jnp.sum(k_refs[2 * n][...], axis=0) + jnp.sum(k_refs[2 * n + 1][...], axis=0)
        means.append(tot * (1.0 / MOBA_BLOCK))
    means.append(jnp.zeros((LANES - n_blocks * N_HEADS, HEAD_DIM), F32))
    km = jnp.concatenate(means, axis=0)
    lane_s = lax.broadcasted_iota(jnp.int32, (sr, LANES), 1)
    row_s = lax.broadcasted_iota(jnp.int32, (sr, LANES), 0)
    gate = _dot_nt_f32(qm[:sr], km)
    picks = _top_blocks(gate, lane_s, (lane_s < n_blocks * N_HEADS) & (_lane_head(lane_s) == _row_head(row_s)))
    picks = [(jnp.floor(idx * (1.0 / N_HEADS)), ok) for idx, ok in picks]

    head_col = _row_head(lax.broadcasted_iota(jnp.int32, (sr, 1), 0))
    far_bias = jnp.where(head_col == 0, fb_ref[0],
                         jnp.where(head_col == 1, fb_ref[1], jnp.where(head_col == 2, fb_ref[2], fb_ref[3])))

    s = _dot_nt(qmb, knew_ref[...].astype(BF16))
    la = s[:sr] + bias_new_ref[...]
    m = jnp.max(la, axis=-1, keepdims=True)
    pa = jnp.exp(la - m)
    l = jnp.sum(pa, axis=-1, keepdims=True)
    zb = s[sr:]
    keep = ((_lane_head(lane_s) == 4 + _row_head(row_s)) & (_lane_key(lane_s) < _row_query(row_s))
            & (lane_s < T_NEW * N_HEADS))
    sp = _softplus(zb)
    lk = jnp.where(keep, -sp, 0.0)
    a = jnp.where(keep, jnp.exp(zb - sp + _suffix_sum(lk, u_ref[0:LANES, 0:LANES])), 0.0)
    r_acc = jnp.sum(lk, axis=-1, keepdims=True)
    o = _dot(jnp.concatenate([pa, a], axis=0).astype(BF16), vnew_ref[...].astype(BF16))
    acc_a = o[:sr]
    acc_b = o[sr:]

    lane_w = lax.broadcasted_iota(jnp.int32, (sr, wide), 1)
    row_w = lax.broadcasted_iota(jnp.int32, (sr, wide), 0)
    own_a = _lane_head(lane_w) == _row_head(row_w)
    own_b = _lane_head(lane_w) == 4 + _row_head(row_w)
    for p in range(n_pages - 1, -1, -1):
        kp = k_refs[p][...].reshape(wide, HEAD_DIM).astype(BF16)
        vp = v_refs[p][...].reshape(wide, HEAD_DIM).astype(BF16)
        s = _dot_nt(qmb, kp)
        chosen = _is_picked(picks, p // 2)
        if p == n_pages - 1:
            la = jnp.where(chosen, s[:sr] + bias_last_ref[...], MASKED)
        else:
            la = jnp.where(chosen & own_a, s[:sr] + far_bias, MASKED)
        m_new = jnp.maximum(m, jnp.max(la, axis=-1, keepdims=True))
        alpha = jnp.exp(m - m_new)
        pa = jnp.exp(la - m_new)
        l = alpha * l + jnp.sum(pa, axis=-1, keepdims=True)
        m = m_new

        def sticks(zb, r_acc):
            sp = _softplus(zb)
            lk = jnp.where(own_b, -sp, 0.0)
            a = jnp.where(own_b, jnp.exp(zb - sp + _suffix_sum(lk, u_ref[...]) + r_acc), 0.0)
            return a, r_acc + jnp.sum(lk, axis=-1, keepdims=True)

        def no_sticks(zb, r_acc):
            return jnp.zeros_like(zb), r_acc

        a, r_acc = lax.cond(jnp.max(r_acc) > SB_DROP, sticks, no_sticks, s[sr:], r_acc)
        o = _dot(jnp.concatenate([pa, a], axis=0).astype(BF16), vp)
        acc_a = alpha * acc_a + o[:sr]
        acc_b = acc_b + o[sr:]
    o_ref[...] = jnp.concatenate([acc_a / l, acc_b], axis=0)


def _attn_sample(q_s, k_s, v_s, cache_k, cache_v, layer, page_table, bt):
    n_seq, n_pages = page_table.shape
    assert q_s.shape[0] == n_seq * T_NEW and n_pages % 2 == 0
    wide = PAGE * N_HEADS
    qm =(q_s * ATT_SCALE).reshape(n_seq, T_NEW, N_HEADS, HEAD_DIM).transpose(0, 2, 1, 3)
    qm = qm.reshape(n_seq, N_HEADS * T_NEW, HEAD_DIM)
    pad = ((0, 0), (0, LANES - T_NEW * N_HEADS), (0, 0))
    knew = jnp.pad(k_s.reshape(n_seq, T_NEW * N_HEADS, HEAD_DIM), pad)
    vnew = jnp.pad(v_s.reshape(n_seq, T_NEW * N_HEADS, HEAD_DIM), pad)

    bt_rows = jnp.repeat(bt[:4], T_NEW, axis=0)

    def lane_bias(width, first_key_dist, need_causal):
        r = lax.broadcasted_iota(jnp.int32, (SOFTMAX_ROWS, width), 0)
        f = lax.broadcasted_iota(jnp.int32, (SOFTMAX_ROWS, width), 1)
        dist = first_key_dist + _row_query(r) - _lane_key(f)
        valid = _lane_head(f) == _row_head(r)
        if need_causal:
            valid = valid & (dist >= 0) & (f < T_NEW * N_HEADS)
        return _bias_by_dist(bt_rows, dist, valid)

    bias_last = lane_bias(wide, PAGE, False)
    bias_new = lane_bias(LANES, 0, True)
    far_bias = bt[:4, REL_BUCKETS - 1]

    def page_spec(p):
        return pl.BlockSpec((None, None, PAGE, N_HEADS, HEAD_DIM),
                            lambda b, pt, fb, p=p: (layer, pt[b * n_pages + p], 0, 0, 0))

    seq3 = lambda b, pt, fb: (b, 0, 0)
    const2 = lambda b, pt, fb: (0, 0)
    out = pl.pallas_call(
        functools.partial(_attn_sample_body, n_pages=n_pages),
        grid_spec=pltpu.PrefetchScalarGridSpec(
            num_scalar_prefetch=2,
            grid=(n_seq,),
            in_specs=[pl.BlockSpec((None, N_HEADS * T_NEW, HEAD_DIM), seq3),
                      pl.BlockSpec((None, LANES, HEAD_DIM), seq3),
                      pl.BlockSpec((None, LANES, HEAD_DIM), seq3),
                      pl.BlockSpec((SOFTMAX_ROWS, wide), const2),
                      pl.BlockSpec((SOFTMAX_ROWS, LANES), const2),
                      pl.BlockSpec((wide, wide), const2)]
                     + [page_spec(p) for p in range(n_pages)] * 2,
            out_specs=pl.BlockSpec((None, N_HEADS * T_NEW, HEAD_DIM), seq3),
        ),
        out_shape=jax.ShapeDtypeStruct((n_seq, N_HEADS * T_NEW, HEAD_DIM), F32),
        compiler_params=_params("parallel"),
        name="attn_sample",
    )(page_table.reshape(-1), far_bias, qm, knew, vnew, bias_last, bias_new, _strict_upper(wide),
      *([cache_k] * n_pages), *([cache_v] * n_pages))
    return out.reshape(n_seq, N_HEADS, T_NEW, HEAD_DIM).transpose(0, 2, 1, 3).reshape(n_seq * T_NEW, QKV)


def _mix_body(*refs, chunks_per_seq):
    if chunks_per_seq is None:
        uv_ref, bch_ref, p1_ref, p2_ref, ws_ref, bs_ref, gs_ref, wc_ref, ob_ref, od_ref, z_ref, vn_ref = refs
    else:
        uv_ref, bch_ref, prev_ref, ws_ref, bs_ref, gs_ref, wc_ref, ob_ref, od_ref, z_ref, vn_ref = refs
    uv = uv_ref[...]
    x = _gelu_tanh(uv[:, SGU_WIDTH:])
    xc = x - jnp.mean(x, axis=-1, keepdims=True)
    vn = xc * lax.rsqrt(jnp.mean(xc * xc, axis=-1, keepdims=True) + NORM_EPS) * gs_ref[...]
    vn_ref[...] = vn
    vnb = vn.astype(BF16)
    group = lax.broadcasted_iota(jnp.int32, (SGU_CHUNK, SGU_WIDTH), 1) >> 6
    s = jnp.zeros((SGU_CHUNK, SGU_WIDTH), F32)
    for g in range(SGU_GROUPS):
        s = jnp.where(group == g, _dot(ws_ref[g], vnb), s)
    ob_ref[...] = _gelu_tanh(uv[:, :SGU_WIDTH]) * (s + bs_ref[...])

    bch = bch_ref[...]
    z = bch[:, CONV_DIM:2 * CONV_DIM] * bch[:, 2 * CONV_DIM:]
    z_ref[...] = z
    row = lax.broadcasted_iota(jnp.int32, (SGU_CHUNK, CONV_DIM), 0)
    r1 = pltpu.roll(z, 1, 0)
    r2 = pltpu.roll(z, 2, 0)
    if chunks_per_seq is None:
        rr = row & (T_NEW - 1)
        z1 = jnp.where(rr >= 1, r1, p1_ref[...])
        z2 = jnp.where(rr >= 2, r2, p2_ref[...])
    else:
        pb = prev_ref[...]
        zp = pb[:, CONV_DIM:2 * CONV_DIM] * pb[:, 2 * CONV_DIM:]
        zp = jnp.where(pl.program_id(0) % chunks_per_seq == 0, 0.0, zp)
        z1 = jnp.where(row >= 1, r1, zp[7:8, :])
        z2 = jnp.where(row >= 2, r2, jnp.where(row == 1, zp[7:8, :], zp[6:7, :]))
    wc = wc_ref[...]
    od_ref[...] = bch[:, :CONV_DIM] * (wc[0:1, :] * z2 + wc[1:2, :] * z1 + wc[2:3, :] * z)


def _mix(uv, bch, history, ws_bf16, bs, gs, wc, chunks_per_seq, n_rows):
    nc = n_rows // SGU_CHUNK
    row = lambda i: (i, 0)
    const = lambda i: (0, 0)
    if chunks_per_seq is None:
        hist_specs = [pl.BlockSpec((SGU_CHUNK, CONV_DIM), row)] * 2
        hist = list(history)
    else:
        sub = SGU_CHUNK // 8
        hist_specs = [pl.BlockSpec((8, 3 * CONV_DIM), lambda i: (jnp.maximum(i * sub - 1, 0), 0))]
        hist = [bch]
    return pl.pallas_call(
        functools.partial(_mix_body, chunks_per_seq=chunks_per_seq),
        grid=(nc,),
        in_specs=[pl.BlockSpec((SGU_CHUNK, 2 * SGU_WIDTH), row), pl.BlockSpec((SGU_CHUNK, 3 * CONV_DIM), row)]
                 + hist_specs
                 + [pl.BlockSpec((SGU_GROUPS, SGU_CHUNK, SGU_CHUNK), lambda i: (0, 0, 0)),
                    pl.BlockSpec((SGU_CHUNK, SGU_WIDTH), const),
                    pl.BlockSpec((1, SGU_WIDTH), const),
                    pl.BlockSpec((8, CONV_DIM), const)],
        out_specs=[pl.BlockSpec((SGU_CHUNK, SGU_WIDTH), row)] * 4,
        out_shape=[jax.ShapeDtypeStruct((n_rows, SGU_WIDTH), F32)] * 4,
        compiler_params=_params("parallel"),
        name="mix_sample" if chunks_per_seq is None else "mix_prompt",
    )(uv, bch, *hist, ws_bf16, bs, gs.reshape(1, SGU_WIDTH), jnp.pad(wc, ((0, 8 - CONV_WIDTH), (0, 0))))


def _merge_body(x_ref, g_ref, pa_ref, pb_ref, pc_ref, pd_ref, sa_ref, sb_ref, sc_ref, sd_ref,
                wg_ref, wb_ref, wo_ref, o_ref, *, n_prompt_tiles):
    x = x_ref[...]
    d = x.shape[1]
    h = _rms(x, g_ref[...]).astype(BF16)
    is_prompt = pl.program_id(0) < n_prompt_tiles
    merged = jnp.zeros(x.shape, F32)
    for n, (p_ref, s_ref) in enumerate(((pa_ref, sa_ref), (pb_ref, sb_ref), (pc_ref, sc_ref), (pd_ref, sd_ref))):
        br = jnp.where(is_prompt, p_ref[...], s_ref[...]).astype(BF16)
        up = _dot(br, wb_ref[n])
        gate = _sigmoid(_dot(h, wg_ref[:, n * d:(n + 1) * d]))
        merged = merged + gate * up
    o_ref[...] = x + _dot(merged.astype(BF16), wo_ref[...])


def _merge(x, g, prompt_br, sample_br, wg_bf16, wb_bf16, wo_bf16, n_prompt):
    nt, d = x.shape
    tm = ROW_TILE
    npt = n_prompt // tm
    row = lambda i: (i, 0)
    p_spec = pl.BlockSpec((tm, BRANCH_DIM), lambda i: (jnp.minimum(i, npt - 1), 0))
    s_spec = pl.BlockSpec((tm, BRANCH_DIM), lambda i: (jnp.maximum(i - npt, 0), 0))
    return pl.pallas_call(
        functools.partial(_merge_body, n_prompt_tiles=npt),
        grid=(nt // tm,),
        in_specs=[pl.BlockSpec((tm, d), row), pl.BlockSpec((1, d), lambda i: (0, 0))]
                 + [p_spec] * N_BRANCH + [s_spec] * N_BRANCH
                 + [pl.BlockSpec((d, N_BRANCH * d), lambda i: (0, 0)),
                    pl.BlockSpec((N_BRANCH, BRANCH_DIM, d), lambda i: (0, 0, 0)),
                    pl.BlockSpec((d, d), lambda i: (0, 0))],
        out_specs=pl.BlockSpec((tm, d), row),
        out_shape=jax.ShapeDtypeStruct((nt, d), F32),
        compiler_params=_params("parallel"),
        name="merge",
    )(x, g.reshape(1, d), *prompt_br, *sample_br, wg_bf16, wb_bf16, wo_bf16)


def _silu(x):
    return x * _sigmoid(x)


def _ffn_body(x_ref, g_ref, wg_ref, wu_ref, wd_ref, o_ref, h_ref, acc_ref):
    j = pl.program_id(1)

    @pl.when(j == 0)
    def _():
        h_ref[...] = _rms(x_ref[...], g_ref[...]).astype(BF16)
        acc_ref[...] = jnp.zeros_like(acc_ref)

    h = h_ref[...]
    a = _silu(_dot(h, wg_ref[...])) * _dot(h, wu_ref[...])
    acc_ref[...] += _dot(a.astype(BF16), wd_ref[...])

    @pl.when(j == pl.num_programs(1) - 1)
    def _():
        o_ref[...] = x_ref[...] + acc_ref[...]


def _ffn(x, g, wg_bf16, wu_bf16, wd_bf16):
    nt, d = x.shape
    dff = wg_bf16.shape[1]
    tm, tf = FFN_ROWS, FFN_FF_TILE
    return pl.pallas_call(
        _ffn_body,
        grid=(nt // tm, dff // tf),
        in_specs=[pl.BlockSpec((tm, d), lambda i, j: (i, 0)), pl.BlockSpec((1, d), lambda i, j: (0, 0)),
                  pl.BlockSpec((d, tf), lambda i, j: (0, j)), pl.BlockSpec((d, tf), lambda i, j: (0, j)),
                  pl.BlockSpec((tf, d), lambda i, j: (j, 0))],
        out_specs=pl.BlockSpec((tm, d), lambda i, j: (i, 0)),
        out_shape=jax.ShapeDtypeStruct((nt, d), F32),
        scratch_shapes=[pltpu.VMEM((tm, d), BF16), pltpu.VMEM((tm, d), F32)],
        compiler_params=_params("parallel", "arbitrary"),
        name="ffn",
    )(x, g.reshape(1, d), wg_bf16, wu_bf16, wd_bf16)


def _route_body(x_ref, g_ref, wr_ref, idx_ref, gate_ref, *, n_exp):
    h = _rms(x_ref[...], g_ref[...])
    logits = _dot_nt_f32(h, wr_ref[...])
    lane_i = lax.broadcasted_iota(jnp.int32, logits.shape, 1)
    lane = lane_i.astype(F32)
    lg = jnp.where(lane_i < n_exp, logits, -jnp.inf)
    m1 = jnp.max(lg, axis=-1, keepdims=True)
    i1 = jnp.min(jnp.where(lg == m1, lane, float(LANES)), axis=-1, keepdims=True)
    lg2 = jnp.where(lane == i1, -jnp.inf, lg)
    m2 = jnp.max(lg2, axis=-1, keepdims=True)
    i2 = jnp.min(jnp.where(lg2 == m2, lane, float(LANES)), axis=-1, keepdims=True)
    e2 = jnp.exp(m2 - m1)
    idx_ref[...] = jnp.where(lane_i == 0, i1, jnp.where(lane_i == 1, i2, 0.0)).astype(jnp.int32)
    gate_ref[...] = jnp.where(lane_i == 0, 1.0 / (1.0 + e2), jnp.where(lane_i == 1, e2 / (1.0 + e2), 0.0))


def _route(x, g, w_router):
    nt, d = x.shape
    n_exp = w_router.shape[1]
    assert n_exp <= LANES
    wr = jnp.pad(w_router.T, ((0, LANES - n_exp), (0, 0)))
    tm = ROW_TILE
    row = lambda i: (i, 0)
    return pl.pallas_call(
        functools.partial(_route_body, n_exp=n_exp),
        grid=(nt // tm,),
        in_specs=[pl.BlockSpec((tm, d), row), pl.BlockSpec((1, d), lambda i: (0, 0)),
                  pl.BlockSpec((LANES, d), lambda i: (0, 0))],
        out_specs=[pl.BlockSpec((tm, LANES), row)] * 2,
        out_shape=[jax.ShapeDtypeStruct((nt, LANES), jnp.int32), jax.ShapeDtypeStruct((nt, LANES), F32)],
        compiler_params=_params("parallel"),
        name="route",
    )(x, g.reshape(1, d), wr)


def _gather_body(idx_ref, src_ref, o_ref, sem):
    base = pl.program_id(0) * GATHER_ROWS

    def copy(r):
        return pltpu.make_async_copy(src_ref.at[pl.ds(idx_ref[base + r], 1), :], o_ref.at[pl.ds(r, 1), :], sem)

    def start(r, c):
        copy(r).start()
        return c

    def wait(r, c):
        copy(r).wait()
        return c

    lax.fori_loop(0, GATHER_ROWS, start, 0)
    lax.fori_loop(0, GATHER_ROWS, wait, 0)


def _gather_rows(src, idx):
    n = idx.shape[0]
    d = src.shape[1]
    return pl.pallas_call(
        _gather_body,
        grid_spec=pltpu.PrefetchScalarGridSpec(
            num_scalar_prefetch=1,
            grid=(n // GATHER_ROWS,),
            in_specs=[pl.BlockSpec(memory_space=pl.ANY)],
            out_specs=pl.BlockSpec((GATHER_ROWS, d), lambda i, idx: (i, 0)),
            scratch_shapes=[pltpu.SemaphoreType.DMA],
        ),
        out_shape=jax.ShapeDtypeStruct((n, d), src.dtype),
        compiler_params=_params("arbitrary"),
        name="gather_rows",
    )(idx, src)


def _expert_body(be_ref, x_ref, g_ref, rg_ref, wg_ref, wu_ref, wd_ref, o_ref, h_ref, acc_ref):
    del be_ref
    j = pl.program_id(1)

    @pl.when(j == 0)
    def _():
        h_ref[...] = _rms(x_ref[...], g_ref[...]).astype(BF16)
        acc_ref[...] = jnp.zeros_like(acc_ref)

    h = h_ref[...]
    a = _silu(_dot(h, wg_ref[...])) * _dot(h, wu_ref[...])
    acc_ref[...] += _dot(a.astype(BF16), wd_ref[...])

    @pl.when(j == pl.num_programs(1) - 1)
    def _():
        o_ref[...] = acc_ref[...] * rg_ref[...]


def _experts(x_rows, g, row_gate, block_expert, wg_bf16, wu_bf16, wd_bf16):
    r, d = x_rows.shape
    dff = wg_bf16.shape[2]
    tm, tf = MOE_ROWS, MOE_FF_TILE
    return pl.pallas_call(
        _expert_body,
        grid_spec=pltpu.PrefetchScalarGridSpec(
            num_scalar_prefetch=1,
            grid=(r // tm, dff // tf),
            in_specs=[pl.BlockSpec((tm, d), lambda i, j, be: (i, 0)),
                      pl.BlockSpec((1, d), lambda i, j, be: (0, 0)),
                      pl.BlockSpec((tm, 1), lambda i, j, be: (i, 0)),
                      pl.BlockSpec((None, d, tf), lambda i, j, be: (be[i], 0, j)),
                      pl.BlockSpec((None, d, tf), lambda i, j, be: (be[i], 0, j)),
                      pl.BlockSpec((None, tf, d), lambda i, j, be: (be[i], j, 0))],
            out_specs=pl.BlockSpec((tm, d), lambda i, j, be: (i, 0)),
            scratch_shapes=[pltpu.VMEM((tm, d), BF16), pltpu.VMEM((tm, d), F32)],
        ),
        out_shape=jax.ShapeDtypeStruct((r, d), F32),
        compiler_params=_params("parallel", "arbitrary"),
        name="experts",
    )(block_expert, x_rows, g.reshape(1, d), row_gate, wg_bf16, wu_bf16, wd_bf16)


def _combine_body(d0_ref, d1_ref, x_ref, src_ref, o_ref, a_ref, b_ref, sem):
    base = pl.program_id(0) * GATHER_ROWS

    def copies(r):
        return (pltpu.make_async_copy(src_ref.at[pl.ds(d0_ref[base + r], 1), :], a_ref.at[pl.ds(r, 1), :], sem),
                pltpu.make_async_copy(src_ref.at[pl.ds(d1_ref[base + r], 1), :], b_ref.at[pl.ds(r, 1), :], sem))

    def start(r, c):
        for cp in copies(r):
            cp.start()
        return c

    def wait(r, c):
        for cp in copies(r):
            cp.wait()
        return c

    lax.fori_loop(0, GATHER_ROWS, start, 0)
    lax.fori_loop(0, GATHER_ROWS, wait, 0)
    o_ref[...] = x_ref[...] + a_ref[...] + b_ref[...]


def _combine(x, expert_rows, d0, d1):
    nt, d = x.shape
    return pl.pallas_call(
        _combine_body,
        grid_spec=pltpu.PrefetchScalarGridSpec(
            num_scalar_prefetch=2,
            grid=(nt // GATHER_ROWS,),
            in_specs=[pl.BlockSpec((GATHER_ROWS, d), lambda i, a, b: (i, 0)),
                      pl.BlockSpec(memory_space=pl.ANY)],
            out_specs=pl.BlockSpec((GATHER_ROWS, d), lambda i, a, b: (i, 0)),
            scratch_shapes=[pltpu.VMEM((GATHER_ROWS, d), F32), pltpu.VMEM((GATHER_ROWS, d), F32),
                            pltpu.SemaphoreType.DMA],
        ),
        out_shape=jax.ShapeDtypeStruct((nt, d), F32),
        compiler_params=_params("arbitrary"),
        name="combine",
    )(d0, d1, x, expert_rows)


def _moe(x, g, w_router, wg_bf16, wu_bf16, wd_bf16):
    nt, d = x.shape
    n_exp = w_router.shape[1]
    tm = MOE_ROWS
    idx, gate = _route(x, g, w_router)
    e_flat = idx[:, :TOP_K].reshape(-1)
    g_flat = gate[:, :TOP_K].reshape(-1)
    n_assign = nt * TOP_K
    onehot = (e_flat[:, None] == jnp.arange(n_exp, dtype=jnp.int32)[None, :]).astype(jnp.int32)
    rank = jnp.cumsum(onehot, axis=0) - onehot
    pos = jnp.sum(rank * onehot, axis=1)
    counts = jnp.sum(onehot, axis=0)
    padded = (counts + tm - 1) // tm * tm
    pend = jnp.cumsum(padded)
    dest = ((pend - padded)[e_flat] + pos).astype(jnp.int32)
    n_rows = (n_assign // tm + n_exp) * tm
    tok = jnp.arange(n_assign, dtype=jnp.int32) // TOP_K
    src_tok = jnp.zeros((n_rows,), jnp.int32).at[dest].set(tok, unique_indices=True)
    row_gate = jnp.zeros((n_rows,), F32).at[dest].set(g_flat, unique_indices=True)
    block_expert = jnp.minimum(
        jnp.searchsorted(pend, jnp.arange(n_rows // tm, dtype=jnp.int32) * tm, side='right'), n_exp - 1
    ).astype(jnp.int32)
    x_rows = _gather_rows(x, src_tok)
    y_rows = _experts(x_rows, g, row_gate.reshape(n_rows, 1), block_expert, wg_bf16, wu_bf16, wd_bf16)
    dest2 = dest.reshape(nt, TOP_K)
    return _combine(x, y_rows, dest2[:, 0], dest2[:, 1])


def _norm_body(x_ref, g_ref, o_ref):
    o_ref[...] = _rms(x_ref[...], g_ref[...])


def _final_norm(x, g):
    nt, d = x.shape
    tm = ROW_TILE
    return pl.pallas_call(
        _norm_body,
        grid=(nt // tm,),
        in_specs=[pl.BlockSpec((tm, d), lambda i: (i, 0)), pl.BlockSpec((1, d), lambda i: (0, 0))],
        out_specs=pl.BlockSpec((tm, d), lambda i: (i, 0)),
        out_shape=jax.ShapeDtypeStruct((nt, d), F32),
        compiler_params=_params("parallel"),
        name="final_norm",
    )(x, g.reshape(1, d))


def kernel(x_prompt, x_sample, cache_k, cache_v, state_conv, page_table, rel_bias, norm_mix, w_in, w_gate,
           w_sgu, b_sgu, g_sgu, w_conv, w_branch, w_out, norm_ffn, w_ff_gate, w_ff_up, w_ff_down,
           w_router, w_exp_gate, w_exp_up, w_exp_down, norm_final):
    batch, seq, d = x_prompt.shape
    n_seq, t_new, _ = x_sample.shape
    depth = w_in.shape[0]
    n_p = batch * seq
    n_s = n_seq * t_new
    assert t_new == T_NEW and seq % MOBA_BLOCK == 0 and seq // MOBA_BLOCK <= LANES
    assert n_p % FFN_ROWS == 0 and n_s % ROW_TILE == 0 and (n_p + n_s) % FFN_ROWS == 0
    n_kb = seq // MOBA_BLOCK
    group_w = SGU_WIDTH // SGU_GROUPS

    x = jnp.concatenate([x_prompt.reshape(n_p, d), x_sample.reshape(n_s, d)], axis=0)
    bt = rel_bias.T.astype(F32)
    moba_tiles = _moba_bias_tiles(bt)
    far_bias = bt[:, REL_BUCKETS - 1]

    outs = {k: [] for k in ("kp", "vp", "ks", "vs", "cp", "cs", "sv")}
    for l in range(depth):
        q, k, v, uv, bch, kb, vb, km = _project(x, norm_mix[l], w_in[l].astype(BF16))

        kmean = jnp.pad(km[:n_p // MOBA_BLOCK, 0, :].reshape(batch, n_kb, QKV), ((0, 0), (0, LANES - n_kb), (0, 0)))
        o_a = _moba_prompt(q, kb, vb, kmean, moba_tiles, far_bias, batch, seq)
        o_c = _sb_prompt(q, kb, vb, batch, seq)
        ws = jnp.tril(w_sgu[l]).astype(BF16)
        bs = jnp.repeat(b_sgu[l].T, group_w, axis=1)
        o_b, o_d, z_p, _ = _mix(uv, bch, None, ws, bs, g_sgu[l], w_conv[l], seq // SGU_CHUNK, n_p)

        o_att = _attn_sample(q[n_p:], k[n_p:], v[n_p:], cache_k, cache_v, l, page_table, bt)
        per_chunk = SGU_CHUNK // T_NEW
        w4 = jnp.tril(w_sgu[l][:, :T_NEW, :T_NEW])
        ws_s = jnp.einsum('ab,gts->gatbs', jnp.eye(per_chunk, dtype=F32), w4)
        ws_s = ws_s.reshape(SGU_GROUPS, SGU_CHUNK, SGU_CHUNK).astype(BF16)
        bs_s = jnp.tile(jnp.repeat(b_sgu[l][:, :T_NEW].T, group_w, axis=1), (per_chunk, 1))
        prev = state_conv[l]
        zero = jnp.zeros((n_seq, 1, CONV_DIM), F32)
        p1 = jnp.concatenate([prev[:, 1:2], zero, zero, zero], axis=1).reshape(n_s, CONV_DIM)
        p2 = jnp.concatenate([prev[:, 0:1], prev[:, 1:2], zero, zero], axis=1).reshape(n_s, CONV_DIM)
        s_b, s_d, z_s, vn_s = _mix(uv[n_p:], bch[n_p:], (p1, p2), ws_s, bs_s, g_sgu[l], w_conv[l], None, n_s)

        prompt_br = (o_a, o_b, o_c, o_d)
        sample_br = (o_att[:, :QKV // 2], s_b, o_att[:, QKV // 2:], s_d)
        x = _merge(x, norm_mix[l], prompt_br, sample_br, w_gate[l].astype(BF16), w_branch[l].astype(BF16),
                   w_out[l].astype(BF16), n_p)

        i = l // 2
        if l % 2 == 0:
            x = _ffn(x, norm_ffn[l], w_ff_gate[i].astype(BF16), w_ff_up[i].astype(BF16), w_ff_down[i].astype(BF16))
        else:
            x = _moe(x, norm_ffn[l], w_router[i], w_exp_gate[i].astype(BF16), w_exp_up[i].astype(BF16),
                     w_exp_down[i].astype(BF16))

        outs["kp"].append(k[:n_p].reshape(batch, seq, N_HEADS, HEAD_DIM))
        outs["vp"].append(v[:n_p].reshape(batch, seq, N_HEADS, HEAD_DIM))
        outs["ks"].append(k[n_p:].reshape(n_seq, t_new, N_HEADS, HEAD_DIM))
        outs["vs"].append(v[n_p:].reshape(n_seq, t_new, N_HEADS, HEAD_DIM))
        outs["cp"].append(z_p.reshape(batch, seq, CONV_DIM)[:, seq - (CONV_WIDTH - 1):])
        outs["cs"].append(z_s.reshape(n_seq, t_new, CONV_DIM)[:, t_new - (CONV_WIDTH - 1):])
        outs["sv"].append(vn_s.reshape(n_seq, t_new, SGU_WIDTH))

    y = _final_norm(x, norm_final)
    return (y[:n_p].reshape(batch, seq, d), y[n_p:].reshape(n_seq, t_new, d),
            jnp.stack(outs["kp"]), jnp.stack(outs["vp"]), jnp.stack(outs["ks"]), jnp.stack(outs["vs"]),
            jnp.stack(outs["cp"]), jnp.stack(outs["cs"]), jnp.stack(outs["sv"]))
```

```python
import functools
import math

import numpy as np
import jax
import jax.numpy as jnp
from jax import lax
from jax.experimental import pallas as pl
from jax.experimental.pallas import tpu as pltpu

F32 = jnp.float32
BF16 = jnp.bfloat16

NORM_EPS = 1e-6
HEAD_DIM = 64
N_HEADS = 8
QKV = N_HEADS * HEAD_DIM
MOBA_BLOCK = 256
MOBA_TOPK = 3
Q_BLOCK = 128
SGU_CHUNK = 128
SGU_WIDTH = 256
SGU_GROUPS = 4
CONV_DIM = 256
CONV_WIDTH = 3
N_BRANCH = 4
BRANCH_DIM = 256
PAGE = 128
REL_BUCKETS = 32
REL_MAX_DIST = 128
TOP_K = 2
ATT_SCALE = HEAD_DIM ** -0.5
MASKED = -1e30
SB_DROP = -105.0
LANES = 128
ROW_TILE = 256
MOE_ROWS = 512
MOE_FF_TILE = 896
FFN_ROWS = 512
FFN_FF_TILE = 256
GATHER_ROWS = 256
V7X_VMEM_LIMIT = 56 * 1024 * 1024


def _params(*sem):
    return pltpu.CompilerParams(dimension_semantics=sem, vmem_limit_bytes=V7X_VMEM_LIMIT)


def _dot(a, b):
    return jnp.dot(a, b, preferred_element_type=F32)


def _dot_nt(a, b):
    return lax.dot_general(a, b, (((1,), (1,)), ((), ())), preferred_element_type=F32)


def _split_bf16(x):
    hi = x.astype(BF16)
    lo = (x - hi.astype(F32)).astype(BF16)
    return hi, lo


def _dot_nt_f32(a, b):
    ah, al = _split_bf16(a)
    bh, bl = _split_bf16(b)
    return _dot_nt(ah, bh) + _dot_nt(ah, bl) + _dot_nt(al, bh)


def _dot_f32(a, b):
    ah, al = _split_bf16(a)
    bh, bl = _split_bf16(b)
    return _dot(ah, bh) + _dot(ah, bl) + _dot(al, bh)


def _rms(x, g):
    return x * lax.rsqrt(jnp.mean(x * x, axis=-1, keepdims=True) + NORM_EPS) * g


def _gelu_tanh(x):
    return 0.5 * x * (1.0 + jnp.tanh(math.sqrt(2.0 / math.pi) * (x + 0.044715 * (x * x * x))))


def _sigmoid(x):
    return 1.0 / (1.0 + jnp.exp(-x))


def _softplus(z):
    return jnp.maximum(z, 0.0) + jnp.log1p(jnp.exp(-jnp.abs(z)))


def _suffix_sum(lk, u):
    n = lk.shape[0]
    hi, lo = _split_bf16(lk)
    r = _dot(jnp.concatenate([hi, lo], axis=0), u)
    return r[:n] + r[n:]


def _strict_upper(n):
    j = np.arange(n)[:, None]
    s = np.arange(n)[None, :]
    return jnp.asarray((j > s).astype(np.float32), dtype=BF16)


def _t5_bucket_table(max_dist):
    d = np.arange(max_dist + 1)
    max_exact = REL_BUCKETS // 2
    large = max_exact + (np.log(np.maximum(d, 1).astype(np.float32) / np.float32(max_exact))
                         / np.float32(math.log(REL_MAX_DIST / max_exact))
                         * np.float32(REL_BUCKETS - max_exact)).astype(np.int32)
    large = np.minimum(large, REL_BUCKETS - 1)
    return np.where(d < max_exact, d, large).astype(np.int32)


def _bias_by_dist(bt_rows, dist, valid):
    table = _t5_bucket_table(2 * REL_MAX_DIST)
    b = jnp.broadcast_to(bt_rows[:, 0:1], dist.shape)
    for k in range(1, REL_BUCKETS):
        first = int(np.argmax(table >= k))
        b = jnp.where(dist >= first, bt_rows[:, k:k + 1], b)
    return jnp.where(valid, b, MASKED)


def _proj_body(x_ref, g_ref, w_ref, q_ref, k_ref, v_ref, uv_ref, bch_ref, kb_ref, vb_ref, km_ref):
    h = _rms(x_ref[...], g_ref[...]).astype(BF16)

    def mm(lo, hi):
        return _dot(h, w_ref[:, lo:hi])

    q_ref[...] = mm(0, QKV)
    k = mm(QKV, 2 * QKV)
    k_ref[...] = k
    kb_ref[...] = k.astype(BF16)
    km_ref[...] = jnp.broadcast_to(jnp.mean(k, axis=0, keepdims=True), km_ref.shape)
    v = mm(2 * QKV, 3 * QKV)
    v_ref[...] = v
    vb_ref[...] = v.astype(BF16)
    uv_ref[...] = mm(3 * QKV, 3 * QKV + 2 * SGU_WIDTH)
    bch_ref[...] = mm(3 * QKV + 2 * SGU_WIDTH, 3 * QKV + 2 * SGU_WIDTH + 3 * CONV_DIM)


def _project(x, g, w_bf16):
    nt, d = x.shape
    in_dim = w_bf16.shape[1]
    tm = MOBA_BLOCK
    nb = nt // tm
    row = lambda i: (i, 0)
    return pl.pallas_call(
        _proj_body,
        grid=(nb,),
        in_specs=[pl.BlockSpec((tm, d), row),
                  pl.BlockSpec((1, d), lambda i: (0, 0)),
                  pl.BlockSpec((d, in_dim), lambda i: (0, 0))],
        out_specs=[pl.BlockSpec((tm, QKV), row), pl.BlockSpec((tm, QKV), row), pl.BlockSpec((tm, QKV), row),
                   pl.BlockSpec((tm, 2 * SGU_WIDTH), row), pl.BlockSpec((tm, 3 * CONV_DIM), row),
                   pl.BlockSpec((tm, QKV), row), pl.BlockSpec((tm, QKV), row),
                   pl.BlockSpec((None, 8, QKV), lambda i: (i, 0, 0))],
        out_shape=[jax.ShapeDtypeStruct((nt, QKV), F32), jax.ShapeDtypeStruct((nt, QKV), F32),
                   jax.ShapeDtypeStruct((nt, QKV), F32), jax.ShapeDtypeStruct((nt, 2 * SGU_WIDTH), F32),
                   jax.ShapeDtypeStruct((nt, 3 * CONV_DIM), F32),
                   jax.ShapeDtypeStruct((nt, QKV), BF16), jax.ShapeDtypeStruct((nt, QKV), BF16),
                   jax.ShapeDtypeStruct((nb, 8, QKV), F32)],
        compiler_params=_params("parallel"),
        name="proj",
    )(x, g.reshape(1, d), w_bf16)


def _top_blocks(gate, lane, lane_mask):
    lane = lane.astype(F32)
    g = jnp.where(lane_mask, gate, -jnp.inf)
    picks = []
    for _ in range(MOBA_TOPK):
        m = jnp.max(g, axis=-1, keepdims=True)
        idx = jnp.min(jnp.where(g == m, lane, 1e9), axis=-1, keepdims=True)
        picks.append((idx, m > -jnp.inf))
        g = jnp.where(lane == idx, -jnp.inf, g)
    return picks


def _is_picked(picks, j):
    j = jnp.asarray(j).astype(F32)
    c = (picks[0][0] == j) & picks[0][1]
    for idx, ok in picks[1:]:
        c = c | ((idx == j) & ok)
    return c


def _moba_prompt_body(fb_ref, q_ref, k_ref, v_ref, km_ref, bias_ref, spread_ref, o_ref, *, chunk):
    pr = pl.program_id(1)
    qi = pl.program_id(2)
    cur = qi // 2
    half = qi % 2
    rows = 2 * Q_BLOCK
    lane = lax.broadcasted_iota(jnp.int32, (Q_BLOCK, LANES), 1)
    q = q_ref[...] * ATT_SCALE
    qs = jnp.concatenate([jnp.where(lane < HEAD_DIM, q, 0.0), jnp.where(lane >= HEAD_DIM, q, 0.0)], axis=0)
    qsb = qs.astype(BF16)
    lane2 = lax.broadcasted_iota(jnp.int32, (rows, LANES), 1)
    picks = _top_blocks(_dot_nt_f32(qs, km_ref[...]), lane2, lane2 < cur)
    first_head = lax.broadcasted_iota(jnp.int32, (rows, 1), 0) < Q_BLOCK
    far_bias = jnp.where(first_head, fb_ref[2 * pr], fb_ref[2 * pr + 1])

    start = pl.multiple_of(cur * MOBA_BLOCK, MOBA_BLOCK)
    start_p = pl.multiple_of(jnp.maximum(cur - 1, 0) * MOBA_BLOCK, MOBA_BLOCK)
    s_own = _dot_nt(qsb, k_ref[pl.ds(start, MOBA_BLOCK), :]) + bias_ref[half]
    s_prev = _dot_nt(qsb, k_ref[pl.ds(start_p, MOBA_BLOCK), :]) + bias_ref[2 + half]
    s_prev = jnp.where(_is_picked(picks, cur - 1), s_prev, MASKED)
    m = jnp.maximum(jnp.max(s_own, axis=-1, keepdims=True), jnp.max(s_prev, axis=-1, keepdims=True))
    p_own = jnp.exp(s_own - m)
    p_prev = jnp.exp(s_prev - m)
    l = jnp.sum(p_own, axis=-1, keepdims=True) + jnp.sum(p_prev, axis=-1, keepdims=True)
    acc = (_dot(p_own.astype(BF16), v_ref[pl.ds(start, MOBA_BLOCK), :])
           + _dot(p_prev.astype(BF16), v_ref[pl.ds(start_p, MOBA_BLOCK), :]))

    lane2f = lane2.astype(F32)
    picked = (lane2f == picks[0][0]) & picks[0][1]
    for idx, ok in picks[1:]:
        picked = picked | ((lane2f == idx) & ok)
    add_rows = jnp.where(picked & (lane2 < cur - 1), 0.0, MASKED).astype(BF16)

    def chunk_step(c, carry):
        m, l, acc = carry
        start = pl.multiple_of(c * (chunk * MOBA_BLOCK), chunk * MOBA_BLOCK)
        s = _dot_nt(qsb, k_ref[pl.ds(start, chunk * MOBA_BLOCK), :]) + far_bias + _dot(add_rows, spread_ref[c])
        m_new = jnp.maximum(m, jnp.max(s, axis=-1, keepdims=True))
        alpha = jnp.exp(m - m_new)
        p = jnp.exp(s - m_new)
        l = alpha * l + jnp.sum(p, axis=-1, keepdims=True)
        acc = alpha * acc + _dot(p.astype(BF16), v_ref[pl.ds(start, chunk * MOBA_BLOCK), :])
        return m_new, l, acc

    n_older = jnp.maximum(cur - 1, 0)
    m, l, acc = lax.fori_loop(0, (n_older + chunk - 1) // chunk, chunk_step, (m, l, acc))
    out = acc / l
    o_ref[...] = jnp.where(lane < HEAD_DIM, out[:Q_BLOCK], out[Q_BLOCK:])


def _moba_bias_tiles(bt):
    n_pairs = bt.shape[0] // 2
    shape = (n_pairs * 2 * Q_BLOCK, MOBA_BLOCK)
    i = lax.broadcasted_iota(jnp.int32, shape, 0) & (Q_BLOCK - 1)
    c = lax.broadcasted_iota(jnp.int32, shape, 1)
    bt_rows = jnp.repeat(bt, Q_BLOCK, axis=0)
    tiles = []
    for delta in (0, Q_BLOCK, MOBA_BLOCK, MOBA_BLOCK + Q_BLOCK):
        dist = delta + i - c
        tiles.append(_bias_by_dist(bt_rows, dist, dist >= 0).reshape(n_pairs, 2 * Q_BLOCK, MOBA_BLOCK))
    return jnp.stack(tiles, axis=1)


def _moba_prompt(q, kb, vb, kmean, bias_tiles, far_bias, batch, seq):
    nq = seq // Q_BLOCK
    n_kb = seq // MOBA_BLOCK
    chunk = next(c for c in (4, 2, 1) if n_kb % c == 0)
    blk = np.arange(LANES)[None, :, None]
    key = np.arange(chunk * MOBA_BLOCK)[None, None, :]
    cid = np.arange(n_kb // chunk)[:, None, None]
    spread = jnp.asarray((blk == cid * chunk + key // MOBA_BLOCK).astype(np.float32), dtype=BF16)
    return pl.pallas_call(
        functools.partial(_moba_prompt_body, chunk=chunk),
        grid_spec=pltpu.PrefetchScalarGridSpec(
            num_scalar_prefetch=1,
            grid=(batch, 2, nq),
            in_specs=[pl.BlockSpec((Q_BLOCK, LANES), lambda b, p, i, fb: (b * nq + i, p)),
                      pl.BlockSpec((seq, LANES), lambda b, p, i, fb: (b, p)),
                      pl.BlockSpec((seq, LANES), lambda b, p, i, fb: (b, p)),
                      pl.BlockSpec((None, LANES, LANES), lambda b, p, i, fb: (b, 0, p)),
                      pl.BlockSpec((None, 4, 2 * Q_BLOCK, MOBA_BLOCK), lambda b, p, i, fb: (p, 0, 0, 0)),
                      pl.BlockSpec(spread.shape, lambda b, p, i, fb: (0, 0, 0))],
            out_specs=pl.BlockSpec((Q_BLOCK, LANES), lambda b, p, i, fb: (b * nq + i, p)),
        ),
        out_shape=jax.ShapeDtypeStruct((batch * seq, 2 * LANES), F32),
        compiler_params=_params("parallel", "parallel", "arbitrary"),
        name="moba_prompt",
    )(far_bias, q, kb, vb, kmean, bias_tiles, spread)


def _sb_prompt_body(q_ref, k_ref, v_ref, u_ref, o_ref):
    qi = pl.program_id(2)
    rows = 2 * Q_BLOCK
    lane = lax.broadcasted_iota(jnp.int32, (Q_BLOCK, LANES), 1)
    q = q_ref[...] * ATT_SCALE
    qsb = jnp.concatenate([jnp.where(lane < HEAD_DIM, q, 0.0), jnp.where(lane >= HEAD_DIM, q, 0.0)],
                          axis=0).astype(BF16)
    u = u_ref[...]
    col = lax.broadcasted_iota(jnp.int32, (rows, Q_BLOCK), 1)
    qrow = lax.broadcasted_iota(jnp.int32, (rows, Q_BLOCK), 0) & (Q_BLOCK - 1)

    def tile(j, keep, r_acc, acc):
        start = pl.multiple_of(jnp.maximum(j, 0) * Q_BLOCK, Q_BLOCK)
        z = _dot_nt(qsb, k_ref[pl.ds(start, Q_BLOCK), :])
        sp = _softplus(z)
        lk = -sp if keep is None else jnp.where(keep, -sp, 0.0)
        a = jnp.exp(z - sp + _suffix_sum(lk, u) + r_acc)
        if keep is not None:
            a = jnp.where(keep, a, 0.0)
        acc = acc + _dot(a.astype(BF16), v_ref[pl.ds(start, Q_BLOCK), :])
        return r_acc + jnp.sum(lk, axis=-1, keepdims=True), acc

    def exists(j):
        return jnp.broadcast_to(j >= 0, (rows, Q_BLOCK))

    r_acc, acc = tile(qi, col < qrow, jnp.zeros((rows, 1), F32), jnp.zeros((rows, LANES), F32))
    r_acc, acc = tile(qi - 1, exists(qi - 1), r_acc, acc)
    r_acc, acc = tile(qi - 2, exists(qi - 2), r_acc, acc)

    def more(state):
        it, go, _, _ = state
        return (qi - 3 - 2 * it >= 0) & (go > 0)

    def older(state):
        it, _, r_acc, acc = state
        j = qi - 3 - 2 * it
        r_acc, acc = tile(j, None, r_acc, acc)
        r_acc, acc = tile(j - 1, exists(j - 1), r_acc, acc)
        return it + 1, (jnp.max(r_acc) > SB_DROP).astype(jnp.int32), r_acc, acc

    go = (jnp.max(r_acc) > SB_DROP).astype(jnp.int32)
    _, _, _, acc = lax.while_loop(more, older, (jnp.int32(0), go, r_acc, acc))
    o_ref[...] = jnp.where(lane < HEAD_DIM, acc[:Q_BLOCK], acc[Q_BLOCK:])


def _sb_prompt(q, kb, vb, batch, seq):
    nq = seq // Q_BLOCK
    return pl.pallas_call(
        _sb_prompt_body,
        grid=(batch, 2, nq),
        in_specs=[pl.BlockSpec((Q_BLOCK, LANES), lambda b, p, i: (b * nq + i, 2 + p)),
                  pl.BlockSpec((seq, LANES), lambda b, p, i: (b, 2 + p)),
                  pl.BlockSpec((seq, LANES), lambda b, p, i: (b, 2 + p)),
                  pl.BlockSpec((Q_BLOCK, Q_BLOCK), lambda b, p, i: (0, 0))],
        out_specs=pl.BlockSpec((Q_BLOCK, LANES), lambda b, p, i: (b * nq + i, p)),
        out_shape=jax.ShapeDtypeStruct((batch * seq, 2 * LANES), F32),
        compiler_params=_params("parallel", "parallel", "arbitrary"),
        name="sb_prompt",
    )(q, kb, vb, _strict_upper(Q_BLOCK))


T_NEW = 4
SOFTMAX_ROWS = 4 * T_NEW


def _attn_sample_body(pt_ref, qx_ref, knew_ref, vnew_ref, bias_ref, bias_new_ref, u_ref, *rest, n_pages):
    del pt_ref
    k_refs = rest[:n_pages]
    v_refs = rest[n_pages:2 * n_pages]
    o_ref = rest[2 * n_pages]
    n_blocks = n_pages // 2
    sr = SOFTMAX_ROWS
    n_rows = N_HEADS * T_NEW
    past = n_pages * PAGE

    qx = qx_ref[...]
    qxb = qx.astype(BF16)

    blk_lane = lax.broadcasted_iota(jnp.int32, (QKV, LANES), 1)
    km = jnp.zeros((QKV, LANES), F32)
    parts = []
    for p in range(n_pages):
        kp = k_refs[p][...].reshape(QKV, PAGE)
        parts.append(_dot(qxb, kp.astype(BF16)))
        if p % 2 == 0:
            ksum = kp
        else:
            mean = jnp.sum(ksum + kp, axis=1, keepdims=True) * (1.0 / MOBA_BLOCK)
            km = jnp.where(blk_lane == p // 2, mean, km)
    s = jnp.concatenate(parts, axis=1)

    pad_rows = jnp.zeros((LANES - 8, QKV), F32)
    knew = jnp.concatenate([knew_ref[...], pad_rows], axis=0).astype(BF16)
    vnew = jnp.concatenate([vnew_ref[...], pad_rows], axis=0).astype(BF16)
    s_new = _dot_nt(qxb, knew)

    lane_s = lax.broadcasted_iota(jnp.int32, (sr, LANES), 1)
    picks = _top_blocks(_dot_f32(qx[:sr], km), lane_s, lane_s < n_blocks)
    key_block = (lax.broadcasted_iota(jnp.int32, (sr, past), 1) >> 8).astype(F32)
    la = jnp.where(_is_picked(picks, key_block), s[:sr] + bias_ref[...], MASKED)
    la_new = s_new[:sr] + bias_new_ref[...]
    m = jnp.maximum(jnp.max(la, axis=-1, keepdims=True), jnp.max(la_new, axis=-1, keepdims=True))
    pa = jnp.exp(la - m)
    pa_new = jnp.exp(la_new - m)
    denom = jnp.sum(pa, axis=-1, keepdims=True) + jnp.sum(pa_new, axis=-1, keepdims=True)

    u = u_ref[...]
    zb_new = s_new[sr:]
    keep_new = lane_s < (lax.broadcasted_iota(jnp.int32, (sr, LANES), 0) & (T_NEW - 1))
    sp_new = _softplus(zb_new)
    lk_new = jnp.where(keep_new, -sp_new, 0.0)
    a_new = jnp.where(keep_new, jnp.exp(zb_new - sp_new + _suffix_sum(lk_new, u)), 0.0)
    zb = s[sr:]
    sp = _softplus(zb)
    lk = -sp
    stacked = jnp.concatenate([lk[:, p * PAGE:(p + 1) * PAGE] for p in range(n_pages)], axis=0)
    within = _suffix_sum(stacked, u)
    after = jnp.sum(lk_new, axis=-1, keepdims=True)
    cols = [None] * n_pages
    for p in range(n_pages - 1, -1, -1):
        cols[p] = within[p * sr:(p + 1) * sr] + after
        after = after + jnp.sum(lk[:, p * PAGE:(p + 1) * PAGE], axis=-1, keepdims=True)
    a = jnp.exp(zb - sp + jnp.concatenate(cols, axis=1))

    w = jnp.concatenate([pa, a], axis=0).astype(BF16)
    w_new = jnp.concatenate([pa_new, a_new], axis=0).astype(BF16)
    o = _dot(w_new, vnew)
    for p in range(n_pages):
        o = o + _dot_nt(w[:, p * PAGE:(p + 1) * PAGE], v_refs[p][...].reshape(QKV, PAGE).astype(BF16))
    row_head = lax.broadcasted_iota(jnp.int32, (n_rows, QKV), 0) >> 2
    col_head = lax.broadcasted_iota(jnp.int32, (n_rows, QKV), 1) >> 6
    o = jnp.where(row_head == col_head, o, 0.0)
    out = o[:, 0:HEAD_DIM]
    for h in range(1, N_HEADS):
        out = out + o[:, h * HEAD_DIM:(h + 1) * HEAD_DIM]
    scale = jnp.concatenate([1.0 / denom, jnp.ones((n_rows - sr, 1), F32)], axis=0)
    o_ref[...] = out * scale


def _attn_sample(q_s, k_s, v_s, cache_k, cache_v, layer, page_table, bt):
    n_seq, n_pages = page_table.shape
    assert q_s.shape[0] == n_seq * T_NEW and n_pages % 2 == 0
    past = n_pages * PAGE
    n_rows = N_HEADS * T_NEW
    q4 = (q_s * ATT_SCALE).reshape(n_seq, T_NEW, N_HEADS, 1, HEAD_DIM).transpose(0, 2, 1, 3, 4)
    eye = jnp.eye(N_HEADS, dtype=F32).reshape(1, N_HEADS, 1, N_HEADS, 1)
    qx = (q4 * eye).reshape(n_seq, n_rows, QKV)
    pad = ((0, 0), (0, 8 - T_NEW), (0, 0))
    knew = jnp.pad(k_s.reshape(n_seq, T_NEW, QKV), pad)
    vnew = jnp.pad(v_s.reshape(n_seq, T_NEW, QKV), pad)
    ck = cache_k.transpose(0, 1, 3, 4, 2)
    cv = cache_v.transpose(0, 1, 3, 4, 2)

    bt_rows = jnp.repeat(bt[:4], T_NEW, axis=0)

    def key_bias(width, first_key_dist, new_keys):
        r = lax.broadcasted_iota(jnp.int32, (SOFTMAX_ROWS, width), 0)
        c = lax.broadcasted_iota(jnp.int32, (SOFTMAX_ROWS, width), 1)
        dist = first_key_dist + (r & (T_NEW - 1)) - c
        valid = (dist >= 0) & (c < T_NEW) if new_keys else dist >= 0
        return _bias_by_dist(bt_rows, dist, valid)

    bias_past = key_bias(past, past, False)
    bias_new = key_bias(LANES, 0, True)

    def page_spec(p):
        return pl.BlockSpec((None, None, N_HEADS, HEAD_DIM, PAGE),
                            lambda b, pt, p=p: (layer, pt[b * n_pages + p], 0, 0, 0))

    seq3 = lambda b, pt: (b, 0, 0)
    const2 = lambda b, pt: (0, 0)
    out = pl.pallas_call(
        functools.partial(_attn_sample_body, n_pages=n_pages),
        grid_spec=pltpu.PrefetchScalarGridSpec(
            num_scalar_prefetch=1,
            grid=(n_seq,),
            in_specs=[pl.BlockSpec((None, n_rows, QKV), seq3),
                      pl.BlockSpec((None, 8, QKV), seq3),
                      pl.BlockSpec((None, 8, QKV), seq3),
                      pl.BlockSpec((SOFTMAX_ROWS, past), const2),
                      pl.BlockSpec((SOFTMAX_ROWS, LANES), const2),
                      pl.BlockSpec((PAGE, PAGE), const2)]
                     + [page_spec(p) for p in range(n_pages)] * 2,
            out_specs=pl.BlockSpec((None, n_rows, HEAD_DIM), seq3),
        ),
        out_shape=jax.ShapeDtypeStruct((n_seq, n_rows, HEAD_DIM), F32),
        compiler_params=_params("parallel"),
        name="attn_sample",
    )(page_table.reshape(-1), qx, knew, vnew, bias_past, bias_new, _strict_upper(PAGE),
      *([ck] * n_pages), *([cv] * n_pages))
    return out.reshape(n_seq, N_HEADS, T_NEW, HEAD_DIM).transpose(0, 2, 1, 3).reshape(n_seq * T_NEW, QKV)


def _mix_body(*refs, chunks_per_seq):
    if chunks_per_seq is None:
        uv_ref, bch_ref, p1_ref, p2_ref, ws_ref, bs_ref, gs_ref, wc_ref, ob_ref, od_ref, z_ref, vn_ref = refs
    else:
        uv_ref, bch_ref, prev_ref, ws_ref, bs_ref, gs_ref, wc_ref, ob_ref, od_ref, z_ref, vn_ref = refs
    uv = uv_ref[...]
    x = _gelu_tanh(uv[:, SGU_WIDTH:])
    xc = x - jnp.mean(x, axis=-1, keepdims=True)
    vn = xc * lax.rsqrt(jnp.mean(xc * xc, axis=-1, keepdims=True) + NORM_EPS) * gs_ref[...]
    vn_ref[...] = vn
    vnb = vn.astype(BF16)
    group = lax.broadcasted_iota(jnp.int32, (SGU_CHUNK, SGU_WIDTH), 1) >> 6
    s = jnp.zeros((SGU_CHUNK, SGU_WIDTH), F32)
    for g in range(SGU_GROUPS):
        s = jnp.where(group == g, _dot(ws_ref[g], vnb), s)
    ob_ref[...] = _gelu_tanh(uv[:, :SGU_WIDTH]) * (s + bs_ref[...])

    bch = bch_ref[...]
    z = bch[:, CONV_DIM:2 * CONV_DIM] * bch[:, 2 * CONV_DIM:]
    z_ref[...] = z
    row = lax.broadcasted_iota(jnp.int32, (SGU_CHUNK, CONV_DIM), 0)
    r1 = pltpu.roll(z, 1, 0)
    r2 = pltpu.roll(z, 2, 0)
    if chunks_per_seq is None:
        rr = row & (T_NEW - 1)
        z1 = jnp.where(rr >= 1, r1, p1_ref[...])
        z2 = jnp.where(rr >= 2, r2, p2_ref[...])
    else:
        pb = prev_ref[...]
        zp = pb[:, CONV_DIM:2 * CONV_DIM] * pb[:, 2 * CONV_DIM:]
        zp = jnp.where(pl.program_id(0) % chunks_per_seq == 0, 0.0, zp)
        z1 = jnp.where(row >= 1, r1, zp[7:8, :])
        z2 = jnp.where(row >= 2, r2, jnp.where(row == 1, zp[7:8, :], zp[6:7, :]))
    wc = wc_ref[...]
    od_ref[...] = bch[:, :CONV_DIM] * (wc[0:1, :] * z2 + wc[1:2, :] * z1 + wc[2:3, :] * z)


def _mix(uv, bch, history, ws_bf16, bs, gs, wc, chunks_per_seq, n_rows):
    nc = n_rows // SGU_CHUNK
    row = lambda i: (i, 0)
    const = lambda i: (0, 0)
    if chunks_per_seq is None:
        hist_specs = [pl.BlockSpec((SGU_CHUNK, CONV_DIM), row)] * 2
        hist = list(history)
    else:
        sub = SGU_CHUNK // 8
        hist_specs = [pl.BlockSpec((8, 3 * CONV_DIM), lambda i: (jnp.maximum(i * sub - 1, 0), 0))]
        hist = [bch]
    return pl.pallas_call(
        functools.partial(_mix_body, chunks_per_seq=chunks_per_seq),
        grid=(nc,),
        in_specs=[pl.BlockSpec((SGU_CHUNK, 2 * SGU_WIDTH), row), pl.BlockSpec((SGU_CHUNK, 3 * CONV_DIM), row)]
                 + hist_specs
                 + [pl.BlockSpec((SGU_GROUPS, SGU_CHUNK, SGU_CHUNK), lambda i: (0, 0, 0)),
                    pl.BlockSpec((SGU_CHUNK, SGU_WIDTH), const),
                    pl.BlockSpec((1, SGU_WIDTH), const),
                    pl.BlockSpec((8, CONV_DIM), const)],
        out_specs=[pl.BlockSpec((SGU_CHUNK, SGU_WIDTH), row)] * 4,
        out_shape=[jax.ShapeDtypeStruct((n_rows, SGU_WIDTH), F32)] * 4,
        compiler_params=_params("parallel"),
        name="mix_sample" if chunks_per_seq is None else "mix_prompt",
    )(uv, bch, *hist, ws_bf16, bs, gs.reshape(1, SGU_WIDTH), jnp.pad(wc, ((0, 8 - CONV_WIDTH), (0, 0))))


def _merge_body(x_ref, g_ref, pa_ref, pb_ref, pc_ref, pd_ref, sa_ref, sb_ref, sc_ref, sd_ref,
                wg_ref, wb_ref, wo_ref, o_ref, *, n_prompt_tiles):
    x = x_ref[...]
    d = x.shape[1]
    h = _rms(x, g_ref[...]).astype(BF16)
    is_prompt = pl.program_id(0) < n_prompt_tiles
    merged = jnp.zeros(x.shape, F32)
    for n, (p_ref, s_ref) in enumerate(((pa_ref, sa_ref), (pb_ref, sb_ref), (pc_ref, sc_ref), (pd_ref, sd_ref))):
        br = jnp.where(is_prompt, p_ref[...], s_ref[...]).astype(BF16)
        up = _dot(br, wb_ref[n])
        gate = _sigmoid(_dot(h, wg_ref[:, n * d:(n + 1) * d]))
        merged = merged + gate * up
    o_ref[...] = x + _dot(merged.astype(BF16), wo_ref[...])


def _merge(x, g, prompt_br, sample_br, wg_bf16, wb_bf16, wo_bf16, n_prompt):
    nt, d = x.shape
    tm = ROW_TILE
    npt = n_prompt // tm
    row = lambda i: (i, 0)
    p_spec = pl.BlockSpec((tm, BRANCH_DIM), lambda i: (jnp.minimum(i, npt - 1), 0))
    s_spec = pl.BlockSpec((tm, BRANCH_DIM), lambda i: (jnp.maximum(i - npt, 0), 0))
    return pl.pallas_call(
        functools.partial(_merge_body, n_prompt_tiles=npt),
        grid=(nt // tm,),
        in_specs=[pl.BlockSpec((tm, d), row), pl.BlockSpec((1, d), lambda i: (0, 0))]
                 + [p_spec] * N_BRANCH + [s_spec] * N_BRANCH
                 + [pl.BlockSpec((d, N_BRANCH * d), lambda i: (0, 0)),
                    pl.BlockSpec((N_BRANCH, BRANCH_DIM, d), lambda i: (0, 0, 0)),
                    pl.BlockSpec((d, d), lambda i: (0, 0))],
        out_specs=pl.BlockSpec((tm, d), row),
        out_shape=jax.ShapeDtypeStruct((nt, d), F32),
        compiler_params=_params("parallel"),
        name="merge",
    )(x, g.reshape(1, d), *prompt_br, *sample_br, wg_bf16, wb_bf16, wo_bf16)


def _silu(x):
    return x * _sigmoid(x)


def _ffn_body(x_ref, g_ref, wg_ref, wu_ref, wd_ref, o_ref, h_ref, acc_ref):
    j = pl.program_id(1)

    @pl.when(j == 0)
    def _():
        h_ref[...] = _rms(x_ref[...], g_ref[...]).astype(BF16)
        acc_ref[...] = jnp.zeros_like(acc_ref)

    h = h_ref[...]
    a = _silu(_dot(h, wg_ref[...])) * _dot(h, wu_ref[...])
    acc_ref[...] += _dot(a.astype(BF16), wd_ref[...])

    @pl.when(j == pl.num_programs(1) - 1)
    def _():
        o_ref[...] = x_ref[...] + acc_ref[...]


def _ffn(x, g, wg_bf16, wu_bf16, wd_bf16):
    nt, d = x.shape
    dff = wg_bf16.shape[1]
    tm, tf = FFN_ROWS, FFN_FF_TILE
    return pl.pallas_call(
        _ffn_body,
        grid=(nt // tm, dff // tf),
        in_specs=[pl.BlockSpec((tm, d), lambda i, j: (i, 0)), pl.BlockSpec((1, d), lambda i, j: (0, 0)),
                  pl.BlockSpec((d, tf), lambda i, j: (0, j)), pl.BlockSpec((d, tf), lambda i, j: (0, j)),
                  pl.BlockSpec((tf, d), lambda i, j: (j, 0))],
        out_specs=pl.BlockSpec((tm, d), lambda i, j: (i, 0)),
        out_shape=jax.ShapeDtypeStruct((nt, d), F32),
        scratch_shapes=[pltpu.VMEM((tm, d), BF16), pltpu.VMEM((tm, d), F32)],
        compiler_params=_params("parallel", "arbitrary"),
        name="ffn",
    )(x, g.reshape(1, d), wg_bf16, wu_bf16, wd_bf16)


def _route_body(x_ref, g_ref, wr_ref, idx_ref, gate_ref, *, n_exp):
    h = _rms(x_ref[...], g_ref[...])
    logits = _dot_nt_f32(h, wr_ref[...])
    lane_i = lax.broadcasted_iota(jnp.int32, logits.shape, 1)
    lane = lane_i.astype(F32)
    lg = jnp.where(lane_i < n_exp, logits, -jnp.inf)
    m1 = jnp.max(lg, axis=-1, keepdims=True)
    i1 = jnp.min(jnp.where(lg == m1, lane, float(LANES)), axis=-1, keepdims=True)
    lg2 = jnp.where(lane == i1, -jnp.inf, lg)
    m2 = jnp.max(lg2, axis=-1, keepdims=True)
    i2 = jnp.min(jnp.where(lg2 == m2, lane, float(LANES)), axis=-1, keepdims=True)
    e2 = jnp.exp(m2 - m1)
    idx_ref[...] = jnp.where(lane_i == 0, i1, jnp.where(lane_i == 1, i2, 0.0)).astype(jnp.int32)
    gate_ref[...] = jnp.where(lane_i == 0, 1.0 / (1.0 + e2), jnp.where(lane_i == 1, e2 / (1.0 + e2), 0.0))


def _route(x, g, w_router):
    nt, d = x.shape
    n_exp = w_router.shape[1]
    assert n_exp <= LANES
    wr = jnp.pad(w_router.T, ((0, LANES - n_exp), (0, 0)))
    tm = ROW_TILE
    row = lambda i: (i, 0)
    return pl.pallas_call(
        functools.partial(_route_body, n_exp=n_exp),
        grid=(nt // tm,),
        in_specs=[pl.BlockSpec((tm, d), row), pl.BlockSpec((1, d), lambda i: (0, 0)),
                  pl.BlockSpec((LANES, d), lambda i: (0, 0))],
        out_specs=[pl.BlockSpec((tm, LANES), row)] * 2,
        out_shape=[jax.ShapeDtypeStruct((nt, LANES), jnp.int32), jax.ShapeDtypeStruct((nt, LANES), F32)],
        compiler_params=_params("parallel"),
        name="route",
    )(x, g.reshape(1, d), wr)


def _expert_body(be_ref, src_ref, x_hbm, g_ref, rg_ref, wg_ref, wu_ref, wd_ref, o_ref,
                 xbuf, sem, h_ref, acc_ref):
    del be_ref
    i = pl.program_id(0)
    j = pl.program_id(1)
    nb = pl.num_programs(0)
    nf = pl.num_programs(1)
    tm = xbuf.shape[1]
    per_step = tm // nf

    def row_copy(block, r, slot):
        tok = src_ref[block * tm + r]
        return pltpu.make_async_copy(x_hbm.at[pl.ds(tok, 1), :], xbuf.at[slot, pl.ds(r, 1), :], sem.at[slot])

    def wait_block(slot):
        def body(r, c):
            row_copy(0, r, slot).wait()
            return c
        lax.fori_loop(0, tm, body, 0)

    @pl.when(j == 0)
    def _():
        @pl.when(i == 0)
        def _():
            def body(r, c):
                row_copy(0, r, 0).start()
                return c
            lax.fori_loop(0, tm, body, 0)

        wait_block(i % 2)
        h_ref[...] = _rms(xbuf[i % 2], g_ref[...]).astype(BF16)
        acc_ref[...] = jnp.zeros_like(acc_ref)

    nxt = jnp.minimum(i + 1, nb - 1)
    for r in range(per_step):
        row_copy(nxt, j * per_step + r, (i + 1) % 2).start()

    h = h_ref[...]
    a = _silu(_dot(h, wg_ref[...])) * _dot(h, wu_ref[...])
    acc_ref[...] += _dot(a.astype(BF16), wd_ref[...])

    @pl.when(j == nf - 1)
    def _():
        o_ref[...] = acc_ref[...] * rg_ref[...]

        @pl.when(i == nb - 1)
        def _():
            wait_block((i + 1) % 2)


def _experts(x, src_tok, g, row_gate, block_expert, wg_bf16, wu_bf16, wd_bf16):
    r = src_tok.shape[0]
    d = x.shape[1]
    dff = wg_bf16.shape[2]
    tm, tf = MOE_ROWS, MOE_FF_TILE
    assert tm % (dff // tf) == 0
    return pl.pallas_call(
        _expert_body,
        grid_spec=pltpu.PrefetchScalarGridSpec(
            num_scalar_prefetch=2,
            grid=(r // tm, dff // tf),
            in_specs=[pl.BlockSpec(memory_space=pl.ANY),
                      pl.BlockSpec((1, d), lambda i, j, be, st: (0, 0)),
                      pl.BlockSpec((tm, 1), lambda i, j, be, st: (i, 0)),
                      pl.BlockSpec((None, d, tf), lambda i, j, be, st: (be[i], 0, j)),
                      pl.BlockSpec((None, d, tf), lambda i, j, be, st: (be[i], 0, j)),
                      pl.BlockSpec((None, tf, d), lambda i, j, be, st: (be[i], j, 0))],
            out_specs=pl.BlockSpec((tm, d), lambda i, j, be, st: (i, 0)),
            scratch_shapes=[pltpu.VMEM((2, tm, d), F32), pltpu.SemaphoreType.DMA((2,)),
                            pltpu.VMEM((tm, d), BF16), pltpu.VMEM((tm, d), F32)],
        ),
        out_shape=jax.ShapeDtypeStruct((r, d), F32),
        compiler_params=_params("arbitrary", "arbitrary"),
        name="experts",
    )(block_expert, src_tok, x, g.reshape(1, d), row_gate, wg_bf16, wu_bf16, wd_bf16)


def _combine_body(d0_ref, d1_ref, x_ref, g_ref, src_ref, o_ref, buf, sem, *, final_norm):
    i = pl.program_id(0)
    n = pl.num_programs(0)

    def copies(block, r, slot):
        t = block * GATHER_ROWS + r
        return (pltpu.make_async_copy(src_ref.at[pl.ds(d0_ref[t], 1), :], buf.at[slot, 0, pl.ds(r, 1), :],
                                      sem.at[slot]),
                pltpu.make_async_copy(src_ref.at[pl.ds(d1_ref[t], 1), :], buf.at[slot, 1, pl.ds(r, 1), :],
                                      sem.at[slot]))

    def issue(block, slot):
        def body(r, c):
            for cp in copies(block, r, slot):
                cp.start()
            return c
        lax.fori_loop(0, GATHER_ROWS, body, 0, unroll=4)

    @pl.when(i == 0)
    def _():
        issue(0, 0)

    @pl.when(i + 1 < n)
    def _():
        issue(i + 1, (i + 1) % 2)

    def wait(r, c):
        for cp in copies(0, r, i % 2):
            cp.wait()
        return c

    lax.fori_loop(0, GATHER_ROWS, wait, 0, unroll=4)
    y = x_ref[...] + buf[i % 2, 0] + buf[i % 2, 1]
    o_ref[...] = _rms(y, g_ref[...]) if final_norm else y


def _combine(x, expert_rows, d0, d1, final_g=None):
    nt, d = x.shape
    g = jnp.ones((d,), F32) if final_g is None else final_g
    return pl.pallas_call(
        functools.partial(_combine_body, final_norm=final_g is not None),
        grid_spec=pltpu.PrefetchScalarGridSpec(
            num_scalar_prefetch=2,
            grid=(nt // GATHER_ROWS,),
            in_specs=[pl.BlockSpec((GATHER_ROWS, d), lambda i, a, b: (i, 0)),
                      pl.BlockSpec((1, d), lambda i, a, b: (0, 0)),
                      pl.BlockSpec(memory_space=pl.ANY)],
            out_specs=pl.BlockSpec((GATHER_ROWS, d), lambda i, a, b: (i, 0)),
            scratch_shapes=[pltpu.VMEM((2, 2, GATHER_ROWS, d), F32), pltpu.SemaphoreType.DMA((2,))],
        ),
        out_shape=jax.ShapeDtypeStruct((nt, d), F32),
        compiler_params=_params("arbitrary"),
        name="combine",
    )(d0, d1, x, g.reshape(1, d), expert_rows)


def _moe(x, g, w_router, wg_bf16, wu_bf16, wd_bf16, final_g=None):
    nt, d = x.shape
    n_exp = w_router.shape[1]
    tm = MOE_ROWS
    idx, gate = _route(x, g, w_router)
    e_flat = idx[:, :TOP_K].reshape(-1)
    g_flat = gate[:, :TOP_K].reshape(-1)
    n_assign = nt * TOP_K
    onehot = (e_flat[:, None] == jnp.arange(n_exp, dtype=jnp.int32)[None, :]).astype(jnp.int32)
    rank = jnp.cumsum(onehot, axis=0) - onehot
    pos = jnp.sum(rank * onehot, axis=1)
    counts = jnp.sum(onehot, axis=0)
    padded = (counts + tm - 1) // tm * tm
    pend = jnp.cumsum(padded)
    dest = ((pend - padded)[e_flat] + pos).astype(jnp.int32)
    n_rows = (n_assign // tm + n_exp) * tm
    tok = jnp.arange(n_assign, dtype=jnp.int32) // TOP_K
    src_tok = jnp.zeros((n_rows,), jnp.int32).at[dest].set(tok, unique_indices=True)
    row_gate = jnp.zeros((n_rows,), F32).at[dest].set(g_flat, unique_indices=True)
    block_expert = jnp.minimum(
        jnp.searchsorted(pend, jnp.arange(n_rows // tm, dtype=jnp.int32) * tm, side='right'), n_exp - 1
    ).astype(jnp.int32)
    y_rows = _experts(x, src_tok, g, row_gate.reshape(n_rows, 1), block_expert, wg_bf16, wu_bf16, wd_bf16)
    dest2 = dest.reshape(nt, TOP_K)
    return _combine(x, y_rows, dest2[:, 0], dest2[:, 1], final_g)


def _norm_body(x_ref, g_ref, o_ref):
    o_ref[...] = _rms(x_ref[...], g_ref[...])


def _final_norm(x, g):
    nt, d = x.shape
    tm = ROW_TILE
    return pl.pallas_call(
        _norm_body,
        grid=(nt // tm,),
        in_specs=[pl.BlockSpec((tm, d), lambda i: (i, 0)), pl.BlockSpec((1, d), lambda i: (0, 0))],
        out_specs=pl.BlockSpec((tm, d), lambda i: (i, 0)),
        out_shape=jax.ShapeDtypeStruct((nt, d), F32),
        compiler_params=_params("parallel"),
        name="final_norm",
    )(x, g.reshape(1, d))


def kernel(x_prompt, x_sample, cache_k, cache_v, state_conv, page_table, rel_bias, norm_mix, w_in, w_gate,
           w_sgu, b_sgu, g_sgu, w_conv, w_branch, w_out, norm_ffn, w_ff_gate, w_ff_up, w_ff_down,
           w_router, w_exp_gate, w_exp_up, w_exp_down, norm_final):
    batch, seq, d = x_prompt.shape
    n_seq, t_new, _ = x_sample.shape
    depth = w_in.shape[0]
    n_p = batch * seq
    n_s = n_seq * t_new
    assert t_new == T_NEW and seq % MOBA_BLOCK == 0 and seq // MOBA_BLOCK <= LANES
    assert n_p % FFN_ROWS == 0 and n_s % ROW_TILE == 0 and (n_p + n_s) % FFN_ROWS == 0
    n_kb = seq // MOBA_BLOCK
    group_w = SGU_WIDTH // SGU_GROUPS

    x = jnp.concatenate([x_prompt.reshape(n_p, d), x_sample.reshape(n_s, d)], axis=0)
    bt = rel_bias.T.astype(F32)
    moba_tiles = _moba_bias_tiles(bt)
    far_bias = bt[:, REL_BUCKETS - 1]

    outs = {k: [] for k in ("kp", "vp", "ks", "vs", "cp", "cs", "sv")}
    for l in range(depth):
        q, k, v, uv, bch, kb, vb, km = _project(x, norm_mix[l], w_in[l].astype(BF16))

        kmean = jnp.pad(km[:n_p // MOBA_BLOCK, 0, :].reshape(batch, n_kb, QKV), ((0, 0), (0, LANES - n_kb), (0, 0)))
        o_a = _moba_prompt(q, kb, vb, kmean, moba_tiles, far_bias, batch, seq)
        o_c = _sb_prompt(q, kb, vb, batch, seq)
        ws = jnp.tril(w_sgu[l]).astype(BF16)
        bs = jnp.repeat(b_sgu[l].T, group_w, axis=1)
        o_b, o_d, z_p, _ = _mix(uv, bch, None, ws, bs, g_sgu[l], w_conv[l], seq // SGU_CHUNK, n_p)

        o_att = _attn_sample(q[n_p:], k[n_p:], v[n_p:], cache_k, cache_v, l, page_table, bt)
        per_chunk = SGU_CHUNK // T_NEW
        w4 = jnp.tril(w_sgu[l][:, :T_NEW, :T_NEW])
        ws_s = jnp.einsum('ab,gts->gatbs', jnp.eye(per_chunk, dtype=F32), w4)
        ws_s = ws_s.reshape(SGU_GROUPS, SGU_CHUNK, SGU_CHUNK).astype(BF16)
        bs_s = jnp.tile(jnp.repeat(b_sgu[l][:, :T_NEW].T, group_w, axis=1), (per_chunk, 1))
        prev = state_conv[l]
        zero = jnp.zeros((n_seq, 1, CONV_DIM), F32)
        p1 = jnp.concatenate([prev[:, 1:2], zero, zero, zero], axis=1).reshape(n_s, CONV_DIM)
        p2 = jnp.concatenate([prev[:, 0:1], prev[:, 1:2], zero, zero], axis=1).reshape(n_s, CONV_DIM)
        s_b, s_d, z_s, vn_s = _mix(uv[n_p:], bch[n_p:], (p1, p2), ws_s, bs_s, g_sgu[l], w_conv[l], None, n_s)

        prompt_br = (o_a, o_b, o_c, o_d)
        sample_br = (o_att[:, :QKV // 2], s_b, o_att[:, QKV // 2:], s_d)
        x = _merge(x, norm_mix[l], prompt_br, sample_br, w_gate[l].astype(BF16), w_branch[l].astype(BF16),
                   w_out[l].astype(BF16), n_p)

        i = l // 2
        if l % 2 == 0:
            x = _ffn(x, norm_ffn[l], w_ff_gate[i].astype(BF16), w_ff_up[i].astype(BF16), w_ff_down[i].astype(BF16))
        else:
            x = _moe(x, norm_ffn[l], w_router[i], w_exp_gate[i].astype(BF16), w_exp_up[i].astype(BF16),
                     w_exp_down[i].astype(BF16), norm_final if l == depth - 1 else None)

        outs["kp"].append(k[:n_p].reshape(batch, seq, N_HEADS, HEAD_DIM))
        outs["vp"].append(v[:n_p].reshape(batch, seq, N_HEADS, HEAD_DIM))
        outs["ks"].append(k[n_p:].reshape(n_seq, t_new, N_HEADS, HEAD_DIM))
        outs["vs"].append(v[n_p:].reshape(n_seq, t_new, N_HEADS, HEAD_DIM))
        outs["cp"].append(z_p.reshape(batch, seq, CONV_DIM)[:, seq - (CONV_WIDTH - 1):])
        outs["cs"].append(z_s.reshape(n_seq, t_new, CONV_DIM)[:, t_new - (CONV_WIDTH - 1):])
        outs["sv"].append(vn_s.reshape(n_seq, t_new, SGU_WIDTH))

    y = x if depth % 2 == 0 else _final_norm(x, norm_final)
    return (y[:n_p].reshape(batch, seq, d), y[n_p:].reshape(n_seq, t_new, d),
            jnp.stack(outs["kp"]), jnp.stack(outs["vp"]), jnp.stack(outs["ks"]), jnp.stack(outs["vs"]),
            jnp.stack(outs["cp"]), jnp.stack(outs["cs"]), jnp.stack(outs["sv"]))
```

```python
import functools
import math

import numpy as np
import jax
import jax.numpy as jnp
from jax import lax
from jax.experimental import pallas as pl
from jax.experimental.pallas import tpu as pltpu

F32 = jnp.float32
BF16 = jnp.bfloat16

NORM_EPS = 1e-6
HEAD_DIM = 64
N_HEADS = 8
QKV = N_HEADS * HEAD_DIM
MOBA_BLOCK = 256
MOBA_TOPK = 3
Q_BLOCK = 128
SGU_CHUNK = 128
SGU_WIDTH = 256
SGU_GROUPS = 4
CONV_DIM = 256
CONV_WIDTH = 3
N_BRANCH = 4
BRANCH_DIM = 256
PAGE = 128
REL_BUCKETS = 32
REL_MAX_DIST = 128
TOP_K = 2
ATT_SCALE = HEAD_DIM ** -0.5
MASKED = -1e30
SB_DROP = -105.0
LANES = 128
ROW_TILE = 256
MOE_ROWS = 1024
MOE_FF_TILE = 896
FFN_ROWS = 512
FFN_FF_TILE = 256
GATHER_ROWS = 256
V7X_VMEM_LIMIT = 56 * 1024 * 1024


def _params(*sem):
    return pltpu.CompilerParams(dimension_semantics=sem, vmem_limit_bytes=V7X_VMEM_LIMIT)


def _dot(a, b):
    return jnp.dot(a, b, preferred_element_type=F32)


def _dot_nt(a, b):
    return lax.dot_general(a, b, (((1,), (1,)), ((), ())), preferred_element_type=F32)


def _split_bf16(x):
    hi = x.astype(BF16)
    lo = (x - hi.astype(F32)).astype(BF16)
    return hi, lo


def _dot_nt_f32(a, b):
    ah, al = _split_bf16(a)
    bh, bl = _split_bf16(b)
    return _dot_nt(ah, bh) + _dot_nt(ah, bl) + _dot_nt(al, bh)


def _dot_f32(a, b):
    ah, al = _split_bf16(a)
    bh, bl = _split_bf16(b)
    return _dot(ah, bh) + _dot(ah, bl) + _dot(al, bh)


def _rms(x, g):
    return x * lax.rsqrt(jnp.mean(x * x, axis=-1, keepdims=True) + NORM_EPS) * g


def _gelu_tanh(x):
    return 0.5 * x * (1.0 + jnp.tanh(math.sqrt(2.0 / math.pi) * (x + 0.044715 * (x * x * x))))


def _sigmoid(x):
    return 1.0 / (1.0 + jnp.exp(-x))


def _softplus(z):
    return jnp.maximum(z, 0.0) + jnp.log1p(jnp.exp(-jnp.abs(z)))


def _suffix_sum(lk, u):
    n = lk.shape[0]
    hi, lo = _split_bf16(lk)
    r = _dot(jnp.concatenate([hi, lo], axis=0), u)
    return r[:n] + r[n:]


def _strict_upper(n):
    j = np.arange(n)[:, None]
    s = np.arange(n)[None, :]
    return jnp.asarray((j > s).astype(np.float32), dtype=BF16)


def _t5_bucket_table(max_dist):
    d = np.arange(max_dist + 1)
    max_exact = REL_BUCKETS // 2
    large = max_exact + (np.log(np.maximum(d, 1).astype(np.float32) / np.float32(max_exact))
                         / np.float32(math.log(REL_MAX_DIST / max_exact))
                         * np.float32(REL_BUCKETS - max_exact)).astype(np.int32)
    large = np.minimum(large, REL_BUCKETS - 1)
    return np.where(d < max_exact, d, large).astype(np.int32)


def _bias_by_dist(bt_rows, dist, valid):
    table = _t5_bucket_table(2 * REL_MAX_DIST)
    b = jnp.broadcast_to(bt_rows[:, 0:1], dist.shape)
    for k in range(1, REL_BUCKETS):
        first = int(np.argmax(table >= k))
        b = jnp.where(dist >= first, bt_rows[:, k:k + 1], b)
    return jnp.where(valid, b, MASKED)


def _proj_body(x_ref, g_ref, w_ref, q_ref, k_ref, v_ref, uv_ref, bch_ref, kb_ref, vb_ref, km_ref):
    h = _rms(x_ref[...], g_ref[...]).astype(BF16)

    def mm(lo, hi):
        return _dot(h, w_ref[:, lo:hi])

    q_ref[...] = mm(0, QKV)
    k = mm(QKV, 2 * QKV)
    k_ref[...] = k
    kb_ref[...] = k.astype(BF16)
    km_ref[...] = jnp.broadcast_to(jnp.mean(k, axis=0, keepdims=True), km_ref.shape)
    v = mm(2 * QKV, 3 * QKV)
    v_ref[...] = v
    vb_ref[...] = v.astype(BF16)
    uv_ref[...] = mm(3 * QKV, 3 * QKV + 2 * SGU_WIDTH)
    bch_ref[...] = mm(3 * QKV + 2 * SGU_WIDTH, 3 * QKV + 2 * SGU_WIDTH + 3 * CONV_DIM)


def _project(x, g, w_bf16):
    nt, d = x.shape
    in_dim = w_bf16.shape[1]
    tm = MOBA_BLOCK
    nb = nt // tm
    row = lambda i: (i, 0)
    return pl.pallas_call(
        _proj_body,
        grid=(nb,),
        in_specs=[pl.BlockSpec((tm, d), row),
                  pl.BlockSpec((1, d), lambda i: (0, 0)),
                  pl.BlockSpec((d, in_dim), lambda i: (0, 0))],
        out_specs=[pl.BlockSpec((tm, QKV), row), pl.BlockSpec((tm, QKV), row), pl.BlockSpec((tm, QKV), row),
                   pl.BlockSpec((tm, 2 * SGU_WIDTH), row), pl.BlockSpec((tm, 3 * CONV_DIM), row),
                   pl.BlockSpec((tm, QKV), row), pl.BlockSpec((tm, QKV), row),
                   pl.BlockSpec((None, 8, QKV), lambda i: (i, 0, 0))],
        out_shape=[jax.ShapeDtypeStruct((nt, QKV), F32), jax.ShapeDtypeStruct((nt, QKV), F32),
                   jax.ShapeDtypeStruct((nt, QKV), F32), jax.ShapeDtypeStruct((nt, 2 * SGU_WIDTH), F32),
                   jax.ShapeDtypeStruct((nt, 3 * CONV_DIM), F32),
                   jax.ShapeDtypeStruct((nt, QKV), BF16), jax.ShapeDtypeStruct((nt, QKV), BF16),
                   jax.ShapeDtypeStruct((nb, 8, QKV), F32)],
        compiler_params=_params("parallel"),
        name="proj",
    )(x, g.reshape(1, d), w_bf16)


def _top_blocks(gate, lane, lane_mask):
    lane = lane.astype(F32)
    g = jnp.where(lane_mask, gate, -jnp.inf)
    picks = []
    for _ in range(MOBA_TOPK):
        m = jnp.max(g, axis=-1, keepdims=True)
        idx = jnp.min(jnp.where(g == m, lane, 1e9), axis=-1, keepdims=True)
        picks.append((idx, m > -jnp.inf))
        g = jnp.where(lane == idx, -jnp.inf, g)
    return picks


def _is_picked(picks, j):
    j = jnp.asarray(j).astype(F32)
    c = (picks[0][0] == j) & picks[0][1]
    for idx, ok in picks[1:]:
        c = c | ((idx == j) & ok)
    return c


def _moba_prompt_body(fb_ref, q_ref, k_ref, v_ref, km_ref, bias_ref, spread_ref, o_ref, *, chunk):
    pr = pl.program_id(1)
    qi = pl.program_id(2)
    cur = qi // 2
    half = qi % 2
    rows = 2 * Q_BLOCK
    lane = lax.broadcasted_iota(jnp.int32, (Q_BLOCK, LANES), 1)
    q = q_ref[...] * ATT_SCALE
    qs = jnp.concatenate([jnp.where(lane < HEAD_DIM, q, 0.0), jnp.where(lane >= HEAD_DIM, q, 0.0)], axis=0)
    qsb = qs.astype(BF16)
    lane2 = lax.broadcasted_iota(jnp.int32, (rows, LANES), 1)
    picks = _top_blocks(_dot_nt_f32(qs, km_ref[...]), lane2, lane2 < cur)
    first_head = lax.broadcasted_iota(jnp.int32, (rows, 1), 0) < Q_BLOCK
    far_bias = jnp.where(first_head, fb_ref[2 * pr], fb_ref[2 * pr + 1])

    start = pl.multiple_of(cur * MOBA_BLOCK, MOBA_BLOCK)
    start_p = pl.multiple_of(jnp.maximum(cur - 1, 0) * MOBA_BLOCK, MOBA_BLOCK)
    s_own = _dot_nt(qsb, k_ref[pl.ds(start, MOBA_BLOCK), :]) + bias_ref[half]
    s_prev = _dot_nt(qsb, k_ref[pl.ds(start_p, MOBA_BLOCK), :]) + bias_ref[2 + half]
    s_prev = jnp.where(_is_picked(picks, cur - 1), s_prev, MASKED)
    m = jnp.maximum(jnp.max(s_own, axis=-1, keepdims=True), jnp.max(s_prev, axis=-1, keepdims=True))
    p_own = jnp.exp(s_own - m)
    p_prev = jnp.exp(s_prev - m)
    l = jnp.sum(p_own, axis=-1, keepdims=True) + jnp.sum(p_prev, axis=-1, keepdims=True)
    acc = (_dot(p_own.astype(BF16), v_ref[pl.ds(start, MOBA_BLOCK), :])
           + _dot(p_prev.astype(BF16), v_ref[pl.ds(start_p, MOBA_BLOCK), :]))

    lane2f = lane2.astype(F32)
    picked = (lane2f == picks[0][0]) & picks[0][1]
    for idx, ok in picks[1:]:
        picked = picked | ((lane2f == idx) & ok)
    add_rows = jnp.where(picked & (lane2 < cur - 1), 0.0, MASKED).astype(BF16)

    def chunk_step(c, carry):
        m, l, acc = carry
        start = pl.multiple_of(c * (chunk * MOBA_BLOCK), chunk * MOBA_BLOCK)
        s = _dot_nt(qsb, k_ref[pl.ds(start, chunk * MOBA_BLOCK), :]) + far_bias + _dot(add_rows, spread_ref[c])
        m_new = jnp.maximum(m, jnp.max(s, axis=-1, keepdims=True))
        alpha = jnp.exp(m - m_new)
        p = jnp.exp(s - m_new)
        l = alpha * l + jnp.sum(p, axis=-1, keepdims=True)
        acc = alpha * acc + _dot(p.astype(BF16), v_ref[pl.ds(start, chunk * MOBA_BLOCK), :])
        return m_new, l, acc

    n_older = jnp.maximum(cur - 1, 0)
    m, l, acc = lax.fori_loop(0, (n_older + chunk - 1) // chunk, chunk_step, (m, l, acc))
    out = acc / l
    o_ref[...] = jnp.where(lane < HEAD_DIM, out[:Q_BLOCK], out[Q_BLOCK:])


def _moba_bias_tiles(bt):
    n_pairs = bt.shape[0] // 2
    shape = (n_pairs * 2 * Q_BLOCK, MOBA_BLOCK)
    i = lax.broadcasted_iota(jnp.int32, shape, 0) & (Q_BLOCK - 1)
    c = lax.broadcasted_iota(jnp.int32, shape, 1)
    bt_rows = jnp.repeat(bt, Q_BLOCK, axis=0)
    tiles = []
    for delta in (0, Q_BLOCK, MOBA_BLOCK, MOBA_BLOCK + Q_BLOCK):
        dist = delta + i - c
        tiles.append(_bias_by_dist(bt_rows, dist, dist >= 0).reshape(n_pairs, 2 * Q_BLOCK, MOBA_BLOCK))
    return jnp.stack(tiles, axis=1)


def _moba_prompt(q, kb, vb, kmean, bias_tiles, far_bias, batch, seq):
    nq = seq // Q_BLOCK
    n_kb = seq // MOBA_BLOCK
    chunk = next(c for c in (4, 2, 1) if n_kb % c == 0)
    blk = np.arange(LANES)[None, :, None]
    key = np.arange(chunk * MOBA_BLOCK)[None, None, :]
    cid = np.arange(n_kb // chunk)[:, None, None]
    spread = jnp.asarray((blk == cid * chunk + key // MOBA_BLOCK).astype(np.float32), dtype=BF16)
    return pl.pallas_call(
        functools.partial(_moba_prompt_body, chunk=chunk),
        grid_spec=pltpu.PrefetchScalarGridSpec(
            num_scalar_prefetch=1,
            grid=(batch, 2, nq),
            in_specs=[pl.BlockSpec((Q_BLOCK, LANES), lambda b, p, i, fb: (b * nq + i, p)),
                      pl.BlockSpec((seq, LANES), lambda b, p, i, fb: (b, p)),
                      pl.BlockSpec((seq, LANES), lambda b, p, i, fb: (b, p)),
                      pl.BlockSpec((None, LANES, LANES), lambda b, p, i, fb: (b, 0, p)),
                      pl.BlockSpec((None, 4, 2 * Q_BLOCK, MOBA_BLOCK), lambda b, p, i, fb: (p, 0, 0, 0)),
                      pl.BlockSpec(spread.shape, lambda b, p, i, fb: (0, 0, 0))],
            out_specs=pl.BlockSpec((Q_BLOCK, LANES), lambda b, p, i, fb: (b * nq + i, p)),
        ),
        out_shape=jax.ShapeDtypeStruct((batch * seq, 2 * LANES), F32),
        compiler_params=_params("parallel", "parallel", "arbitrary"),
        name="moba_prompt",
    )(far_bias, q, kb, vb, kmean, bias_tiles, spread)


def _sb_prompt_body(q_ref, k_ref, v_ref, u_ref, o_ref):
    qi = pl.program_id(2)
    rows = 2 * Q_BLOCK
    lane = lax.broadcasted_iota(jnp.int32, (Q_BLOCK, LANES), 1)
    q = q_ref[...] * ATT_SCALE
    qsb = jnp.concatenate([jnp.where(lane < HEAD_DIM, q, 0.0), jnp.where(lane >= HEAD_DIM, q, 0.0)],
                          axis=0).astype(BF16)
    u = u_ref[...]
    col = lax.broadcasted_iota(jnp.int32, (rows, Q_BLOCK), 1)
    qrow = lax.broadcasted_iota(jnp.int32, (rows, Q_BLOCK), 0) & (Q_BLOCK - 1)

    def tile(j, keep, r_acc, acc):
        start = pl.multiple_of(jnp.maximum(j, 0) * Q_BLOCK, Q_BLOCK)
        z = _dot_nt(qsb, k_ref[pl.ds(start, Q_BLOCK), :])
        sp = _softplus(z)
        lk = -sp if keep is None else jnp.where(keep, -sp, 0.0)
        a = jnp.exp(z - sp + _suffix_sum(lk, u) + r_acc)
        if keep is not None:
            a = jnp.where(keep, a, 0.0)
        acc = acc + _dot(a.astype(BF16), v_ref[pl.ds(start, Q_BLOCK), :])
        return r_acc + jnp.sum(lk, axis=-1, keepdims=True), acc

    def exists(j):
        return jnp.broadcast_to(j >= 0, (rows, Q_BLOCK))

    r_acc, acc = tile(qi, col < qrow, jnp.zeros((rows, 1), F32), jnp.zeros((rows, LANES), F32))
    r_acc, acc = tile(qi - 1, exists(qi - 1), r_acc, acc)
    r_acc, acc = tile(qi - 2, exists(qi - 2), r_acc, acc)

    def more(state):
        it, go, _, _ = state
        return (qi - 3 - 2 * it >= 0) & (go > 0)

    def older(state):
        it, _, r_acc, acc = state
        j = qi - 3 - 2 * it
        r_acc, acc = tile(j, None, r_acc, acc)
        r_acc, acc = tile(j - 1, exists(j - 1), r_acc, acc)
        return it + 1, (jnp.max(r_acc) > SB_DROP).astype(jnp.int32), r_acc, acc

    go = (jnp.max(r_acc) > SB_DROP).astype(jnp.int32)
    _, _, _, acc = lax.while_loop(more, older, (jnp.int32(0), go, r_acc, acc))
    o_ref[...] = jnp.where(lane < HEAD_DIM, acc[:Q_BLOCK], acc[Q_BLOCK:])


def _sb_prompt(q, kb, vb, batch, seq):
    nq = seq // Q_BLOCK
    return pl.pallas_call(
        _sb_prompt_body,
        grid=(batch, 2, nq),
        in_specs=[pl.BlockSpec((Q_BLOCK, LANES), lambda b, p, i: (b * nq + i, 2 + p)),
                  pl.BlockSpec((seq, LANES), lambda b, p, i: (b, 2 + p)),
                  pl.BlockSpec((seq, LANES), lambda b, p, i: (b, 2 + p)),
                  pl.BlockSpec((Q_BLOCK, Q_BLOCK), lambda b, p, i: (0, 0))],
        out_specs=pl.BlockSpec((Q_BLOCK, LANES), lambda b, p, i: (b * nq + i, p)),
        out_shape=jax.ShapeDtypeStruct((batch * seq, 2 * LANES), F32),
        compiler_params=_params("parallel", "parallel", "arbitrary"),
        name="sb_prompt",
    )(q, kb, vb, _strict_upper(Q_BLOCK))


T_NEW = 4
SOFTMAX_ROWS = 4 * T_NEW


def _attn_sample_body(pt_ref, qx_ref, knew_ref, vnew_ref, bias_ref, bias_new_ref, u_ref, *rest, n_pages):
    del pt_ref
    k_refs = rest[:n_pages]
    v_refs = rest[n_pages:2 * n_pages]
    o_ref = rest[2 * n_pages]
    n_blocks = n_pages // 2
    sr = SOFTMAX_ROWS
    n_rows = N_HEADS * T_NEW
    past = n_pages * PAGE

    qx = qx_ref[...]
    qxb = qx.astype(BF16)

    blk_lane = lax.broadcasted_iota(jnp.int32, (QKV, LANES), 1)
    km = jnp.zeros((QKV, LANES), F32)
    parts = []
    for p in range(n_pages):
        kp = k_refs[p][...].reshape(QKV, PAGE)
        parts.append(_dot(qxb, kp.astype(BF16)))
        if p % 2 == 0:
            ksum = kp
        else:
            mean = jnp.sum(ksum + kp, axis=1, keepdims=True) * (1.0 / MOBA_BLOCK)
            km = jnp.where(blk_lane == p // 2, mean, km)
    s = jnp.concatenate(parts, axis=1)

    pad_rows = jnp.zeros((LANES - 8, QKV), F32)
    knew = jnp.concatenate([knew_ref[...], pad_rows], axis=0).astype(BF16)
    vnew = jnp.concatenate([vnew_ref[...], pad_rows], axis=0).astype(BF16)
    s_new = _dot_nt(qxb, knew)

    lane_s = lax.broadcasted_iota(jnp.int32, (sr, LANES), 1)
    picks = _top_blocks(_dot_f32(qx[:sr], km), lane_s, lane_s < n_blocks)
    key_block = (lax.broadcasted_iota(jnp.int32, (sr, past), 1) >> 8).astype(F32)
    la = jnp.where(_is_picked(picks, key_block), s[:sr] + bias_ref[...], MASKED)
    la_new = s_new[:sr] + bias_new_ref[...]
    m = jnp.maximum(jnp.max(la, axis=-1, keepdims=True), jnp.max(la_new, axis=-1, keepdims=True))
    pa = jnp.exp(la - m)
    pa_new = jnp.exp(la_new - m)
    denom = jnp.sum(pa, axis=-1, keepdims=True) + jnp.sum(pa_new, axis=-1, keepdims=True)

    u = u_ref[...]
    zb_new = s_new[sr:]
    keep_new = lane_s < (lax.broadcasted_iota(jnp.int32, (sr, LANES), 0) & (T_NEW - 1))
    sp_new = _softplus(zb_new)
    lk_new = jnp.where(keep_new, -sp_new, 0.0)
    a_new = jnp.where(keep_new, jnp.exp(zb_new - sp_new + _suffix_sum(lk_new, u)), 0.0)
    zb = s[sr:]
    sp = _softplus(zb)
    lk = -sp
    stacked = jnp.concatenate([lk[:, p * PAGE:(p + 1) * PAGE] for p in range(n_pages)], axis=0)
    within = _suffix_sum(stacked, u)
    after = jnp.sum(lk_new, axis=-1, keepdims=True)
    cols = [None] * n_pages
    for p in range(n_pages - 1, -1, -1):
        cols[p] = within[p * sr:(p + 1) * sr] + after
        after = after + jnp.sum(lk[:, p * PAGE:(p + 1) * PAGE], axis=-1, keepdims=True)
    a = jnp.exp(zb - sp + jnp.concatenate(cols, axis=1))

    w = jnp.concatenate([pa, a], axis=0).astype(BF16)
    w_new = jnp.concatenate([pa_new, a_new], axis=0).astype(BF16)
    o = _dot(w_new, vnew)
    for p in range(n_pages):
        o = o + _dot_nt(w[:, p * PAGE:(p + 1) * PAGE], v_refs[p][...].reshape(QKV, PAGE).astype(BF16))
    row_head = lax.broadcasted_iota(jnp.int32, (n_rows, QKV), 0) >> 2
    col_head = lax.broadcasted_iota(jnp.int32, (n_rows, QKV), 1) >> 6
    o = jnp.where(row_head == col_head, o, 0.0)
    out = o[:, 0:HEAD_DIM]
    for h in range(1, N_HEADS):
        out = out + o[:, h * HEAD_DIM:(h + 1) * HEAD_DIM]
    scale = jnp.concatenate([1.0 / denom, jnp.ones((n_rows - sr, 1), F32)], axis=0)
    o_ref[...] = out * scale


def _attn_sample(q_s, k_s, v_s, cache_k, cache_v, layer, page_table, bt):
    n_seq, n_pages = page_table.shape
    assert q_s.shape[0] == n_seq * T_NEW and n_pages % 2 == 0
    past = n_pages * PAGE
    n_rows = N_HEADS * T_NEW
    q4 = (q_s * ATT_SCALE).reshape(n_seq, T_NEW, N_HEADS, 1, HEAD_DIM).transpose(0, 2, 1, 3, 4)
    eye = jnp.eye(N_HEADS, dtype=F32).reshape(1, N_HEADS, 1, N_HEADS, 1)
    qx = (q4 * eye).reshape(n_seq, n_rows, QKV)
    pad = ((0, 0), (0, 8 - T_NEW), (0, 0))
    knew = jnp.pad(k_s.reshape(n_seq, T_NEW, QKV), pad)
    vnew = jnp.pad(v_s.reshape(n_seq, T_NEW, QKV), pad)
    ck = cache_k.transpose(0, 1, 3, 4, 2)
    cv = cache_v.transpose(0, 1, 3, 4, 2)

    bt_rows = jnp.repeat(bt[:4], T_NEW, axis=0)

    def key_bias(width, first_key_dist, new_keys):
        r = lax.broadcasted_iota(jnp.int32, (SOFTMAX_ROWS, width), 0)
        c = lax.broadcasted_iota(jnp.int32, (SOFTMAX_ROWS, width), 1)
        dist = first_key_dist + (r & (T_NEW - 1)) - c
        valid = (dist >= 0) & (c < T_NEW) if new_keys else dist >= 0
        return _bias_by_dist(bt_rows, dist, valid)

    bias_past = key_bias(past, past, False)
    bias_new = key_bias(LANES, 0, True)

    def page_spec(p):
        return pl.BlockSpec((None, None, N_HEADS, HEAD_DIM, PAGE),
                            lambda b, pt, p=p: (layer, pt[b * n_pages + p], 0, 0, 0))

    seq3 = lambda b, pt: (b, 0, 0)
    const2 = lambda b, pt: (0, 0)
    out = pl.pallas_call(
        functools.partial(_attn_sample_body, n_pages=n_pages),
        grid_spec=pltpu.PrefetchScalarGridSpec(
            num_scalar_prefetch=1,
            grid=(n_seq,),
            in_specs=[pl.BlockSpec((None, n_rows, QKV), seq3),
                      pl.BlockSpec((None, 8, QKV), seq3),
                      pl.BlockSpec((None, 8, QKV), seq3),
                      pl.BlockSpec((SOFTMAX_ROWS, past), const2),
                      pl.BlockSpec((SOFTMAX_ROWS, LANES), const2),
                      pl.BlockSpec((PAGE, PAGE), const2)]
                     + [page_spec(p) for p in range(n_pages)] * 2,
            out_specs=pl.BlockSpec((None, n_rows, HEAD_DIM), seq3),
        ),
        out_shape=jax.ShapeDtypeStruct((n_seq, n_rows, HEAD_DIM), F32),
        compiler_params=_params("parallel"),
        name="attn_sample",
    )(page_table.reshape(-1), qx, knew, vnew, bias_past, bias_new, _strict_upper(PAGE),
      *([ck] * n_pages), *([cv] * n_pages))
    return out.reshape(n_seq, N_HEADS, T_NEW, HEAD_DIM).transpose(0, 2, 1, 3).reshape(n_seq * T_NEW, QKV)


def _mix_body(*refs, chunks_per_seq):
    if chunks_per_seq is None:
        uv_ref, bch_ref, p1_ref, p2_ref, ws_ref, bs_ref, gs_ref, wc_ref, ob_ref, od_ref, z_ref, vn_ref = refs
    else:
        uv_ref, bch_ref, prev_ref, ws_ref, bs_ref, gs_ref, wc_ref, ob_ref, od_ref, z_ref, vn_ref = refs
    uv = uv_ref[...]
    x = _gelu_tanh(uv[:, SGU_WIDTH:])
    xc = x - jnp.mean(x, axis=-1, keepdims=True)
    vn = xc * lax.rsqrt(jnp.mean(xc * xc, axis=-1, keepdims=True) + NORM_EPS) * gs_ref[...]
    vn_ref[...] = vn
    vnb = vn.astype(BF16)
    group = lax.broadcasted_iota(jnp.int32, (SGU_CHUNK, SGU_WIDTH), 1) >> 6
    s = jnp.zeros((SGU_CHUNK, SGU_WIDTH), F32)
    for g in range(SGU_GROUPS):
        s = jnp.where(group == g, _dot(ws_ref[g], vnb), s)
    ob_ref[...] = _gelu_tanh(uv[:, :SGU_WIDTH]) * (s + bs_ref[...])

    bch = bch_ref[...]
    z = bch[:, CONV_DIM:2 * CONV_DIM] * bch[:, 2 * CONV_DIM:]
    z_ref[...] = z
    row = lax.broadcasted_iota(jnp.int32, (SGU_CHUNK, CONV_DIM), 0)
    r1 = pltpu.roll(z, 1, 0)
    r2 = pltpu.roll(z, 2, 0)
    if chunks_per_seq is None:
        rr = row & (T_NEW - 1)
        z1 = jnp.where(rr >= 1, r1, p1_ref[...])
        z2 = jnp.where(rr >= 2, r2, p2_ref[...])
    else:
        pb = prev_ref[...]
        zp = pb[:, CONV_DIM:2 * CONV_DIM] * pb[:, 2 * CONV_DIM:]
        zp = jnp.where(pl.program_id(0) % chunks_per_seq == 0, 0.0, zp)
        z1 = jnp.where(row >= 1, r1, zp[7:8, :])
        z2 = jnp.where(row >= 2, r2, jnp.where(row == 1, zp[7:8, :], zp[6:7, :]))
    wc = wc_ref[...]
    od_ref[...] = bch[:, :CONV_DIM] * (wc[0:1, :] * z2 + wc[1:2, :] * z1 + wc[2:3, :] * z)


def _mix(uv, bch, history, ws_bf16, bs, gs, wc, chunks_per_seq, n_rows):
    nc = n_rows // SGU_CHUNK
    row = lambda i: (i, 0)
    const = lambda i: (0, 0)
    if chunks_per_seq is None:
        hist_specs = [pl.BlockSpec((SGU_CHUNK, CONV_DIM), row)] * 2
        hist = list(history)
    else:
        sub = SGU_CHUNK // 8
        hist_specs = [pl.BlockSpec((8, 3 * CONV_DIM), lambda i: (jnp.maximum(i * sub - 1, 0), 0))]
        hist = [bch]
    return pl.pallas_call(
        functools.partial(_mix_body, chunks_per_seq=chunks_per_seq),
        grid=(nc,),
        in_specs=[pl.BlockSpec((SGU_CHUNK, 2 * SGU_WIDTH), row), pl.BlockSpec((SGU_CHUNK, 3 * CONV_DIM), row)]
                 + hist_specs
                 + [pl.BlockSpec((SGU_GROUPS, SGU_CHUNK, SGU_CHUNK), lambda i: (0, 0, 0)),
                    pl.BlockSpec((SGU_CHUNK, SGU_WIDTH), const),
                    pl.BlockSpec((1, SGU_WIDTH), const),
                    pl.BlockSpec((8, CONV_DIM), const)],
        out_specs=[pl.BlockSpec((SGU_CHUNK, SGU_WIDTH), row)] * 4,
        out_shape=[jax.ShapeDtypeStruct((n_rows, SGU_WIDTH), F32)] * 4,
        compiler_params=_params("parallel"),
        name="mix_sample" if chunks_per_seq is None else "mix_prompt",
    )(uv, bch, *hist, ws_bf16, bs, gs.reshape(1, SGU_WIDTH), jnp.pad(wc, ((0, 8 - CONV_WIDTH), (0, 0))))


def _merge_body(x_ref, g_ref, pa_ref, pb_ref, pc_ref, pd_ref, sa_ref, sb_ref, sc_ref, sd_ref,
                wg_ref, wb_ref, wo_ref, o_ref, *, n_prompt_tiles):
    x = x_ref[...]
    d = x.shape[1]
    h = _rms(x, g_ref[...]).astype(BF16)
    is_prompt = pl.program_id(0) < n_prompt_tiles
    merged = jnp.zeros(x.shape, F32)
    for n, (p_ref, s_ref) in enumerate(((pa_ref, sa_ref), (pb_ref, sb_ref), (pc_ref, sc_ref), (pd_ref, sd_ref))):
        br = jnp.where(is_prompt, p_ref[...], s_ref[...]).astype(BF16)
        up = _dot(br, wb_ref[n])
        gate = _sigmoid(_dot(h, wg_ref[:, n * d:(n + 1) * d]))
        merged = merged + gate * up
    o_ref[...] = x + _dot(merged.astype(BF16), wo_ref[...])


def _merge(x, g, prompt_br, sample_br, wg_bf16, wb_bf16, wo_bf16, n_prompt):
    nt, d = x.shape
    tm = ROW_TILE
    npt = n_prompt // tm
    row = lambda i: (i, 0)
    p_spec = pl.BlockSpec((tm, BRANCH_DIM), lambda i: (jnp.minimum(i, npt - 1), 0))
    s_spec = pl.BlockSpec((tm, BRANCH_DIM), lambda i: (jnp.maximum(i - npt, 0), 0))
    return pl.pallas_call(
        functools.partial(_merge_body, n_prompt_tiles=npt),
        grid=(nt // tm,),
        in_specs=[pl.BlockSpec((tm, d), row), pl.BlockSpec((1, d), lambda i: (0, 0))]
                 + [p_spec] * N_BRANCH + [s_spec] * N_BRANCH
                 + [pl.BlockSpec((d, N_BRANCH * d), lambda i: (0, 0)),
                    pl.BlockSpec((N_BRANCH, BRANCH_DIM, d), lambda i: (0, 0, 0)),
                    pl.BlockSpec((d, d), lambda i: (0, 0))],
        out_specs=pl.BlockSpec((tm, d), row),
        out_shape=jax.ShapeDtypeStruct((nt, d), F32),
        compiler_params=_params("parallel"),
        name="merge",
    )(x, g.reshape(1, d), *prompt_br, *sample_br, wg_bf16, wb_bf16, wo_bf16)


def _silu(x):
    return x * _sigmoid(x)


def _ffn_body(x_ref, g_ref, wg_ref, wu_ref, wd_ref, o_ref, h_ref, acc_ref):
    j = pl.program_id(1)

    @pl.when(j == 0)
    def _():
        h_ref[...] = _rms(x_ref[...], g_ref[...]).astype(BF16)
        acc_ref[...] = jnp.zeros_like(acc_ref)

    h = h_ref[...]
    a = _silu(_dot(h, wg_ref[...])) * _dot(h, wu_ref[...])
    acc_ref[...] += _dot(a.astype(BF16), wd_ref[...])

    @pl.when(j == pl.num_programs(1) - 1)
    def _():
        o_ref[...] = x_ref[...] + acc_ref[...]


def _ffn(x, g, wg_bf16, wu_bf16, wd_bf16):
    nt, d = x.shape
    dff = wg_bf16.shape[1]
    tm, tf = FFN_ROWS, FFN_FF_TILE
    return pl.pallas_call(
        _ffn_body,
        grid=(nt // tm, dff // tf),
        in_specs=[pl.BlockSpec((tm, d), lambda i, j: (i, 0)), pl.BlockSpec((1, d), lambda i, j: (0, 0)),
                  pl.BlockSpec((d, tf), lambda i, j: (0, j)), pl.BlockSpec((d, tf), lambda i, j: (0, j)),
                  pl.BlockSpec((tf, d), lambda i, j: (j, 0))],
        out_specs=pl.BlockSpec((tm, d), lambda i, j: (i, 0)),
        out_shape=jax.ShapeDtypeStruct((nt, d), F32),
        scratch_shapes=[pltpu.VMEM((tm, d), BF16), pltpu.VMEM((tm, d), F32)],
        compiler_params=_params("parallel", "arbitrary"),
        name="ffn",
    )(x, g.reshape(1, d), wg_bf16, wu_bf16, wd_bf16)


def _route_body(x_ref, g_ref, wr_ref, idx_ref, gate_ref, *, n_exp):
    h = _rms(x_ref[...], g_ref[...])
    logits = _dot_nt_f32(h, wr_ref[...])
    lane_i = lax.broadcasted_iota(jnp.int32, logits.shape, 1)
    lane = lane_i.astype(F32)
    lg = jnp.where(lane_i < n_exp, logits, -jnp.inf)
    m1 = jnp.max(lg, axis=-1, keepdims=True)
    i1 = jnp.min(jnp.where(lg == m1, lane, float(LANES)), axis=-1, keepdims=True)
    lg2 = jnp.where(lane == i1, -jnp.inf, lg)
    m2 = jnp.max(lg2, axis=-1, keepdims=True)
    i2 = jnp.min(jnp.where(lg2 == m2, lane, float(LANES)), axis=-1, keepdims=True)
    e2 = jnp.exp(m2 - m1)
    idx_ref[...] = jnp.where(lane_i == 0, i1, jnp.where(lane_i == 1, i2, 0.0)).astype(jnp.int32)
    gate_ref[...] = jnp.where(lane_i == 0, 1.0 / (1.0 + e2), jnp.where(lane_i == 1, e2 / (1.0 + e2), 0.0))


def _route(x, g, w_router):
    nt, d = x.shape
    n_exp = w_router.shape[1]
    assert n_exp <= LANES
    wr = jnp.pad(w_router.T, ((0, LANES - n_exp), (0, 0)))
    tm = ROW_TILE
    row = lambda i: (i, 0)
    return pl.pallas_call(
        functools.partial(_route_body, n_exp=n_exp),
        grid=(nt // tm,),
        in_specs=[pl.BlockSpec((tm, d), row), pl.BlockSpec((1, d), lambda i: (0, 0)),
                  pl.BlockSpec((LANES, d), lambda i: (0, 0))],
        out_specs=[pl.BlockSpec((tm, LANES), row)] * 2,
        out_shape=[jax.ShapeDtypeStruct((nt, LANES), jnp.int32), jax.ShapeDtypeStruct((nt, LANES), F32)],
        compiler_params=_params("parallel"),
        name="route",
    )(x, g.reshape(1, d), wr)


def _expert_body(be_ref, src_ref, nu_ref, x_hbm, g_ref, wg_ref, wu_ref, wd_ref, o_ref,
                 xbuf, sem, h_ref, acc_ref):
    del be_ref
    i = pl.program_id(0)
    used = i < nu_ref[0]
    j = pl.program_id(1)
    nb = pl.num_programs(0)
    nf = pl.num_programs(1)
    tm = xbuf.shape[1]
    per_step = tm // nf

    def row_copy(block, r, slot):
        tok = src_ref[block * tm + r]
        return pltpu.make_async_copy(x_hbm.at[pl.ds(tok, 1), :], xbuf.at[slot, pl.ds(r, 1), :], sem.at[slot])

    def wait_block(slot):
        def body(r, c):
            row_copy(0, r, slot).wait()
            return c
        lax.fori_loop(0, tm, body, 0)

    @pl.when(j == 0)
    def _():
        @pl.when(i == 0)
        def _():
            def body(r, c):
                row_copy(0, r, 0).start()
                return c
            lax.fori_loop(0, tm, body, 0)

        wait_block(i % 2)
        h_ref[...] = _rms(xbuf[i % 2], g_ref[...]).astype(BF16)
        acc_ref[...] = jnp.zeros_like(acc_ref)

    nxt = jnp.minimum(i + 1, nb - 1)

    @pl.when(jnp.logical_not(used))
    def _():
        def body(r, c):
            row_copy(nxt, j * per_step + r, (i + 1) % 2).start()
            return c
        lax.fori_loop(0, per_step, body, 0)

    @pl.when(used)
    def _():
        for r in range(per_step):
            row_copy(nxt, j * per_step + r, (i + 1) % 2).start()
        h = h_ref[...]
        a = _silu(_dot(h, wg_ref[...])) * _dot(h, wu_ref[...])
        acc_ref[...] += _dot(a.astype(BF16), wd_ref[...])

    @pl.when(j == nf - 1)
    def _():
        o_ref[...] = acc_ref[...]

        @pl.when(i == nb - 1)
        def _():
            wait_block((i + 1) % 2)


def _experts(x, src_tok, g, n_used, block_expert, wg_bf16, wu_bf16, wd_bf16):
    r = src_tok.shape[0]
    d = x.shape[1]
    dff = wg_bf16.shape[2]
    tm, tf = MOE_ROWS, MOE_FF_TILE
    assert tm % (dff // tf) == 0
    wmap = lambda i, j, be, st, nu: (be[i], 0, j)
    return pl.pallas_call(
        _expert_body,
        grid_spec=pltpu.PrefetchScalarGridSpec(
            num_scalar_prefetch=3,
            grid=(r // tm, dff // tf),
            in_specs=[pl.BlockSpec(memory_space=pl.ANY),
                      pl.BlockSpec((1, d), lambda i, j, be, st, nu: (0, 0)),
                      pl.BlockSpec((None, d, tf), wmap),
                      pl.BlockSpec((None, d, tf), wmap),
                      pl.BlockSpec((None, tf, d), lambda i, j, be, st, nu: (be[i], j, 0))],
            out_specs=pl.BlockSpec((tm, d), lambda i, j, be, st, nu: (i, 0)),
            scratch_shapes=[pltpu.VMEM((2, tm, d), F32), pltpu.SemaphoreType.DMA((2,)),
                            pltpu.VMEM((tm, d), BF16), pltpu.VMEM((tm, d), F32)],
        ),
        out_shape=jax.ShapeDtypeStruct((r, d), F32),
        compiler_params=_params("arbitrary", "arbitrary"),
        name="experts",
    )(block_expert, src_tok, n_used, x, g.reshape(1, d), wg_bf16, wu_bf16, wd_bf16)


def _combine_body(d0_ref, d1_ref, x_ref, gate_ref, g_ref, src_ref, *rest, n_prompt_tiles):
    i = pl.program_id(0)
    n = pl.num_programs(0)
    buf, sem = rest[-2:]

    def copies(block, r, slot):
        t = block * GATHER_ROWS + r
        return (pltpu.make_async_copy(src_ref.at[pl.ds(d0_ref[t], 1), :], buf.at[slot, 0, pl.ds(r, 1), :],
                                      sem.at[slot]),
                pltpu.make_async_copy(src_ref.at[pl.ds(d1_ref[t], 1), :], buf.at[slot, 1, pl.ds(r, 1), :],
                                      sem.at[slot]))

    def issue(block, slot):
        def body(r, c):
            for prio, cp in enumerate(copies(block, r, slot)):
                cp.start(priority=prio)
            return c
        lax.fori_loop(0, GATHER_ROWS, body, 0, unroll=4)

    @pl.when(i == 0)
    def _():
        issue(0, 0)

    @pl.when(i + 1 < n)
    def _():
        issue(i + 1, (i + 1) % 2)

    def wait(r, c):
        for cp in copies(0, r, i % 2):
            cp.wait()
        return c

    lax.fori_loop(0, GATHER_ROWS, wait, 0, unroll=4)
    gate = gate_ref[...]
    y = x_ref[...] + gate[:, 0:1] * buf[i % 2, 0] + gate[:, 1:2] * buf[i % 2, 1]
    if n_prompt_tiles is None:
        rest[0][...] = y
    else:
        y = _rms(y, g_ref[...])

        @pl.when(i < n_prompt_tiles)
        def _():
            rest[0][...] = y

        @pl.when(i >= n_prompt_tiles)
        def _():
            rest[1][...] = y


def _combine(x, gate, expert_rows, d0, d1, final_g=None, n_prompt=None):
    nt, d = x.shape
    tm = GATHER_ROWS
    row = lambda i, a, b: (i, 0)
    if final_g is None:
        g = jnp.ones((d,), F32)
        npt = None
        out_specs = pl.BlockSpec((tm, d), row)
        out_shape = jax.ShapeDtypeStruct((nt, d), F32)
    else:
        g = final_g
        npt = n_prompt // tm
        out_specs = [pl.BlockSpec((tm, d), lambda i, a, b: (jnp.minimum(i, npt - 1), 0)),
                     pl.BlockSpec((tm, d), lambda i, a, b: (jnp.maximum(i - npt, 0), 0))]
        out_shape = [jax.ShapeDtypeStruct((n_prompt, d), F32), jax.ShapeDtypeStruct((nt - n_prompt, d), F32)]
    return pl.pallas_call(
        functools.partial(_combine_body, n_prompt_tiles=npt),
        grid_spec=pltpu.PrefetchScalarGridSpec(
            num_scalar_prefetch=2,
            grid=(nt // tm,),
            in_specs=[pl.BlockSpec((tm, d), row),
                      pl.BlockSpec((tm, LANES), row),
                      pl.BlockSpec((1, d), lambda i, a, b: (0, 0)),
                      pl.BlockSpec(memory_space=pl.ANY)],
            out_specs=out_specs,
            scratch_shapes=[pltpu.VMEM((2, 2, tm, d), F32), pltpu.SemaphoreType.DMA((2,))],
        ),
        out_shape=out_shape,
        compiler_params=_params("arbitrary"),
        name="combine",
    )(d0, d1, x, gate, g.reshape(1, d), expert_rows)


def _moe(x, g, w_router, wg_bf16, wu_bf16, wd_bf16, final_g=None, n_prompt=None):
    nt, d = x.shape
    n_exp = w_router.shape[1]
    tm = MOE_ROWS
    idx, gate = _route(x, g, w_router)
    e_flat = idx[:, :TOP_K].reshape(-1)
    n_assign = nt * TOP_K
    onehot = (e_flat[:, None] == jnp.arange(n_exp, dtype=jnp.int32)[None, :]).astype(jnp.int32)
    rank = jnp.cumsum(onehot, axis=0) - onehot
    pos = jnp.sum(rank * onehot, axis=1)
    counts = jnp.sum(onehot, axis=0)
    padded = (counts + tm - 1) // tm * tm
    pend = jnp.cumsum(padded)
    dest = ((pend - padded)[e_flat] + pos).astype(jnp.int32)
    n_rows = (n_assign // tm + n_exp) * tm
    tok = jnp.arange(n_assign, dtype=jnp.int32) // TOP_K
    src_tok = jnp.zeros((n_rows,), jnp.int32).at[dest].set(tok, unique_indices=True)
    block_expert = jnp.minimum(
        jnp.searchsorted(pend, jnp.arange(n_rows // tm, dtype=jnp.int32) * tm, side='right'), n_exp - 1
    ).astype(jnp.int32)
    n_used = (pend[n_exp - 1:] // tm).astype(jnp.int32)
    y_rows = _experts(x, src_tok, g, n_used, block_expert, wg_bf16, wu_bf16, wd_bf16)
    dest2 = dest.reshape(nt, TOP_K)
    return _combine(x, gate, y_rows, dest2[:, 0], dest2[:, 1], final_g, n_prompt)


def _norm_body(x_ref, g_ref, o_ref):
    o_ref[...] = _rms(x_ref[...], g_ref[...])


def _final_norm(x, g):
    nt, d = x.shape
    tm = ROW_TILE
    return pl.pallas_call(
        _norm_body,
        grid=(nt // tm,),
        in_specs=[pl.BlockSpec((tm, d), lambda i: (i, 0)), pl.BlockSpec((1, d), lambda i: (0, 0))],
        out_specs=pl.BlockSpec((tm, d), lambda i: (i, 0)),
        out_shape=jax.ShapeDtypeStruct((nt, d), F32),
        compiler_params=_params("parallel"),
        name="final_norm",
    )(x, g.reshape(1, d))


def kernel(x_prompt, x_sample, cache_k, cache_v, state_conv, page_table, rel_bias, norm_mix, w_in, w_gate,
           w_sgu, b_sgu, g_sgu, w_conv, w_branch, w_out, norm_ffn, w_ff_gate, w_ff_up, w_ff_down,
           w_router, w_exp_gate, w_exp_up, w_exp_down, norm_final):
    batch, seq, d = x_prompt.shape
    n_seq, t_new, _ = x_sample.shape
    depth = w_in.shape[0]
    n_p = batch * seq
    n_s = n_seq * t_new
    assert t_new == T_NEW and seq % MOBA_BLOCK == 0 and seq // MOBA_BLOCK <= LANES
    assert n_p % FFN_ROWS == 0 and n_s % ROW_TILE == 0 and (n_p + n_s) % FFN_ROWS == 0
    n_kb = seq // MOBA_BLOCK
    group_w = SGU_WIDTH // SGU_GROUPS

    x = jnp.concatenate([x_prompt.reshape(n_p, d), x_sample.reshape(n_s, d)], axis=0)
    bt = rel_bias.T.astype(F32)
    moba_tiles = _moba_bias_tiles(bt)
    far_bias = bt[:, REL_BUCKETS - 1]

    outs = {k: [] for k in ("kp", "vp", "ks", "vs", "cp", "cs", "sv")}
    for l in range(depth):
        q, k, v, uv, bch, kb, vb, km = _project(x, norm_mix[l], w_in[l].astype(BF16))

        kmean = jnp.pad(km[:n_p // MOBA_BLOCK, 0, :].reshape(batch, n_kb, QKV), ((0, 0), (0, LANES - n_kb), (0, 0)))
        o_a = _moba_prompt(q, kb, vb, kmean, moba_tiles, far_bias, batch, seq)
        o_c = _sb_prompt(q, kb, vb, batch, seq)
        ws = jnp.tril(w_sgu[l]).astype(BF16)
        bs = jnp.repeat(b_sgu[l].T, group_w, axis=1)
        o_b, o_d, z_p, _ = _mix(uv, bch, None, ws, bs, g_sgu[l], w_conv[l], seq // SGU_CHUNK, n_p)

        o_att = _attn_sample(q[n_p:], k[n_p:], v[n_p:], cache_k, cache_v, l, page_table, bt)
        per_chunk = SGU_CHUNK // T_NEW
        w4 = jnp.tril(w_sgu[l][:, :T_NEW, :T_NEW])
        ws_s = jnp.einsum('ab,gts->gatbs', jnp.eye(per_chunk, dtype=F32), w4)
        ws_s = ws_s.reshape(SGU_GROUPS, SGU_CHUNK, SGU_CHUNK).astype(BF16)
        bs_s = jnp.tile(jnp.repeat(b_sgu[l][:, :T_NEW].T, group_w, axis=1), (per_chunk, 1))
        prev = state_conv[l]
        zero = jnp.zeros((n_seq, 1, CONV_DIM), F32)
        p1 = jnp.concatenate([prev[:, 1:2], zero, zero, zero], axis=1).reshape(n_s, CONV_DIM)
        p2 = jnp.concatenate([prev[:, 0:1], prev[:, 1:2], zero, zero], axis=1).reshape(n_s, CONV_DIM)
        s_b, s_d, z_s, vn_s = _mix(uv[n_p:], bch[n_p:], (p1, p2), ws_s, bs_s, g_sgu[l], w_conv[l], None, n_s)

        prompt_br = (o_a, o_b, o_c, o_d)
        sample_br = (o_att[:, :QKV // 2], s_b, o_att[:, QKV // 2:], s_d)
        x = _merge(x, norm_mix[l], prompt_br, sample_br, w_gate[l].astype(BF16), w_branch[l].astype(BF16),
                   w_out[l].astype(BF16), n_p)

        i = l // 2
        if l % 2 == 0:
            x = _ffn(x, norm_ffn[l], w_ff_gate[i].astype(BF16), w_ff_up[i].astype(BF16), w_ff_down[i].astype(BF16))
        else:
            x = _moe(x, norm_ffn[l], w_router[i], w_exp_gate[i].astype(BF16), w_exp_up[i].astype(BF16),
                     w_exp_down[i].astype(BF16), norm_final if l == depth - 1 else None, n_p)

        outs["kp"].append(k[:n_p].reshape(batch, seq, N_HEADS, HEAD_DIM))
        outs["vp"].append(v[:n_p].reshape(batch, seq, N_HEADS, HEAD_DIM))
        outs["ks"].append(k[n_p:].reshape(n_seq, t_new, N_HEADS, HEAD_DIM))
        outs["vs"].append(v[n_p:].reshape(n_seq, t_new, N_HEADS, HEAD_DIM))
        outs["cp"].append(z_p.reshape(batch, seq, CONV_DIM)[:, seq - (CONV_WIDTH - 1):])
        outs["cs"].append(z_s.reshape(n_seq, t_new, CONV_DIM)[:, t_new - (CONV_WIDTH - 1):])
        outs["sv"].append(vn_s.reshape(n_seq, t_new, SGU_WIDTH))

    if depth % 2 == 0:
        y_p, y_s = x
    else:
        y = _final_norm(x, norm_final)
        y_p, y_s = y[:n_p], y[n_p:]
    return (y_p.reshape(batch, seq, d), y_s.reshape(n_seq, t_new, d),
            jnp.stack(outs["kp"]), jnp.stack(outs["vp"]), jnp.stack(outs["ks"]), jnp.stack(outs["vs"]),
            jnp.stack(outs["cp"]), jnp.stack(outs["cs"]), jnp.stack(outs["sv"]))
```

```python
import functools
import math

import numpy as np
import jax
import jax.numpy as jnp
from jax import lax
from jax.experimental import pallas as pl
from jax.experimental.pallas import tpu as pltpu

F32 = jnp.float32
BF16 = jnp.bfloat16

NORM_EPS = 1e-6
HEAD_DIM = 64
N_HEADS = 8
QKV = N_HEADS * HEAD_DIM
MOBA_BLOCK = 256
MOBA_TOPK = 3
Q_BLOCK = 128
SGU_CHUNK = 128
SGU_WIDTH = 256
SGU_GROUPS = 4
CONV_DIM = 256
CONV_WIDTH = 3
N_BRANCH = 4
BRANCH_DIM = 256
PAGE = 128
REL_BUCKETS = 32
REL_MAX_DIST = 128
TOP_K = 2
ATT_SCALE = HEAD_DIM ** -0.5
MASKED = -1e30
SB_DROP = -105.0
LANES = 128
ROW_TILE = 256
MOE_ROWS = 512
MOE_FF_TILE = 896
FFN_ROWS = 512
FFN_FF_TILE = 256
GATHER_ROWS = 256
V7X_VMEM_LIMIT = 56 * 1024 * 1024


def _params(*sem):
    return pltpu.CompilerParams(dimension_semantics=sem, vmem_limit_bytes=V7X_VMEM_LIMIT)


def _dot(a, b):
    return jnp.dot(a, b, preferred_element_type=F32)


def _dot_nt(a, b):
    return lax.dot_general(a, b, (((1,), (1,)), ((), ())), preferred_element_type=F32)


def _split_bf16(x):
    hi = x.astype(BF16)
    lo = (x - hi.astype(F32)).astype(BF16)
    return hi, lo


def _dot_nt_f32(a, b):
    ah, al = _split_bf16(a)
    bh, bl = _split_bf16(b)
    return _dot_nt(ah, bh) + _dot_nt(ah, bl) + _dot_nt(al, bh)


def _dot_f32(a, b):
    ah, al = _split_bf16(a)
    bh, bl = _split_bf16(b)
    return _dot(ah, bh) + _dot(ah, bl) + _dot(al, bh)


def _rms(x, g):
    return x * lax.rsqrt(jnp.mean(x * x, axis=-1, keepdims=True) + NORM_EPS) * g


def _gelu_tanh(x):
    return 0.5 * x * (1.0 + jnp.tanh(math.sqrt(2.0 / math.pi) * (x + 0.044715 * (x * x * x))))


def _sigmoid(x):
    return 1.0 / (1.0 + jnp.exp(-x))


def _softplus(z):
    return jnp.maximum(z, 0.0) + jnp.log1p(jnp.exp(-jnp.abs(z)))


def _suffix_sum(lk, u):
    n = lk.shape[0]
    hi, lo = _split_bf16(lk)
    r = _dot(jnp.concatenate([hi, lo], axis=0), u)
    return r[:n] + r[n:]


def _strict_upper(n):
    j = np.arange(n)[:, None]
    s = np.arange(n)[None, :]
    return jnp.asarray((j > s).astype(np.float32), dtype=BF16)


def _t5_bucket_table(max_dist):
    d = np.arange(max_dist + 1)
    max_exact = REL_BUCKETS // 2
    large = max_exact + (np.log(np.maximum(d, 1).astype(np.float32) / np.float32(max_exact))
                         / np.float32(math.log(REL_MAX_DIST / max_exact))
                         * np.float32(REL_BUCKETS - max_exact)).astype(np.int32)
    large = np.minimum(large, REL_BUCKETS - 1)
    return np.where(d < max_exact, d, large).astype(np.int32)


def _bias_by_dist(bt_rows, dist, valid):
    table = _t5_bucket_table(2 * REL_MAX_DIST)
    b = jnp.broadcast_to(bt_rows[:, 0:1], dist.shape)
    for k in range(1, REL_BUCKETS):
        first = int(np.argmax(table >= k))
        b = jnp.where(dist >= first, bt_rows[:, k:k + 1], b)
    return jnp.where(valid, b, MASKED)


def _proj_body(x_ref, g_ref, w_ref, q_ref, k_ref, v_ref, uv_ref, bch_ref, kb_ref, vb_ref, km_ref):
    h = _rms(x_ref[...], g_ref[...]).astype(BF16)

    def mm(lo, hi):
        return _dot(h, w_ref[:, lo:hi])

    q_ref[...] = mm(0, QKV)
    k = mm(QKV, 2 * QKV)
    k_ref[...] = k
    kb_ref[...] = k.astype(BF16)
    km_ref[...] = jnp.broadcast_to(jnp.mean(k, axis=0, keepdims=True), km_ref.shape)
    v = mm(2 * QKV, 3 * QKV)
    v_ref[...] = v
    vb_ref[...] = v.astype(BF16)
    uv_ref[...] = mm(3 * QKV, 3 * QKV + 2 * SGU_WIDTH)
    bch_ref[...] = mm(3 * QKV + 2 * SGU_WIDTH, 3 * QKV + 2 * SGU_WIDTH + 3 * CONV_DIM)


def _project(x, g, w_bf16):
    nt, d = x.shape
    in_dim = w_bf16.shape[1]
    tm = MOBA_BLOCK
    nb = nt // tm
    row = lambda i: (i, 0)
    return pl.pallas_call(
        _proj_body,
        grid=(nb,),
        in_specs=[pl.BlockSpec((tm, d), row),
                  pl.BlockSpec((1, d), lambda i: (0, 0)),
                  pl.BlockSpec((d, in_dim), lambda i: (0, 0))],
        out_specs=[pl.BlockSpec((tm, QKV), row), pl.BlockSpec((tm, QKV), row), pl.BlockSpec((tm, QKV), row),
                   pl.BlockSpec((tm, 2 * SGU_WIDTH), row), pl.BlockSpec((tm, 3 * CONV_DIM), row),
                   pl.BlockSpec((tm, QKV), row), pl.BlockSpec((tm, QKV), row),
                   pl.BlockSpec((None, 8, QKV), lambda i: (i, 0, 0))],
        out_shape=[jax.ShapeDtypeStruct((nt, QKV), F32), jax.ShapeDtypeStruct((nt, QKV), F32),
                   jax.ShapeDtypeStruct((nt, QKV), F32), jax.ShapeDtypeStruct((nt, 2 * SGU_WIDTH), F32),
                   jax.ShapeDtypeStruct((nt, 3 * CONV_DIM), F32),
                   jax.ShapeDtypeStruct((nt, QKV), BF16), jax.ShapeDtypeStruct((nt, QKV), BF16),
                   jax.ShapeDtypeStruct((nb, 8, QKV), F32)],
        compiler_params=_params("parallel"),
        name="proj",
    )(x, g.reshape(1, d), w_bf16)


def _top_blocks(gate, lane, lane_mask):
    lane = lane.astype(F32)
    g = jnp.where(lane_mask, gate, -jnp.inf)
    picks = []
    for _ in range(MOBA_TOPK):
        m = jnp.max(g, axis=-1, keepdims=True)
        idx = jnp.min(jnp.where(g == m, lane, 1e9), axis=-1, keepdims=True)
        picks.append((idx, m > -jnp.inf))
        g = jnp.where(lane == idx, -jnp.inf, g)
    return picks


def _is_picked(picks, j):
    j = jnp.asarray(j).astype(F32)
    c = (picks[0][0] == j) & picks[0][1]
    for idx, ok in picks[1:]:
        c = c | ((idx == j) & ok)
    return c


def _moba_prompt_body(fb_ref, q_ref, k_ref, v_ref, km_ref, bias_ref, spread_ref, o_ref, *, chunk):
    pr = pl.program_id(1)
    qi = pl.program_id(2)
    cur = qi // 2
    half = qi % 2
    rows = 2 * Q_BLOCK
    lane = lax.broadcasted_iota(jnp.int32, (Q_BLOCK, LANES), 1)
    q = q_ref[...] * ATT_SCALE
    qs = jnp.concatenate([jnp.where(lane < HEAD_DIM, q, 0.0), jnp.where(lane >= HEAD_DIM, q, 0.0)], axis=0)
    qsb = qs.astype(BF16)
    lane2 = lax.broadcasted_iota(jnp.int32, (rows, LANES), 1)
    picks = _top_blocks(_dot_nt_f32(qs, km_ref[...]), lane2, lane2 < cur)
    first_head = lax.broadcasted_iota(jnp.int32, (rows, 1), 0) < Q_BLOCK
    far_bias = jnp.where(first_head, fb_ref[2 * pr], fb_ref[2 * pr + 1])

    start = pl.multiple_of(cur * MOBA_BLOCK, MOBA_BLOCK)
    start_p = pl.multiple_of(jnp.maximum(cur - 1, 0) * MOBA_BLOCK, MOBA_BLOCK)
    s_own = _dot_nt(qsb, k_ref[pl.ds(start, MOBA_BLOCK), :]) + bias_ref[half]
    s_prev = _dot_nt(qsb, k_ref[pl.ds(start_p, MOBA_BLOCK), :]) + bias_ref[2 + half]
    s_prev = jnp.where(_is_picked(picks, cur - 1), s_prev, MASKED)
    m = jnp.maximum(jnp.max(s_own, axis=-1, keepdims=True), jnp.max(s_prev, axis=-1, keepdims=True))
    p_own = jnp.exp(s_own - m)
    p_prev = jnp.exp(s_prev - m)
    l = jnp.sum(p_own, axis=-1, keepdims=True) + jnp.sum(p_prev, axis=-1, keepdims=True)
    acc = (_dot(p_own.astype(BF16), v_ref[pl.ds(start, MOBA_BLOCK), :])
           + _dot(p_prev.astype(BF16), v_ref[pl.ds(start_p, MOBA_BLOCK), :]))

    lane2f = lane2.astype(F32)
    picked = (lane2f == picks[0][0]) & picks[0][1]
    for idx, ok in picks[1:]:
        picked = picked | ((lane2f == idx) & ok)
    add_rows = jnp.where(picked & (lane2 < cur - 1), 0.0, MASKED).astype(BF16)

    def chunk_step(c, carry):
        m, l, acc = carry
        start = pl.multiple_of(c * (chunk * MOBA_BLOCK), chunk * MOBA_BLOCK)
        s = _dot_nt(qsb, k_ref[pl.ds(start, chunk * MOBA_BLOCK), :]) + far_bias + _dot(add_rows, spread_ref[c])
        m_new = jnp.maximum(m, jnp.max(s, axis=-1, keepdims=True))
        alpha = jnp.exp(m - m_new)
        p = jnp.exp(s - m_new)
        l = alpha * l + jnp.sum(p, axis=-1, keepdims=True)
        acc = alpha * acc + _dot(p.astype(BF16), v_ref[pl.ds(start, chunk * MOBA_BLOCK), :])
        return m_new, l, acc

    n_older = jnp.maximum(cur - 1, 0)
    m, l, acc = lax.fori_loop(0, (n_older + chunk - 1) // chunk, chunk_step, (m, l, acc))
    out = acc / l
    o_ref[...] = jnp.where(lane < HEAD_DIM, out[:Q_BLOCK], out[Q_BLOCK:])


def _moba_bias_tiles(bt):
    n_pairs = bt.shape[0] // 2
    shape = (n_pairs * 2 * Q_BLOCK, MOBA_BLOCK)
    i = lax.broadcasted_iota(jnp.int32, shape, 0) & (Q_BLOCK - 1)
    c = lax.broadcasted_iota(jnp.int32, shape, 1)
    bt_rows = jnp.repeat(bt, Q_BLOCK, axis=0)
    tiles = []
    for delta in (0, Q_BLOCK, MOBA_BLOCK, MOBA_BLOCK + Q_BLOCK):
        dist = delta + i - c
        tiles.append(_bias_by_dist(bt_rows, dist, dist >= 0).reshape(n_pairs, 2 * Q_BLOCK, MOBA_BLOCK))
    return jnp.stack(tiles, axis=1)


def _moba_prompt(q, kb, vb, kmean, bias_tiles, far_bias, batch, seq):
    nq = seq // Q_BLOCK
    n_kb = seq // MOBA_BLOCK
    chunk = next(c for c in (4, 2, 1) if n_kb % c == 0)
    blk = np.arange(LANES)[None, :, None]
    key = np.arange(chunk * MOBA_BLOCK)[None, None, :]
    cid = np.arange(n_kb // chunk)[:, None, None]
    spread = jnp.asarray((blk == cid * chunk + key // MOBA_BLOCK).astype(np.float32), dtype=BF16)
    return pl.pallas_call(
        functools.partial(_moba_prompt_body, chunk=chunk),
        grid_spec=pltpu.PrefetchScalarGridSpec(
            num_scalar_prefetch=1,
            grid=(batch, 2, nq),
            in_specs=[pl.BlockSpec((Q_BLOCK, LANES), lambda b, p, i, fb: (b * nq + i, p)),
                      pl.BlockSpec((seq, LANES), lambda b, p, i, fb: (b, p)),
                      pl.BlockSpec((seq, LANES), lambda b, p, i, fb: (b, p)),
                      pl.BlockSpec((None, LANES, LANES), lambda b, p, i, fb: (b, 0, p)),
                      pl.BlockSpec((None, 4, 2 * Q_BLOCK, MOBA_BLOCK), lambda b, p, i, fb: (p, 0, 0, 0)),
                      pl.BlockSpec(spread.shape, lambda b, p, i, fb: (0, 0, 0))],
            out_specs=pl.BlockSpec((Q_BLOCK, LANES), lambda b, p, i, fb: (b * nq + i, p)),
        ),
        out_shape=jax.ShapeDtypeStruct((batch * seq, 2 * LANES), F32),
        compiler_params=_params("parallel", "parallel", "arbitrary"),
        name="moba_prompt",
    )(far_bias, q, kb, vb, kmean, bias_tiles, spread)


def _sb_prompt_body(q_ref, k_ref, v_ref, u_ref, o_ref):
    qi = pl.program_id(2)
    rows = 2 * Q_BLOCK
    lane = lax.broadcasted_iota(jnp.int32, (Q_BLOCK, LANES), 1)
    q = q_ref[...] * ATT_SCALE
    qsb = jnp.concatenate([jnp.where(lane < HEAD_DIM, q, 0.0), jnp.where(lane >= HEAD_DIM, q, 0.0)],
                          axis=0).astype(BF16)
    u = u_ref[...]
    col = lax.broadcasted_iota(jnp.int32, (rows, Q_BLOCK), 1)
    qrow = lax.broadcasted_iota(jnp.int32, (rows, Q_BLOCK), 0) & (Q_BLOCK - 1)

    def tile(j, keep, r_acc, acc):
        start = pl.multiple_of(jnp.maximum(j, 0) * Q_BLOCK, Q_BLOCK)
        z = _dot_nt(qsb, k_ref[pl.ds(start, Q_BLOCK), :])
        sp = _softplus(z)
        lk = -sp if keep is None else jnp.where(keep, -sp, 0.0)
        a = jnp.exp(z - sp + _suffix_sum(lk, u) + r_acc)
        if keep is not None:
            a = jnp.where(keep, a, 0.0)
        acc = acc + _dot(a.astype(BF16), v_ref[pl.ds(start, Q_BLOCK), :])
        return r_acc + jnp.sum(lk, axis=-1, keepdims=True), acc

    def exists(j):
        return jnp.broadcast_to(j >= 0, (rows, Q_BLOCK))

    r_acc, acc = tile(qi, col < qrow, jnp.zeros((rows, 1), F32), jnp.zeros((rows, LANES), F32))
    r_acc, acc = tile(qi - 1, exists(qi - 1), r_acc, acc)
    r_acc, acc = tile(qi - 2, exists(qi - 2), r_acc, acc)

    def more(state):
        it, go, _, _ = state
        return (qi - 3 - 2 * it >= 0) & (go > 0)

    def older(state):
        it, _, r_acc, acc = state
        j = qi - 3 - 2 * it
        r_acc, acc = tile(j, None, r_acc, acc)
        r_acc, acc = tile(j - 1, exists(j - 1), r_acc, acc)
        return it + 1, (jnp.max(r_acc) > SB_DROP).astype(jnp.int32), r_acc, acc

    go = (jnp.max(r_acc) > SB_DROP).astype(jnp.int32)
    _, _, _, acc = lax.while_loop(more, older, (jnp.int32(0), go, r_acc, acc))
    o_ref[...] = jnp.where(lane < HEAD_DIM, acc[:Q_BLOCK], acc[Q_BLOCK:])


def _sb_prompt(q, kb, vb, batch, seq):
    nq = seq // Q_BLOCK
    return pl.pallas_call(
        _sb_prompt_body,
        grid=(batch, 2, nq),
        in_specs=[pl.BlockSpec((Q_BLOCK, LANES), lambda b, p, i: (b * nq + i, 2 + p)),
                  pl.BlockSpec((seq, LANES), lambda b, p, i: (b, 2 + p)),
                  pl.BlockSpec((seq, LANES), lambda b, p, i: (b, 2 + p)),
                  pl.BlockSpec((Q_BLOCK, Q_BLOCK), lambda b, p, i: (0, 0))],
        out_specs=pl.BlockSpec((Q_BLOCK, LANES), lambda b, p, i: (b * nq + i, p)),
        out_shape=jax.ShapeDtypeStruct((batch * seq, 2 * LANES), F32),
        compiler_params=_params("parallel", "parallel", "arbitrary"),
        name="sb_prompt",
    )(q, kb, vb, _strict_upper(Q_BLOCK))


T_NEW = 4
SOFTMAX_ROWS = 4 * T_NEW


def _attn_sample_body(pt_ref, qx_ref, knew_ref, vnew_ref, bias_ref, bias_new_ref, u_ref, *rest, n_pages):
    del pt_ref
    k_refs = rest[:n_pages]
    v_refs = rest[n_pages:2 * n_pages]
    o_ref = rest[2 * n_pages]
    n_blocks = n_pages // 2
    sr = SOFTMAX_ROWS
    n_rows = N_HEADS * T_NEW
    past = n_pages * PAGE

    qx = qx_ref[...]
    qxb = qx.astype(BF16)

    blk_lane = lax.broadcasted_iota(jnp.int32, (QKV, LANES), 1)
    km = jnp.zeros((QKV, LANES), F32)
    parts = []
    for p in range(n_pages):
        kp = k_refs[p][...].reshape(QKV, PAGE)
        parts.append(_dot(qxb, kp.astype(BF16)))
        if p % 2 == 0:
            ksum = kp
        else:
            mean = jnp.sum(ksum + kp, axis=1, keepdims=True) * (1.0 / MOBA_BLOCK)
            km = jnp.where(blk_lane == p // 2, mean, km)
    s = jnp.concatenate(parts, axis=1)

    pad_rows = jnp.zeros((LANES - 8, QKV), F32)
    knew = jnp.concatenate([knew_ref[...], pad_rows], axis=0).astype(BF16)
    vnew = jnp.concatenate([vnew_ref[...], pad_rows], axis=0).astype(BF16)
    s_new = _dot_nt(qxb, knew)

    lane_s = lax.broadcasted_iota(jnp.int32, (sr, LANES), 1)
    picks = _top_blocks(_dot_f32(qx[:sr], km), lane_s, lane_s < n_blocks)
    key_block = (lax.broadcasted_iota(jnp.int32, (sr, past), 1) >> 8).astype(F32)
    la = jnp.where(_is_picked(picks, key_block), s[:sr] + bias_ref[...], MASKED)
    la_new = s_new[:sr] + bias_new_ref[...]
    m = jnp.maximum(jnp.max(la, axis=-1, keepdims=True), jnp.max(la_new, axis=-1, keepdims=True))
    pa = jnp.exp(la - m)
    pa_new = jnp.exp(la_new - m)
    denom = jnp.sum(pa, axis=-1, keepdims=True) + jnp.sum(pa_new, axis=-1, keepdims=True)

    u = u_ref[...]
    zb_new = s_new[sr:]
    keep_new = lane_s < (lax.broadcasted_iota(jnp.int32, (sr, LANES), 0) & (T_NEW - 1))
    sp_new = _softplus(zb_new)
    lk_new = jnp.where(keep_new, -sp_new, 0.0)
    a_new = jnp.where(keep_new, jnp.exp(zb_new - sp_new + _suffix_sum(lk_new, u)), 0.0)
    zb = s[sr:]
    sp = _softplus(zb)
    lk = -sp
    stacked = jnp.concatenate([lk[:, p * PAGE:(p + 1) * PAGE] for p in range(n_pages)], axis=0)
    within = _suffix_sum(stacked, u)
    after = jnp.sum(lk_new, axis=-1, keepdims=True)
    cols = [None] * n_pages
    for p in range(n_pages - 1, -1, -1):
        cols[p] = within[p * sr:(p + 1) * sr] + after
        after = after + jnp.sum(lk[:, p * PAGE:(p + 1) * PAGE], axis=-1, keepdims=True)
    a = jnp.exp(zb - sp + jnp.concatenate(cols, axis=1))

    w = jnp.concatenate([pa, a], axis=0).astype(BF16)
    w_new = jnp.concatenate([pa_new, a_new], axis=0).astype(BF16)
    o = _dot(w_new, vnew)
    for p in range(n_pages):
        o = o + _dot_nt(w[:, p * PAGE:(p + 1) * PAGE], v_refs[p][...].reshape(QKV, PAGE).astype(BF16))
    row_head = lax.broadcasted_iota(jnp.int32, (n_rows, QKV), 0) >> 2
    col_head = lax.broadcasted_iota(jnp.int32, (n_rows, QKV), 1) >> 6
    o = jnp.where(row_head == col_head, o, 0.0)
    out = o[:, 0:HEAD_DIM]
    for h in range(1, N_HEADS):
        out = out + o[:, h * HEAD_DIM:(h + 1) * HEAD_DIM]
    scale = jnp.concatenate([1.0 / denom, jnp.ones((n_rows - sr, 1), F32)], axis=0)
    o_ref[...] = out * scale


def _attn_sample(q_s, k_s, v_s, cache_k, cache_v, layer, page_table, bt):
    n_seq, n_pages = page_table.shape
    assert q_s.shape[0] == n_seq * T_NEW and n_pages % 2 == 0
    past = n_pages * PAGE
    n_rows = N_HEADS * T_NEW
    q4 = (q_s * ATT_SCALE).reshape(n_seq, T_NEW, N_HEADS, 1, HEAD_DIM).transpose(0, 2, 1, 3, 4)
    eye = jnp.eye(N_HEADS, dtype=F32).reshape(1, N_HEADS, 1, N_HEADS, 1)
    qx = (q4 * eye).reshape(n_seq, n_rows, QKV)
    pad = ((0, 0), (0, 8 - T_NEW), (0, 0))
    knew = jnp.pad(k_s.reshape(n_seq, T_NEW, QKV), pad)
    vnew = jnp.pad(v_s.reshape(n_seq, T_NEW, QKV), pad)
    ck = cache_k.transpose(0, 1, 3, 4, 2)
    cv = cache_v.transpose(0, 1, 3, 4, 2)

    bt_rows = jnp.repeat(bt[:4], T_NEW, axis=0)

    def key_bias(width, first_key_dist, new_keys):
        r = lax.broadcasted_iota(jnp.int32, (SOFTMAX_ROWS, width), 0)
        c = lax.broadcasted_iota(jnp.int32, (SOFTMAX_ROWS, width), 1)
        dist = first_key_dist + (r & (T_NEW - 1)) - c
        valid = (dist >= 0) & (c < T_NEW) if new_keys else dist >= 0
        return _bias_by_dist(bt_rows, dist, valid)

    bias_past = key_bias(past, past, False)
    bias_new = key_bias(LANES, 0, True)

    def page_spec(p):
        return pl.BlockSpec((None, None, N_HEADS, HEAD_DIM, PAGE),
                            lambda b, pt, p=p: (layer, pt[b * n_pages + p], 0, 0, 0))

    seq3 = lambda b, pt: (b, 0, 0)
    const2 = lambda b, pt: (0, 0)
    out = pl.pallas_call(
        functools.partial(_attn_sample_body, n_pages=n_pages),
        grid_spec=pltpu.PrefetchScalarGridSpec(
            num_scalar_prefetch=1,
            grid=(n_seq,),
            in_specs=[pl.BlockSpec((None, n_rows, QKV), seq3),
                      pl.BlockSpec((None, 8, QKV), seq3),
                      pl.BlockSpec((None, 8, QKV), seq3),
                      pl.BlockSpec((SOFTMAX_ROWS, past), const2),
                      pl.BlockSpec((SOFTMAX_ROWS, LANES), const2),
                      pl.BlockSpec((PAGE, PAGE), const2)]
                     + [page_spec(p) for p in range(n_pages)] * 2,
            out_specs=pl.BlockSpec((None, n_rows, HEAD_DIM), seq3),
        ),
        out_shape=jax.ShapeDtypeStruct((n_seq, n_rows, HEAD_DIM), F32),
        compiler_params=_params("parallel"),
        name="attn_sample",
    )(page_table.reshape(-1), qx, knew, vnew, bias_past, bias_new, _strict_upper(PAGE),
      *([ck] * n_pages), *([cv] * n_pages))
    return out.reshape(n_seq, N_HEADS, T_NEW, HEAD_DIM).transpose(0, 2, 1, 3).reshape(n_seq * T_NEW, QKV)


def _mix_body(*refs, chunks_per_seq):
    if chunks_per_seq is None:
        uv_ref, bch_ref, p1_ref, p2_ref, ws_ref, bs_ref, gs_ref, wc_ref, ob_ref, od_ref, z_ref, vn_ref = refs
    else:
        uv_ref, bch_ref, prev_ref, ws_ref, bs_ref, gs_ref, wc_ref, ob_ref, od_ref, z_ref, vn_ref = refs
    uv = uv_ref[...]
    x = _gelu_tanh(uv[:, SGU_WIDTH:])
    xc = x - jnp.mean(x, axis=-1, keepdims=True)
    vn = xc * lax.rsqrt(jnp.mean(xc * xc, axis=-1, keepdims=True) + NORM_EPS) * gs_ref[...]
    vn_ref[...] = vn
    vnb = vn.astype(BF16)
    group = lax.broadcasted_iota(jnp.int32, (SGU_CHUNK, SGU_WIDTH), 1) >> 6
    s = jnp.zeros((SGU_CHUNK, SGU_WIDTH), F32)
    for g in range(SGU_GROUPS):
        s = jnp.where(group == g, _dot(ws_ref[g], vnb), s)
    ob_ref[...] = _gelu_tanh(uv[:, :SGU_WIDTH]) * (s + bs_ref[...])

    bch = bch_ref[...]
    z = bch[:, CONV_DIM:2 * CONV_DIM] * bch[:, 2 * CONV_DIM:]
    z_ref[...] = z
    row = lax.broadcasted_iota(jnp.int32, (SGU_CHUNK, CONV_DIM), 0)
    r1 = pltpu.roll(z, 1, 0)
    r2 = pltpu.roll(z, 2, 0)
    if chunks_per_seq is None:
        rr = row & (T_NEW - 1)
        z1 = jnp.where(rr >= 1, r1, p1_ref[...])
        z2 = jnp.where(rr >= 2, r2, p2_ref[...])
    else:
        pb = prev_ref[...]
        zp = pb[:, CONV_DIM:2 * CONV_DIM] * pb[:, 2 * CONV_DIM:]
        zp = jnp.where(pl.program_id(0) % chunks_per_seq == 0, 0.0, zp)
        z1 = jnp.where(row >= 1, r1, zp[7:8, :])
        z2 = jnp.where(row >= 2, r2, jnp.where(row == 1, zp[7:8, :], zp[6:7, :]))
    wc = wc_ref[...]
    od_ref[...] = bch[:, :CONV_DIM] * (wc[0:1, :] * z2 + wc[1:2, :] * z1 + wc[2:3, :] * z)


def _mix(uv, bch, history, ws_bf16, bs, gs, wc, chunks_per_seq, n_rows):
    nc = n_rows // SGU_CHUNK
    row = lambda i: (i, 0)
    const = lambda i: (0, 0)
    if chunks_per_seq is None:
        hist_specs = [pl.BlockSpec((SGU_CHUNK, CONV_DIM), row)] * 2
        hist = list(history)
    else:
        sub = SGU_CHUNK // 8
        hist_specs = [pl.BlockSpec((8, 3 * CONV_DIM), lambda i: (jnp.maximum(i * sub - 1, 0), 0))]
        hist = [bch]
    return pl.pallas_call(
        functools.partial(_mix_body, chunks_per_seq=chunks_per_seq),
        grid=(nc,),
        in_specs=[pl.BlockSpec((SGU_CHUNK, 2 * SGU_WIDTH), row), pl.BlockSpec((SGU_CHUNK, 3 * CONV_DIM), row)]
                 + hist_specs
                 + [pl.BlockSpec((SGU_GROUPS, SGU_CHUNK, SGU_CHUNK), lambda i: (0, 0, 0)),
                    pl.BlockSpec((SGU_CHUNK, SGU_WIDTH), const),
                    pl.BlockSpec((1, SGU_WIDTH), const),
                    pl.BlockSpec((8, CONV_DIM), const)],
        out_specs=[pl.BlockSpec((SGU_CHUNK, SGU_WIDTH), row)] * 4,
        out_shape=[jax.ShapeDtypeStruct((n_rows, SGU_WIDTH), F32)] * 4,
        compiler_params=_params("parallel"),
        name="mix_sample" if chunks_per_seq is None else "mix_prompt",
    )(uv, bch, *hist, ws_bf16, bs, gs.reshape(1, SGU_WIDTH), jnp.pad(wc, ((0, 8 - CONV_WIDTH), (0, 0))))


def _merge_body(x_ref, g_ref, pa_ref, pb_ref, pc_ref, pd_ref, sa_ref, sb_ref, sc_ref, sd_ref,
                wg_ref, wb_ref, wo_ref, o_ref, *, n_prompt_tiles):
    x = x_ref[...]
    d = x.shape[1]
    h = _rms(x, g_ref[...]).astype(BF16)
    is_prompt = pl.program_id(0) < n_prompt_tiles
    merged = jnp.zeros(x.shape, F32)
    for n, (p_ref, s_ref) in enumerate(((pa_ref, sa_ref), (pb_ref, sb_ref), (pc_ref, sc_ref), (pd_ref, sd_ref))):
        br = jnp.where(is_prompt, p_ref[...], s_ref[...]).astype(BF16)
        up = _dot(br, wb_ref[n])
        gate = _sigmoid(_dot(h, wg_ref[:, n * d:(n + 1) * d]))
        merged = merged + gate * up
    o_ref[...] = x + _dot(merged.astype(BF16), wo_ref[...])


def _merge(x, g, prompt_br, sample_br, wg_bf16, wb_bf16, wo_bf16, n_prompt):
    nt, d = x.shape
    tm = ROW_TILE
    npt = n_prompt // tm
    row = lambda i: (i, 0)
    p_spec = pl.BlockSpec((tm, BRANCH_DIM), lambda i: (jnp.minimum(i, npt - 1), 0))
    s_spec = pl.BlockSpec((tm, BRANCH_DIM), lambda i: (jnp.maximum(i - npt, 0), 0))
    return pl.pallas_call(
        functools.partial(_merge_body, n_prompt_tiles=npt),
        grid=(nt // tm,),
        in_specs=[pl.BlockSpec((tm, d), row), pl.BlockSpec((1, d), lambda i: (0, 0))]
                 + [p_spec] * N_BRANCH + [s_spec] * N_BRANCH
                 + [pl.BlockSpec((d, N_BRANCH * d), lambda i: (0, 0)),
                    pl.BlockSpec((N_BRANCH, BRANCH_DIM, d), lambda i: (0, 0, 0)),
                    pl.BlockSpec((d, d), lambda i: (0, 0))],
        out_specs=pl.BlockSpec((tm, d), row),
        out_shape=jax.ShapeDtypeStruct((nt, d), F32),
        compiler_params=_params("parallel"),
        name="merge",
    )(x, g.reshape(1, d), *prompt_br, *sample_br, wg_bf16, wb_bf16, wo_bf16)


def _silu(x):
    return x * _sigmoid(x)


def _ffn_body(x_ref, g_ref, wg_ref, wu_ref, wd_ref, o_ref):
    x = x_ref[...]
    h = _rms(x, g_ref[...]).astype(BF16)
    dff = wg_ref.shape[1]
    acc = x
    for c in range(dff // FFN_FF_TILE):
        cols = slice(c * FFN_FF_TILE, (c + 1) * FFN_FF_TILE)
        a = _silu(_dot(h, wg_ref[:, cols])) * _dot(h, wu_ref[:, cols])
        acc = acc + _dot(a.astype(BF16), wd_ref[cols, :])
    o_ref[...] = acc


def _ffn(x, g, wg_bf16, wu_bf16, wd_bf16):
    nt, d = x.shape
    dff = wg_bf16.shape[1]
    tm = FFN_ROWS
    assert dff % FFN_FF_TILE == 0
    resident = pl.Buffered(1)
    return pl.pallas_call(
        _ffn_body,
        grid=(nt // tm,),
        in_specs=[pl.BlockSpec((tm, d), lambda i: (i, 0)), pl.BlockSpec((1, d), lambda i: (0, 0)),
                  pl.BlockSpec((d, dff), lambda i: (0, 0), pipeline_mode=resident),
                  pl.BlockSpec((d, dff), lambda i: (0, 0), pipeline_mode=resident),
                  pl.BlockSpec((dff, d), lambda i: (0, 0), pipeline_mode=resident)],
        out_specs=pl.BlockSpec((tm, d), lambda i: (i, 0)),
        out_shape=jax.ShapeDtypeStruct((nt, d), F32),
        compiler_params=_params("parallel"),
        name="ffn",
    )(x, g.reshape(1, d), wg_bf16, wu_bf16, wd_bf16)


def _route_body(x_ref, g_ref, wr_ref, idx_ref, gate_ref, *, n_exp):
    h = _rms(x_ref[...], g_ref[...])
    logits = _dot_nt_f32(h, wr_ref[...])
    lane_i = lax.broadcasted_iota(jnp.int32, logits.shape, 1)
    lane = lane_i.astype(F32)
    lg = jnp.where(lane_i < n_exp, logits, -jnp.inf)
    m1 = jnp.max(lg, axis=-1, keepdims=True)
    i1 = jnp.min(jnp.where(lg == m1, lane, float(LANES)), axis=-1, keepdims=True)
    lg2 = jnp.where(lane == i1, -jnp.inf, lg)
    m2 = jnp.max(lg2, axis=-1, keepdims=True)
    i2 = jnp.min(jnp.where(lg2 == m2, lane, float(LANES)), axis=-1, keepdims=True)
    e2 = jnp.exp(m2 - m1)
    idx_ref[...] = jnp.where(lane_i == 0, i1, jnp.where(lane_i == 1, i2, 0.0)).astype(jnp.int32)
    gate_ref[...] = jnp.where(lane_i == 0, 1.0 / (1.0 + e2), jnp.where(lane_i == 1, e2 / (1.0 + e2), 0.0))


def _route(x, g, w_router):
    nt, d = x.shape
    n_exp = w_router.shape[1]
    assert n_exp <= LANES
    wr = jnp.pad(w_router.T, ((0, LANES - n_exp), (0, 0)))
    tm = ROW_TILE
    row = lambda i: (i, 0)
    return pl.pallas_call(
        functools.partial(_route_body, n_exp=n_exp),
        grid=(nt // tm,),
        in_specs=[pl.BlockSpec((tm, d), row), pl.BlockSpec((1, d), lambda i: (0, 0)),
                  pl.BlockSpec((LANES, d), lambda i: (0, 0))],
        out_specs=[pl.BlockSpec((tm, LANES), row)] * 2,
        out_shape=[jax.ShapeDtypeStruct((nt, LANES), jnp.int32), jax.ShapeDtypeStruct((nt, LANES), F32)],
        compiler_params=_params("parallel"),
        name="route",
    )(x, g.reshape(1, d), wr)


def _expert_body(be_ref, src_ref, nu_ref, x_hbm, g_ref, wg_ref, wu_ref, wd_ref, o_ref, xbuf, sem):
    del be_ref
    i = pl.program_id(0)
    nb = pl.num_programs(0)
    used = i < nu_ref[0]
    tm = xbuf.shape[1]
    dff = wg_ref.shape[1]
    tf = MOE_FF_TILE

    def row_copy(block, r, slot):
        tok = src_ref[block * tm + r]
        return pltpu.make_async_copy(x_hbm.at[pl.ds(tok, 1), :], xbuf.at[slot, pl.ds(r, 1), :], sem.at[slot])

    def wait_block(slot):
        def body(r, c):
            row_copy(0, r, slot).wait()
            return c
        lax.fori_loop(0, tm, body, 0, unroll=8)

    @pl.when(i == 0)
    def _():
        def body(r, c):
            row_copy(0, r, 0).start()
            return c
        lax.fori_loop(0, tm, body, 0, unroll=8)

    wait_block(i % 2)
    nxt = jnp.minimum(i + 1, nb - 1)

    @pl.when(jnp.logical_not(used))
    def _():
        def body(r, c):
            row_copy(nxt, r, (i + 1) % 2).start()
            return c
        lax.fori_loop(0, tm, body, 0, unroll=8)
        o_ref[...] = jnp.zeros_like(o_ref)

    @pl.when(used)
    def _():
        for r in range(tm):
            row_copy(nxt, r, (i + 1) % 2).start()
        h = _rms(xbuf[i % 2], g_ref[...]).astype(BF16)
        acc = jnp.zeros(o_ref.shape, F32)
        for c in range(dff // tf):
            cols = slice(c * tf, (c + 1) * tf)
            a = _silu(_dot(h, wg_ref[:, cols])) * _dot(h, wu_ref[:, cols])
            acc = acc + _dot(a.astype(BF16), wd_ref[cols, :])
        o_ref[...] = acc

    @pl.when(i == nb - 1)
    def _():
        wait_block((i + 1) % 2)


def _experts(x, src_tok, g, n_used, block_expert, wg_bf16, wu_bf16, wd_bf16):
    r = src_tok.shape[0]
    d = x.shape[1]
    dff = wg_bf16.shape[2]
    tm = MOE_ROWS
    assert dff % MOE_FF_TILE == 0
    resident = pl.Buffered(1)
    wmap = lambda i, be, st, nu: (be[i], 0, 0)
    return pl.pallas_call(
        _expert_body,
        grid_spec=pltpu.PrefetchScalarGridSpec(
            num_scalar_prefetch=3,
            grid=(r // tm,),
            in_specs=[pl.BlockSpec(memory_space=pl.ANY),
                      pl.BlockSpec((1, d), lambda i, be, st, nu: (0, 0)),
                      pl.BlockSpec((None, d, dff), wmap, pipeline_mode=resident),
                      pl.BlockSpec((None, d, dff), wmap, pipeline_mode=resident),
                      pl.BlockSpec((None, dff, d), wmap, pipeline_mode=resident)],
            out_specs=pl.BlockSpec((tm, d), lambda i, be, st, nu: (i, 0)),
            scratch_shapes=[pltpu.VMEM((2, tm, d), F32), pltpu.SemaphoreType.DMA((2,))],
        ),
        out_shape=jax.ShapeDtypeStruct((r, d), F32),
        compiler_params=_params("arbitrary"),
        name="experts",
    )(block_expert, src_tok, n_used, x, g.reshape(1, d), wg_bf16, wu_bf16, wd_bf16)


def _combine_body(d0_ref, d1_ref, x_ref, gate_ref, g_ref, src_ref, *rest, n_prompt_tiles):
    i = pl.program_id(0)
    n = pl.num_programs(0)
    buf, sem = rest[-2:]

    def copies(block, r, slot):
        t = block * GATHER_ROWS + r
        return (pltpu.make_async_copy(src_ref.at[pl.ds(d0_ref[t], 1), :], buf.at[slot, 0, pl.ds(r, 1), :],
                                      sem.at[slot]),
                pltpu.make_async_copy(src_ref.at[pl.ds(d1_ref[t], 1), :], buf.at[slot, 1, pl.ds(r, 1), :],
                                      sem.at[slot]))

    def issue(block, slot):
        def body(r, c):
            for cp in copies(block, r, slot):
                cp.start()
            return c
        lax.fori_loop(0, GATHER_ROWS, body, 0, unroll=4)

    @pl.when(i == 0)
    def _():
        issue(0, 0)

    @pl.when(i + 1 < n)
    def _():
        issue(i + 1, (i + 1) % 2)

    def wait(r, c):
        for cp in copies(0, r, i % 2):
            cp.wait()
        return c

    lax.fori_loop(0, GATHER_ROWS, wait, 0, unroll=4)
    gate = gate_ref[...]
    y = x_ref[...] + gate[:, 0:1] * buf[i % 2, 0] + gate[:, 1:2] * buf[i % 2, 1]
    if n_prompt_tiles is None:
        rest[0][...] = y
    else:
        y = _rms(y, g_ref[...])

        @pl.when(i < n_prompt_tiles)
        def _():
            rest[0][...] = y

        @pl.when(i >= n_prompt_tiles)
        def _():
            rest[1][...] = y


def _combine(x, gate, expert_rows, d0, d1, final_g=None, n_prompt=None):
    nt, d = x.shape
    tm = GATHER_ROWS
    row = lambda i, a, b: (i, 0)
    if final_g is None:
        g = jnp.ones((d,), F32)
        npt = None
        out_specs = pl.BlockSpec((tm, d), row)
        out_shape = jax.ShapeDtypeStruct((nt, d), F32)
    else:
        g = final_g
        npt = n_prompt // tm
        out_specs = [pl.BlockSpec((tm, d), lambda i, a, b: (jnp.minimum(i, npt - 1), 0)),
                     pl.BlockSpec((tm, d), lambda i, a, b: (jnp.maximum(i - npt, 0), 0))]
        out_shape = [jax.ShapeDtypeStruct((n_prompt, d), F32), jax.ShapeDtypeStruct((nt - n_prompt, d), F32)]
    return pl.pallas_call(
        functools.partial(_combine_body, n_prompt_tiles=npt),
        grid_spec=pltpu.PrefetchScalarGridSpec(
            num_scalar_prefetch=2,
            grid=(nt // tm,),
            in_specs=[pl.BlockSpec((tm, d), row),
                      pl.BlockSpec((tm, LANES), row),
                      pl.BlockSpec((1, d), lambda i, a, b: (0, 0)),
                      pl.BlockSpec(memory_space=pl.ANY)],
            out_specs=out_specs,
            scratch_shapes=[pltpu.VMEM((2, 2, tm, d), F32), pltpu.SemaphoreType.DMA((2,))],
        ),
        out_shape=out_shape,
        compiler_params=_params("arbitrary"),
        name="combine",
    )(d0, d1, x, gate, g.reshape(1, d), expert_rows)


def _moe(x, g, w_router, wg_bf16, wu_bf16, wd_bf16, final_g=None, n_prompt=None):
    nt, d = x.shape
    n_exp = w_router.shape[1]
    tm = MOE_ROWS
    idx, gate = _route(x, g, w_router)
    e_flat = idx[:, :TOP_K].reshape(-1)
    n_assign = nt * TOP_K
    onehot = (e_flat[:, None] == jnp.arange(n_exp, dtype=jnp.int32)[None, :]).astype(jnp.int32)
    rank = jnp.cumsum(onehot, axis=0) - onehot
    pos = jnp.sum(rank * onehot, axis=1)
    counts = jnp.sum(onehot, axis=0)
    padded = (counts + tm - 1) // tm * tm
    pend = jnp.cumsum(padded)
    dest = ((pend - padded)[e_flat] + pos).astype(jnp.int32)
    n_rows = (n_assign // tm + n_exp) * tm
    tok = jnp.arange(n_assign, dtype=jnp.int32) // TOP_K
    src_tok = jnp.zeros((n_rows,), jnp.int32).at[dest].set(tok, unique_indices=True)
    block_expert = jnp.minimum(
        jnp.searchsorted(pend, jnp.arange(n_rows // tm, dtype=jnp.int32) * tm, side='right'), n_exp - 1
    ).astype(jnp.int32)
    n_used = (pend[n_exp - 1:] // tm).astype(jnp.int32)
    y_rows = _experts(x, src_tok, g, n_used, block_expert, wg_bf16, wu_bf16, wd_bf16)
    dest2 = dest.reshape(nt, TOP_K)
    return _combine(x, gate, y_rows, dest2[:, 0], dest2[:, 1], final_g, n_prompt)


def _norm_body(x_ref, g_ref, o_ref):
    o_ref[...] = _rms(x_ref[...], g_ref[...])


def _final_norm(x, g):
    nt, d = x.shape
    tm = ROW_TILE
    return pl.pallas_call(
        _norm_body,
        grid=(nt // tm,),
        in_specs=[pl.BlockSpec((tm, d), lambda i: (i, 0)), pl.BlockSpec((1, d), lambda i: (0, 0))],
        out_specs=pl.BlockSpec((tm, d), lambda i: (i, 0)),
        out_shape=jax.ShapeDtypeStruct((nt, d), F32),
        compiler_params=_params("parallel"),
        name="final_norm",
    )(x, g.reshape(1, d))


def kernel(x_prompt, x_sample, cache_k, cache_v, state_conv, page_table, rel_bias, norm_mix, w_in, w_gate,
           w_sgu, b_sgu, g_sgu, w_conv, w_branch, w_out, norm_ffn, w_ff_gate, w_ff_up, w_ff_down,
           w_router, w_exp_gate, w_exp_up, w_exp_down, norm_final):
    batch, seq, d = x_prompt.shape
    n_seq, t_new, _ = x_sample.shape
    depth = w_in.shape[0]
    n_p = batch * seq
    n_s = n_seq * t_new
    assert t_new == T_NEW and seq % MOBA_BLOCK == 0 and seq // MOBA_BLOCK <= LANES
    assert n_p % FFN_ROWS == 0 and n_s % ROW_TILE == 0 and (n_p + n_s) % FFN_ROWS == 0
    n_kb = seq // MOBA_BLOCK
    group_w = SGU_WIDTH // SGU_GROUPS

    x = jnp.concatenate([x_prompt.reshape(n_p, d), x_sample.reshape(n_s, d)], axis=0)
    bt = rel_bias.T.astype(F32)
    moba_tiles = _moba_bias_tiles(bt)
    far_bias = bt[:, REL_BUCKETS - 1]

    outs = {k: [] for k in ("kp", "vp", "ks", "vs", "cp", "cs", "sv")}
    for l in range(depth):
        q, k, v, uv, bch, kb, vb, km = _project(x, norm_mix[l], w_in[l].astype(BF16))

        kmean = jnp.pad(km[:n_p // MOBA_BLOCK, 0, :].reshape(batch, n_kb, QKV), ((0, 0), (0, LANES - n_kb), (0, 0)))
        o_a = _moba_prompt(q, kb, vb, kmean, moba_tiles, far_bias, batch, seq)
        o_c = _sb_prompt(q, kb, vb, batch, seq)
        ws = jnp.tril(w_sgu[l]).astype(BF16)
        bs = jnp.repeat(b_sgu[l].T, group_w, axis=1)
        o_b, o_d, z_p, _ = _mix(uv, bch, None, ws, bs, g_sgu[l], w_conv[l], seq // SGU_CHUNK, n_p)

        o_att = _attn_sample(q[n_p:], k[n_p:], v[n_p:], cache_k, cache_v, l, page_table, bt)
        per_chunk = SGU_CHUNK // T_NEW
        w4 = jnp.tril(w_sgu[l][:, :T_NEW, :T_NEW])
        ws_s = jnp.einsum('ab,gts->gatbs', jnp.eye(per_chunk, dtype=F32), w4)
        ws_s = ws_s.reshape(SGU_GROUPS, SGU_CHUNK, SGU_CHUNK).astype(BF16)
        bs_s = jnp.tile(jnp.repeat(b_sgu[l][:, :T_NEW].T, group_w, axis=1), (per_chunk, 1))
        prev = state_conv[l]
        zero = jnp.zeros((n_seq, 1, CONV_DIM), F32)
        p1 = jnp.concatenate([prev[:, 1:2], zero, zero, zero], axis=1).reshape(n_s, CONV_DIM)
        p2 = jnp.concatenate([prev[:, 0:1], prev[:, 1:2], zero, zero], axis=1).reshape(n_s, CONV_DIM)
        s_b, s_d, z_s, vn_s = _mix(uv[n_p:], bch[n_p:], (p1, p2), ws_s, bs_s, g_sgu[l], w_conv[l], None, n_s)

        prompt_br = (o_a, o_b, o_c, o_d)
        sample_br = (o_att[:, :QKV // 2], s_b, o_att[:, QKV // 2:], s_d)
        x = _merge(x, norm_mix[l], prompt_br, sample_br, w_gate[l].astype(BF16), w_branch[l].astype(BF16),
                   w_out[l].astype(BF16), n_p)

        i = l // 2
        if l % 2 == 0:
            x = _ffn(x, norm_ffn[l], w_ff_gate[i].astype(BF16), w_ff_up[i].astype(BF16), w_ff_down[i].astype(BF16))
        else:
            x = _moe(x, norm_ffn[l], w_router[i], w_exp_gate[i].astype(BF16), w_exp_up[i].astype(BF16),
                     w_exp_down[i].astype(BF16), norm_final if l == depth - 1 else None, n_p)

        outs["kp"].append(k[:n_p].reshape(batch, seq, N_HEADS, HEAD_DIM))
        outs["vp"].append(v[:n_p].reshape(batch, seq, N_HEADS, HEAD_DIM))
        outs["ks"].append(k[n_p:].reshape(n_seq, t_new, N_HEADS, HEAD_DIM))
        outs["vs"].append(v[n_p:].reshape(n_seq, t_new, N_HEADS, HEAD_DIM))
        outs["cp"].append(z_p.reshape(batch, seq, CONV_DIM)[:, seq - (CONV_WIDTH - 1):])
        outs["cs"].append(z_s.reshape(n_seq, t_new, CONV_DIM)[:, t_new - (CONV_WIDTH - 1):])
        outs["sv"].append(vn_s.reshape(n_seq, t_new, SGU_WIDTH))

    if depth % 2 == 0:
        y_p, y_s = x
    else:
        y = _final_norm(x, norm_final)
        y_p, y_s = y[:n_p], y[n_p:]
    return (y_p.reshape(batch, seq, d), y_s.reshape(n_seq, t_new, d),
            jnp.stack(outs["kp"]), jnp.stack(outs["vp"]), jnp.stack(outs["ks"]), jnp.stack(outs["vs"]),
            jnp.stack(outs["cp"]), jnp.stack(outs["cs"]), jnp.stack(outs["sv"]))
```

```python
import functools
import math

import numpy as np
import jax
import jax.numpy as jnp
from jax import lax
from jax.experimental import pallas as pl
from jax.experimental.pallas import tpu as pltpu

F32 = jnp.float32
BF16 = jnp.bfloat16

NORM_EPS = 1e-6
HEAD_DIM = 64
N_HEADS = 8
QKV = N_HEADS * HEAD_DIM
MOBA_BLOCK = 256
MOBA_TOPK = 3
Q_BLOCK = 128
SGU_CHUNK = 128
SGU_WIDTH = 256
SGU_GROUPS = 4
CONV_DIM = 256
CONV_WIDTH = 3
N_BRANCH = 4
BRANCH_DIM = 256
PAGE = 128
REL_BUCKETS = 32
REL_MAX_DIST = 128
TOP_K = 2
ATT_SCALE = HEAD_DIM ** -0.5
MASKED = -1e30
SB_DROP = -105.0
LANES = 128
ROW_TILE = 256
MOE_ROWS = 512
MOE_FF_TILE = 896
FFN_ROWS = 512
FFN_FF_TILE = 256
GATHER_ROWS = 256
V7X_VMEM_LIMIT = 56 * 1024 * 1024


def _params(*sem):
    return pltpu.CompilerParams(dimension_semantics=sem, vmem_limit_bytes=V7X_VMEM_LIMIT)


def _dot(a, b):
    return jnp.dot(a, b, preferred_element_type=F32)


def _dot_nt(a, b):
    return lax.dot_general(a, b, (((1,), (1,)), ((), ())), preferred_element_type=F32)


def _split_bf16(x):
    hi = x.astype(BF16)
    lo = (x - hi.astype(F32)).astype(BF16)
    return hi, lo


def _dot_nt_f32(a, b):
    ah, al = _split_bf16(a)
    bh, bl = _split_bf16(b)
    return _dot_nt(ah, bh) + _dot_nt(ah, bl) + _dot_nt(al, bh)


def _dot_f32(a, b):
    ah, al = _split_bf16(a)
    bh, bl = _split_bf16(b)
    return _dot(ah, bh) + _dot(ah, bl) + _dot(al, bh)


def _rms(x, g):
    return x * lax.rsqrt(jnp.mean(x * x, axis=-1, keepdims=True) + NORM_EPS) * g


def _gelu_tanh(x):
    return 0.5 * x * (1.0 + jnp.tanh(math.sqrt(2.0 / math.pi) * (x + 0.044715 * (x * x * x))))


def _sigmoid(x):
    return 1.0 / (1.0 + jnp.exp(-x))


def _softplus(z):
    return jnp.maximum(z, 0.0) + jnp.log1p(jnp.exp(-jnp.abs(z)))


def _suffix_sum(lk, u):
    n = lk.shape[0]
    hi, lo = _split_bf16(lk)
    r = _dot(jnp.concatenate([hi, lo], axis=0), u)
    return r[:n] + r[n:]


def _strict_upper(n):
    j = np.arange(n)[:, None]
    s = np.arange(n)[None, :]
    return jnp.asarray((j > s).astype(np.float32), dtype=BF16)


def _t5_bucket_table(max_dist):
    d = np.arange(max_dist + 1)
    max_exact = REL_BUCKETS // 2
    large = max_exact + (np.log(np.maximum(d, 1).astype(np.float32) / np.float32(max_exact))
                         / np.float32(math.log(REL_MAX_DIST / max_exact))
                         * np.float32(REL_BUCKETS - max_exact)).astype(np.int32)
    large = np.minimum(large, REL_BUCKETS - 1)
    return np.where(d < max_exact, d, large).astype(np.int32)


def _bias_by_dist(bt_rows, dist, valid):
    table = _t5_bucket_table(2 * REL_MAX_DIST)
    b = jnp.broadcast_to(bt_rows[:, 0:1], dist.shape)
    for k in range(1, REL_BUCKETS):
        first = int(np.argmax(table >= k))
        b = jnp.where(dist >= first, bt_rows[:, k:k + 1], b)
    return jnp.where(valid, b, MASKED)


def _proj_body(x_ref, g_ref, w_ref, q_ref, k_ref, v_ref, uv_ref, bch_ref, kb_ref, vb_ref, km_ref):
    h = _rms(x_ref[...], g_ref[...]).astype(BF16)

    def mm(lo, hi):
        return _dot(h, w_ref[:, lo:hi])

    q_ref[...] = mm(0, QKV)
    k = mm(QKV, 2 * QKV)
    k_ref[...] = k
    kb_ref[...] = k.astype(BF16)
    km_ref[...] = jnp.broadcast_to(jnp.mean(k, axis=0, keepdims=True), km_ref.shape)
    v = mm(2 * QKV, 3 * QKV)
    v_ref[...] = v
    vb_ref[...] = v.astype(BF16)
    uv_ref[...] = mm(3 * QKV, 3 * QKV + 2 * SGU_WIDTH)
    bch_ref[...] = mm(3 * QKV + 2 * SGU_WIDTH, 3 * QKV + 2 * SGU_WIDTH + 3 * CONV_DIM)


def _project(x, g, w_bf16):
    nt, d = x.shape
    in_dim = w_bf16.shape[1]
    tm = MOBA_BLOCK
    nb = nt // tm
    row = lambda i: (i, 0)
    return pl.pallas_call(
        _proj_body,
        grid=(nb,),
        in_specs=[pl.BlockSpec((tm, d), row),
                  pl.BlockSpec((1, d), lambda i: (0, 0)),
                  pl.BlockSpec((d, in_dim), lambda i: (0, 0))],
        out_specs=[pl.BlockSpec((tm, QKV), row), pl.BlockSpec((tm, QKV), row), pl.BlockSpec((tm, QKV), row),
                   pl.BlockSpec((tm, 2 * SGU_WIDTH), row), pl.BlockSpec((tm, 3 * CONV_DIM), row),
                   pl.BlockSpec((tm, QKV), row), pl.BlockSpec((tm, QKV), row),
                   pl.BlockSpec((None, 8, QKV), lambda i: (i, 0, 0))],
        out_shape=[jax.ShapeDtypeStruct((nt, QKV), F32), jax.ShapeDtypeStruct((nt, QKV), F32),
                   jax.ShapeDtypeStruct((nt, QKV), F32), jax.ShapeDtypeStruct((nt, 2 * SGU_WIDTH), F32),
                   jax.ShapeDtypeStruct((nt, 3 * CONV_DIM), F32),
                   jax.ShapeDtypeStruct((nt, QKV), BF16), jax.ShapeDtypeStruct((nt, QKV), BF16),
                   jax.ShapeDtypeStruct((nb, 8, QKV), F32)],
        compiler_params=_params("parallel"),
        name="proj",
    )(x, g.reshape(1, d), w_bf16)


def _top_blocks(gate, lane, lane_mask):
    lane = lane.astype(F32)
    g = jnp.where(lane_mask, gate, -jnp.inf)
    picks = []
    for _ in range(MOBA_TOPK):
        m = jnp.max(g, axis=-1, keepdims=True)
        idx = jnp.min(jnp.where(g == m, lane, 1e9), axis=-1, keepdims=True)
        picks.append((idx, m > -jnp.inf))
        g = jnp.where(lane == idx, -jnp.inf, g)
    return picks


def _is_picked(picks, j):
    j = jnp.asarray(j).astype(F32)
    c = (picks[0][0] == j) & picks[0][1]
    for idx, ok in picks[1:]:
        c = c | ((idx == j) & ok)
    return c


def _moba_prompt_body(fb_ref, q_ref, k_ref, v_ref, km_ref, bias_ref, spread_ref, o_ref, *, chunk, gate_rows):
    pr = pl.program_id(1)
    qi = pl.program_id(2)
    cur = qi
    rows = 2 * MOBA_BLOCK
    lane = lax.broadcasted_iota(jnp.int32, (MOBA_BLOCK, LANES), 1)
    q = q_ref[...] * ATT_SCALE
    qs = jnp.concatenate([jnp.where(lane < HEAD_DIM, q, 0.0), jnp.where(lane >= HEAD_DIM, q, 0.0)], axis=0)
    qsb = qs.astype(BF16)
    first_head = lax.broadcasted_iota(jnp.int32, (rows, 1), 0) < MOBA_BLOCK
    far_bias = jnp.where(first_head, fb_ref[2 * pr], fb_ref[2 * pr + 1])

    blk = lax.broadcasted_iota(jnp.int32, (gate_rows, rows), 0)
    blk_f = blk.astype(F32)
    g = jnp.where(blk < cur, _dot_nt_f32(km_ref[0:gate_rows, :], qs), -jnp.inf)
    picked = None
    for _ in range(MOBA_TOPK):
        best = jnp.max(g, axis=0, keepdims=True)
        hit = blk_f == jnp.min(jnp.where(g == best, blk_f, 1e9), axis=0, keepdims=True)
        chosen = hit & (best > -jnp.inf)
        picked = chosen if picked is None else picked | chosen
        g = jnp.where(hit, -jnp.inf, g)
    fill = jnp.full((LANES // 2 - gate_rows, rows), MASKED, F32)
    older = jnp.where(picked & (blk < cur - 1), 0.0, MASKED)
    prev = jnp.where(picked & (blk == cur - 1), 0.0, MASKED)
    add_rows = jnp.concatenate([older, fill, prev, fill], axis=0).T.astype(BF16)

    start = pl.multiple_of(cur * MOBA_BLOCK, MOBA_BLOCK)
    start_p = pl.multiple_of(jnp.maximum(cur - 1, 0) * MOBA_BLOCK, MOBA_BLOCK)
    q_and_mask = jnp.concatenate([qsb, add_rows], axis=1)
    spread_prev = lax.broadcasted_iota(jnp.int32, (MOBA_BLOCK, LANES), 1) == LANES // 2 + cur - 1
    keys_prev = jnp.concatenate([k_ref[pl.ds(start_p, MOBA_BLOCK), :], spread_prev.astype(BF16)], axis=1)
    s_own = _dot_nt(qsb, k_ref[pl.ds(start, MOBA_BLOCK), :]) + bias_ref[0]
    s_prev = _dot_nt(q_and_mask, keys_prev) + bias_ref[1]
    m = jnp.maximum(jnp.max(s_own, axis=-1, keepdims=True), jnp.max(s_prev, axis=-1, keepdims=True))
    p_own = jnp.exp(s_own - m)
    p_prev = jnp.exp(s_prev - m)
    l = jnp.sum(p_own, axis=-1, keepdims=True) + jnp.sum(p_prev, axis=-1, keepdims=True)
    acc = (_dot(p_own.astype(BF16), v_ref[pl.ds(start, MOBA_BLOCK), :])
           + _dot(p_prev.astype(BF16), v_ref[pl.ds(start_p, MOBA_BLOCK), :]))

    def chunk_step(c, carry):
        m, l, acc = carry
        start = pl.multiple_of(c * (chunk * MOBA_BLOCK), chunk * MOBA_BLOCK)
        keys = jnp.concatenate([k_ref[pl.ds(start, chunk * MOBA_BLOCK), :], spread_ref[c]], axis=1)
        s = _dot_nt(q_and_mask, keys) + far_bias
        m_new = jnp.maximum(m, jnp.max(s, axis=-1, keepdims=True))
        alpha = jnp.exp(m - m_new)
        p = jnp.exp(s - m_new)
        l = alpha * l + jnp.sum(p, axis=-1, keepdims=True)
        acc = alpha * acc + _dot(p.astype(BF16), v_ref[pl.ds(start, chunk * MOBA_BLOCK), :])
        return m_new, l, acc

    n_older = jnp.maximum(cur - 1, 0)
    m, l, acc = lax.fori_loop(0, (n_older + chunk - 1) // chunk, chunk_step, (m, l, acc))
    out = acc / l
    o_ref[...] = jnp.where(lane < HEAD_DIM, out[:MOBA_BLOCK], out[MOBA_BLOCK:])


def _moba_bias_tiles(bt):
    n_pairs = bt.shape[0] // 2
    shape = (n_pairs * 2 * MOBA_BLOCK, MOBA_BLOCK)
    i = lax.broadcasted_iota(jnp.int32, shape, 0) & (MOBA_BLOCK - 1)
    c = lax.broadcasted_iota(jnp.int32, shape, 1)
    bt_rows = jnp.repeat(bt, MOBA_BLOCK, axis=0)
    tiles = []
    for delta in (0, MOBA_BLOCK):
        dist = delta + i - c
        tiles.append(_bias_by_dist(bt_rows, dist, dist >= 0).reshape(n_pairs, 2 * MOBA_BLOCK, MOBA_BLOCK))
    return jnp.stack(tiles, axis=1)


def _moba_prompt(q, kb, vb, kmean, bias_tiles, far_bias, batch, seq):
    n_kb = seq // MOBA_BLOCK
    nq = n_kb
    chunk = next(c for c in (4, 2, 1) if n_kb % c == 0)
    gate_rows = -(-n_kb // 8) * 8
    assert gate_rows <= LANES // 2
    blk = np.arange(LANES)[None, None, :]
    key = np.arange(chunk * MOBA_BLOCK)[None, :, None]
    cid = np.arange(n_kb // chunk)[:, None, None]
    spread = jnp.asarray((blk == cid * chunk + key // MOBA_BLOCK).astype(np.float32), dtype=BF16)
    return pl.pallas_call(
        functools.partial(_moba_prompt_body, chunk=chunk, gate_rows=gate_rows),
        grid_spec=pltpu.PrefetchScalarGridSpec(
            num_scalar_prefetch=1,
            grid=(batch, 2, nq),
            in_specs=[pl.BlockSpec((MOBA_BLOCK, LANES), lambda b, p, i, fb: (b * nq + i, p)),
                      pl.BlockSpec((seq, LANES), lambda b, p, i, fb: (b, p)),
                      pl.BlockSpec((seq, LANES), lambda b, p, i, fb: (b, p)),
                      pl.BlockSpec((None, LANES, LANES), lambda b, p, i, fb: (b, 0, p)),
                      pl.BlockSpec((None, 2, 2 * MOBA_BLOCK, MOBA_BLOCK), lambda b, p, i, fb: (p, 0, 0, 0)),
                      pl.BlockSpec(spread.shape, lambda b, p, i, fb: (0, 0, 0))],
            out_specs=pl.BlockSpec((MOBA_BLOCK, LANES), lambda b, p, i, fb: (b * nq + i, p)),
        ),
        out_shape=jax.ShapeDtypeStruct((batch * seq, 2 * LANES), F32),
        compiler_params=_params("parallel", "parallel", "arbitrary"),
        name="moba_prompt",
    )(far_bias, q, kb, vb, kmean, bias_tiles, spread)


def _sb_prompt_body(q_ref, k_ref, v_ref, u_ref, o_ref):
    qi = pl.program_id(2)
    rows = 2 * Q_BLOCK
    lane = lax.broadcasted_iota(jnp.int32, (Q_BLOCK, LANES), 1)
    q = q_ref[...] * ATT_SCALE
    qsb = jnp.concatenate([jnp.where(lane < HEAD_DIM, q, 0.0), jnp.where(lane >= HEAD_DIM, q, 0.0)],
                          axis=0).astype(BF16)
    u = u_ref[...]
    col = lax.broadcasted_iota(jnp.int32, (rows, Q_BLOCK), 1)
    qrow = lax.broadcasted_iota(jnp.int32, (rows, Q_BLOCK), 0) & (Q_BLOCK - 1)

    def tile(j, keep, r_acc, acc):
        start = pl.multiple_of(jnp.maximum(j, 0) * Q_BLOCK, Q_BLOCK)
        z = _dot_nt(qsb, k_ref[pl.ds(start, Q_BLOCK), :])
        sp = _softplus(z)
        lk = -sp if keep is None else jnp.where(keep, -sp, 0.0)
        a = jnp.exp(z - sp + _suffix_sum(lk, u) + r_acc)
        if keep is not None:
            a = jnp.where(keep, a, 0.0)
        acc = acc + _dot(a.astype(BF16), v_ref[pl.ds(start, Q_BLOCK), :])
        return r_acc + jnp.sum(lk, axis=-1, keepdims=True), acc

    def exists(j):
        return jnp.broadcast_to(j >= 0, (rows, Q_BLOCK))

    r_acc, acc = tile(qi, col < qrow, jnp.zeros((rows, 1), F32), jnp.zeros((rows, LANES), F32))
    r_acc, acc = tile(qi - 1, exists(qi - 1), r_acc, acc)
    r_acc, acc = tile(qi - 2, exists(qi - 2), r_acc, acc)

    def more(state):
        it, go, _, _ = state
        return (qi - 3 - 2 * it >= 0) & (go > 0)

    def older(state):
        it, _, r_acc, acc = state
        j = qi - 3 - 2 * it
        r_acc, acc = tile(j, None, r_acc, acc)
        r_acc, acc = tile(j - 1, exists(j - 1), r_acc, acc)
        return it + 1, (jnp.max(r_acc) > SB_DROP).astype(jnp.int32), r_acc, acc

    go = (jnp.max(r_acc) > SB_DROP).astype(jnp.int32)
    _, _, _, acc = lax.while_loop(more, older, (jnp.int32(0), go, r_acc, acc))
    o_ref[...] = jnp.where(lane < HEAD_DIM, acc[:Q_BLOCK], acc[Q_BLOCK:])


def _sb_prompt(q, kb, vb, batch, seq):
    nq = seq // Q_BLOCK
    return pl.pallas_call(
        _sb_prompt_body,
        grid=(batch, 2, nq),
        in_specs=[pl.BlockSpec((Q_BLOCK, LANES), lambda b, p, i: (b * nq + i, 2 + p)),
                  pl.BlockSpec((seq, LANES), lambda b, p, i: (b, 2 + p)),
                  pl.BlockSpec((seq, LANES), lambda b, p, i: (b, 2 + p)),
                  pl.BlockSpec((Q_BLOCK, Q_BLOCK), lambda b, p, i: (0, 0))],
        out_specs=pl.BlockSpec((Q_BLOCK, LANES), lambda b, p, i: (b * nq + i, p)),
        out_shape=jax.ShapeDtypeStruct((batch * seq, 2 * LANES), F32),
        compiler_params=_params("parallel", "parallel", "arbitrary"),
        name="sb_prompt",
    )(q, kb, vb, _strict_upper(Q_BLOCK))


T_NEW = 4
SOFTMAX_ROWS = 4 * T_NEW


def _attn_sample_body(pt_ref, qx_ref, knew_ref, vnew_ref, bias_ref, bias_new_ref, u_ref, *rest, n_pages):
    del pt_ref
    k_refs = rest[:n_pages]
    v_refs = rest[n_pages:2 * n_pages]
    o_ref = rest[2 * n_pages]
    n_blocks = n_pages // 2
    sr = SOFTMAX_ROWS
    n_rows = N_HEADS * T_NEW
    past = n_pages * PAGE

    qx = qx_ref[...]
    qxb = qx.astype(BF16)

    blk_lane = lax.broadcasted_iota(jnp.int32, (QKV, LANES), 1)
    km = jnp.zeros((QKV, LANES), F32)
    parts = []
    for p in range(n_pages):
        kp = k_refs[p][...].reshape(QKV, PAGE)
        parts.append(_dot(qxb, kp.astype(BF16)))
        if p % 2 == 0:
            ksum = kp
        else:
            mean = jnp.sum(ksum + kp, axis=1, keepdims=True) * (1.0 / MOBA_BLOCK)
            km = jnp.where(blk_lane == p // 2, mean, km)
    s = jnp.concatenate(parts, axis=1)

    pad_rows = jnp.zeros((LANES - 8, QKV), F32)
    knew = jnp.concatenate([knew_ref[...], pad_rows], axis=0).astype(BF16)
    vnew = jnp.concatenate([vnew_ref[...], pad_rows], axis=0).astype(BF16)
    s_new = _dot_nt(qxb, knew)

    lane_s = lax.broadcasted_iota(jnp.int32, (sr, LANES), 1)
    picks = _top_blocks(_dot_f32(qx[:sr], km), lane_s, lane_s < n_blocks)
    key_block = (lax.broadcasted_iota(jnp.int32, (sr, past), 1) >> 8).astype(F32)
    la = jnp.where(_is_picked(picks, key_block), s[:sr] + bias_ref[...], MASKED)
    la_new = s_new[:sr] + bias_new_ref[...]
    m = jnp.maximum(jnp.max(la, axis=-1, keepdims=True), jnp.max(la_new, axis=-1, keepdims=True))
    pa = jnp.exp(la - m)
    pa_new = jnp.exp(la_new - m)
    denom = jnp.sum(pa, axis=-1, keepdims=True) + jnp.sum(pa_new, axis=-1, keepdims=True)

    u = u_ref[...]
    zb_new = s_new[sr:]
    keep_new = lane_s < (lax.broadcasted_iota(jnp.int32, (sr, LANES), 0) & (T_NEW - 1))
    sp_new = _softplus(zb_new)
    lk_new = jnp.where(keep_new, -sp_new, 0.0)
    a_new = jnp.where(keep_new, jnp.exp(zb_new - sp_new + _suffix_sum(lk_new, u)), 0.0)
    zb = s[sr:]
    sp = _softplus(zb)
    lk = -sp
    stacked = jnp.concatenate([lk[:, p * PAGE:(p + 1) * PAGE] for p in range(n_pages)], axis=0)
    within = _suffix_sum(stacked, u)
    after = jnp.sum(lk_new, axis=-1, keepdims=True)
    cols = [None] * n_pages
    for p in range(n_pages - 1, -1, -1):
        cols[p] = within[p * sr:(p + 1) * sr] + after
        after = after + jnp.sum(lk[:, p * PAGE:(p + 1) * PAGE], axis=-1, keepdims=True)
    a = jnp.exp(zb - sp + jnp.concatenate(cols, axis=1))

    w = jnp.concatenate([pa, a], axis=0).astype(BF16)
    w_new = jnp.concatenate([pa_new, a_new], axis=0).astype(BF16)
    o = _dot(w_new, vnew)
    for p in range(n_pages):
        o = o + _dot_nt(w[:, p * PAGE:(p + 1) * PAGE], v_refs[p][...].reshape(QKV, PAGE).astype(BF16))
    row_head = lax.broadcasted_iota(jnp.int32, (n_rows, QKV), 0) >> 2
    col_head = lax.broadcasted_iota(jnp.int32, (n_rows, QKV), 1) >> 6
    o = jnp.where(row_head == col_head, o, 0.0)
    out = o[:, 0:HEAD_DIM]
    for h in range(1, N_HEADS):
        out = out + o[:, h * HEAD_DIM:(h + 1) * HEAD_DIM]
    scale = jnp.concatenate([1.0 / denom, jnp.ones((n_rows - sr, 1), F32)], axis=0)
    o_ref[...] = out * scale


def _attn_sample(q_s, k_s, v_s, cache_k, cache_v, layer, page_table, bt):
    n_seq, n_pages = page_table.shape
    assert q_s.shape[0] == n_seq * T_NEW and n_pages % 2 == 0
    past = n_pages * PAGE
    n_rows = N_HEADS * T_NEW
    q4 = (q_s * ATT_SCALE).reshape(n_seq, T_NEW, N_HEADS, 1, HEAD_DIM).transpose(0, 2, 1, 3, 4)
    eye = jnp.eye(N_HEADS, dtype=F32).reshape(1, N_HEADS, 1, N_HEADS, 1)
    qx = (q4 * eye).reshape(n_seq, n_rows, QKV)
    pad = ((0, 0), (0, 8 - T_NEW), (0, 0))
    knew = jnp.pad(k_s.reshape(n_seq, T_NEW, QKV), pad)
    vnew = jnp.pad(v_s.reshape(n_seq, T_NEW, QKV), pad)
    ck = cache_k.transpose(0, 1, 3, 4, 2)
    cv = cache_v.transpose(0, 1, 3, 4, 2)

    bt_rows = jnp.repeat(bt[:4], T_NEW, axis=0)

    def key_bias(width, first_key_dist, new_keys):
        r = lax.broadcasted_iota(jnp.int32, (SOFTMAX_ROWS, width), 0)
        c = lax.broadcasted_iota(jnp.int32, (SOFTMAX_ROWS, width), 1)
        dist = first_key_dist + (r & (T_NEW - 1)) - c
        valid = (dist >= 0) & (c < T_NEW) if new_keys else dist >= 0
        return _bias_by_dist(bt_rows, dist, valid)

    bias_past = key_bias(past, past, False)
    bias_new = key_bias(LANES, 0, True)

    def page_spec(p):
        return pl.BlockSpec((None, None, N_HEADS, HEAD_DIM, PAGE),
                            lambda b, pt, p=p: (layer, pt[b * n_pages + p], 0, 0, 0))

    seq3 = lambda b, pt: (b, 0, 0)
    const2 = lambda b, pt: (0, 0)
    out = pl.pallas_call(
        functools.partial(_attn_sample_body, n_pages=n_pages),
        grid_spec=pltpu.PrefetchScalarGridSpec(
            num_scalar_prefetch=1,
            grid=(n_seq,),
            in_specs=[pl.BlockSpec((None, n_rows, QKV), seq3),
                      pl.BlockSpec((None, 8, QKV), seq3),
                      pl.BlockSpec((None, 8, QKV), seq3),
                      pl.BlockSpec((SOFTMAX_ROWS, past), const2),
                      pl.BlockSpec((SOFTMAX_ROWS, LANES), const2),
                      pl.BlockSpec((PAGE, PAGE), const2)]
                     + [page_spec(p) for p in range(n_pages)] * 2,
            out_specs=pl.BlockSpec((None, n_rows, HEAD_DIM), seq3),
        ),
        out_shape=jax.ShapeDtypeStruct((n_seq, n_rows, HEAD_DIM), F32),
        compiler_params=_params("parallel"),
        name="attn_sample",
    )(page_table.reshape(-1), qx, knew, vnew, bias_past, bias_new, _strict_upper(PAGE),
      *([ck] * n_pages), *([cv] * n_pages))
    return out.reshape(n_seq, N_HEADS, T_NEW, HEAD_DIM).transpose(0, 2, 1, 3).reshape(n_seq * T_NEW, QKV)


def _mix_body(*refs, chunks_per_seq):
    if chunks_per_seq is None:
        uv_ref, bch_ref, p1_ref, p2_ref, ws_ref, bs_ref, gs_ref, wc_ref, ob_ref, od_ref, z_ref, vn_ref = refs
    else:
        uv_ref, bch_ref, prev_ref, ws_ref, bs_ref, gs_ref, wc_ref, ob_ref, od_ref, z_ref, vn_ref = refs
    uv = uv_ref[...]
    x = _gelu_tanh(uv[:, SGU_WIDTH:])
    xc = x - jnp.mean(x, axis=-1, keepdims=True)
    vn = xc * lax.rsqrt(jnp.mean(xc * xc, axis=-1, keepdims=True) + NORM_EPS) * gs_ref[...]
    vn_ref[...] = vn
    vnb = vn.astype(BF16)
    group = lax.broadcasted_iota(jnp.int32, (SGU_CHUNK, SGU_WIDTH), 1) >> 6
    s = jnp.zeros((SGU_CHUNK, SGU_WIDTH), F32)
    for g in range(SGU_GROUPS):
        s = jnp.where(group == g, _dot(ws_ref[g], vnb), s)
    ob_ref[...] = _gelu_tanh(uv[:, :SGU_WIDTH]) * (s + bs_ref[...])

    bch = bch_ref[...]
    z = bch[:, CONV_DIM:2 * CONV_DIM] * bch[:, 2 * CONV_DIM:]
    z_ref[...] = z
    row = lax.broadcasted_iota(jnp.int32, (SGU_CHUNK, CONV_DIM), 0)
    r1 = pltpu.roll(z, 1, 0)
    r2 = pltpu.roll(z, 2, 0)
    if chunks_per_seq is None:
        rr = row & (T_NEW - 1)
        z1 = jnp.where(rr >= 1, r1, p1_ref[...])
        z2 = jnp.where(rr >= 2, r2, p2_ref[...])
    else:
        pb = prev_ref[...]
        zp = pb[:, CONV_DIM:2 * CONV_DIM] * pb[:, 2 * CONV_DIM:]
        zp = jnp.where(pl.program_id(0) % chunks_per_seq == 0, 0.0, zp)
        z1 = jnp.where(row >= 1, r1, zp[7:8, :])
        z2 = jnp.where(row >= 2, r2, jnp.where(row == 1, zp[7:8, :], zp[6:7, :]))
    wc = wc_ref[...]
    od_ref[...] = bch[:, :CONV_DIM] * (wc[0:1, :] * z2 + wc[1:2, :] * z1 + wc[2:3, :] * z)


def _mix(uv, bch, history, ws_bf16, bs, gs, wc, chunks_per_seq, n_rows):
    nc = n_rows // SGU_CHUNK
    row = lambda i: (i, 0)
    const = lambda i: (0, 0)
    if chunks_per_seq is None:
        hist_specs = [pl.BlockSpec((SGU_CHUNK, CONV_DIM), row)] * 2
        hist = list(history)
    else:
        sub = SGU_CHUNK // 8
        hist_specs = [pl.BlockSpec((8, 3 * CONV_DIM), lambda i: (jnp.maximum(i * sub - 1, 0), 0))]
        hist = [bch]
    return pl.pallas_call(
        functools.partial(_mix_body, chunks_per_seq=chunks_per_seq),
        grid=(nc,),
        in_specs=[pl.BlockSpec((SGU_CHUNK, 2 * SGU_WIDTH), row), pl.BlockSpec((SGU_CHUNK, 3 * CONV_DIM), row)]
                 + hist_specs
                 + [pl.BlockSpec((SGU_GROUPS, SGU_CHUNK, SGU_CHUNK), lambda i: (0, 0, 0)),
                    pl.BlockSpec((SGU_CHUNK, SGU_WIDTH), const),
                    pl.BlockSpec((1, SGU_WIDTH), const),
                    pl.BlockSpec((8, CONV_DIM), const)],
        out_specs=[pl.BlockSpec((SGU_CHUNK, SGU_WIDTH), row)] * 4,
        out_shape=[jax.ShapeDtypeStruct((n_rows, SGU_WIDTH), F32)] * 4,
        compiler_params=_params("parallel"),
        name="mix_sample" if chunks_per_seq is None else "mix_prompt",
    )(uv, bch, *hist, ws_bf16, bs, gs.reshape(1, SGU_WIDTH), jnp.pad(wc, ((0, 8 - CONV_WIDTH), (0, 0))))


def _merge_body(x_ref, g_ref, pa_ref, pb_ref, pc_ref, pd_ref, sa_ref, sb_ref, sc_ref, sd_ref,
                wg_ref, wb_ref, wo_ref, o_ref, *, n_prompt_tiles):
    x = x_ref[...]
    d = x.shape[1]
    h = _rms(x, g_ref[...]).astype(BF16)
    is_prompt = pl.program_id(0) < n_prompt_tiles
    merged = jnp.zeros(x.shape, F32)
    for n, (p_ref, s_ref) in enumerate(((pa_ref, sa_ref), (pb_ref, sb_ref), (pc_ref, sc_ref), (pd_ref, sd_ref))):
        br = jnp.where(is_prompt, p_ref[...], s_ref[...]).astype(BF16)
        up = _dot(br, wb_ref[n])
        gate = _sigmoid(_dot(h, wg_ref[:, n * d:(n + 1) * d]))
        merged = merged + gate * up
    o_ref[...] = x + _dot(merged.astype(BF16), wo_ref[...])


def _merge(x, g, prompt_br, sample_br, wg_bf16, wb_bf16, wo_bf16, n_prompt):
    nt, d = x.shape
    tm = ROW_TILE
    npt = n_prompt // tm
    row = lambda i: (i, 0)
    p_spec = pl.BlockSpec((tm, BRANCH_DIM), lambda i: (jnp.minimum(i, npt - 1), 0))
    s_spec = pl.BlockSpec((tm, BRANCH_DIM), lambda i: (jnp.maximum(i - npt, 0), 0))
    return pl.pallas_call(
        functools.partial(_merge_body, n_prompt_tiles=npt),
        grid=(nt // tm,),
        in_specs=[pl.BlockSpec((tm, d), row), pl.BlockSpec((1, d), lambda i: (0, 0))]
                 + [p_spec] * N_BRANCH + [s_spec] * N_BRANCH
                 + [pl.BlockSpec((d, N_BRANCH * d), lambda i: (0, 0)),
                    pl.BlockSpec((N_BRANCH, BRANCH_DIM, d), lambda i: (0, 0, 0)),
                    pl.BlockSpec((d, d), lambda i: (0, 0))],
        out_specs=pl.BlockSpec((tm, d), row),
        out_shape=jax.ShapeDtypeStruct((nt, d), F32),
        compiler_params=_params("parallel"),
        name="merge",
    )(x, g.reshape(1, d), *prompt_br, *sample_br, wg_bf16, wb_bf16, wo_bf16)


def _silu(x):
    return x * _sigmoid(x)


def _ffn_body(x_ref, g_ref, wg_ref, wu_ref, wd_ref, o_ref):
    x = x_ref[...]
    h = _rms(x, g_ref[...]).astype(BF16)
    dff = wg_ref.shape[1]
    acc = x
    for c in range(dff // FFN_FF_TILE):
        cols = slice(c * FFN_FF_TILE, (c + 1) * FFN_FF_TILE)
        a = _silu(_dot(h, wg_ref[:, cols])) * _dot(h, wu_ref[:, cols])
        acc = acc + _dot(a.astype(BF16), wd_ref[cols, :])
    o_ref[...] = acc


def _ffn(x, g, wg_bf16, wu_bf16, wd_bf16):
    nt, d = x.shape
    dff = wg_bf16.shape[1]
    tm = FFN_ROWS
    assert dff % FFN_FF_TILE == 0
    resident = pl.Buffered(1)
    return pl.pallas_call(
        _ffn_body,
        grid=(nt // tm,),
        in_specs=[pl.BlockSpec((tm, d), lambda i: (i, 0)), pl.BlockSpec((1, d), lambda i: (0, 0)),
                  pl.BlockSpec((d, dff), lambda i: (0, 0), pipeline_mode=resident),
                  pl.BlockSpec((d, dff), lambda i: (0, 0), pipeline_mode=resident),
                  pl.BlockSpec((dff, d), lambda i: (0, 0), pipeline_mode=resident)],
        out_specs=pl.BlockSpec((tm, d), lambda i: (i, 0)),
        out_shape=jax.ShapeDtypeStruct((nt, d), F32),
        compiler_params=_params("parallel"),
        name="ffn",
    )(x, g.reshape(1, d), wg_bf16, wu_bf16, wd_bf16)


def _route_body(x_ref, g_ref, wr_ref, idx_ref, gate_ref, *, n_exp):
    h = _rms(x_ref[...], g_ref[...])
    logits = _dot_nt_f32(h, wr_ref[...])
    lane_i = lax.broadcasted_iota(jnp.int32, logits.shape, 1)
    lane = lane_i.astype(F32)
    lg = jnp.where(lane_i < n_exp, logits, -jnp.inf)
    m1 = jnp.max(lg, axis=-1, keepdims=True)
    i1 = jnp.min(jnp.where(lg == m1, lane, float(LANES)), axis=-1, keepdims=True)
    lg2 = jnp.where(lane == i1, -jnp.inf, lg)
    m2 = jnp.max(lg2, axis=-1, keepdims=True)
    i2 = jnp.min(jnp.where(lg2 == m2, lane, float(LANES)), axis=-1, keepdims=True)
    e2 = jnp.exp(m2 - m1)
    idx_ref[...] = jnp.where(lane_i == 0, i1, jnp.where(lane_i == 1, i2, 0.0)).astype(jnp.int32)
    gate_ref[...] = jnp.where(lane_i == 0, 1.0 / (1.0 + e2), jnp.where(lane_i == 1, e2 / (1.0 + e2), 0.0))


def _route(x, g, w_router):
    nt, d = x.shape
    n_exp = w_router.shape[1]
    assert n_exp <= LANES
    wr = jnp.pad(w_router.T, ((0, LANES - n_exp), (0, 0)))
    tm = ROW_TILE
    row = lambda i: (i, 0)
    return pl.pallas_call(
        functools.partial(_route_body, n_exp=n_exp),
        grid=(nt // tm,),
        in_specs=[pl.BlockSpec((tm, d), row), pl.BlockSpec((1, d), lambda i: (0, 0)),
                  pl.BlockSpec((LANES, d), lambda i: (0, 0))],
        out_specs=[pl.BlockSpec((tm, LANES), row)] * 2,
        out_shape=[jax.ShapeDtypeStruct((nt, LANES), jnp.int32), jax.ShapeDtypeStruct((nt, LANES), F32)],
        compiler_params=_params("parallel"),
        name="route",
    )(x, g.reshape(1, d), wr)


def _expert_body(be_ref, src_ref, nu_ref, x_hbm, g_ref, wg_ref, wu_ref, wd_ref, o_ref, xbuf, sem):
    del be_ref
    i = pl.program_id(0)
    nb = pl.num_programs(0)
    used = i < nu_ref[0]
    tm = xbuf.shape[1]
    dff = wg_ref.shape[1]
    tf = MOE_FF_TILE

    def row_copy(block, r, slot):
        tok = src_ref[block * tm + r]
        return pltpu.make_async_copy(x_hbm.at[pl.ds(tok, 1), :], xbuf.at[slot, pl.ds(r, 1), :], sem.at[slot])

    def wait_block(slot):
        def body(r, c):
            row_copy(0, r, slot).wait()
            return c
        lax.fori_loop(0, tm, body, 0, unroll=8)

    @pl.when(i == 0)
    def _():
        def body(r, c):
            row_copy(0, r, 0).start()
            return c
        lax.fori_loop(0, tm, body, 0, unroll=8)

    wait_block(i % 2)
    nxt = jnp.minimum(i + 1, nb - 1)

    @pl.when(jnp.logical_not(used))
    def _():
        def body(r, c):
            row_copy(nxt, r, (i + 1) % 2).start()
            return c
        lax.fori_loop(0, tm, body, 0, unroll=8)
        o_ref[...] = jnp.zeros_like(o_ref)

    @pl.when(used)
    def _():
        for r in range(tm):
            row_copy(nxt, r, (i + 1) % 2).start()
        h = _rms(xbuf[i % 2], g_ref[...]).astype(BF16)
        acc = jnp.zeros(o_ref.shape, F32)
        for c in range(dff // tf):
            cols = slice(c * tf, (c + 1) * tf)
            a = _silu(_dot(h, wg_ref[:, cols])) * _dot(h, wu_ref[:, cols])
            acc = acc + _dot(a.astype(BF16), wd_ref[cols, :])
        o_ref[...] = acc

    @pl.when(i == nb - 1)
    def _():
        wait_block((i + 1) % 2)


def _experts(x, src_tok, g, n_used, block_expert, wg_bf16, wu_bf16, wd_bf16):
    r = src_tok.shape[0]
    d = x.shape[1]
    dff = wg_bf16.shape[2]
    tm = MOE_ROWS
    assert dff % MOE_FF_TILE == 0
    resident = pl.Buffered(1)
    wmap = lambda i, be, st, nu: (be[i], 0, 0)
    return pl.pallas_call(
        _expert_body,
        grid_spec=pltpu.PrefetchScalarGridSpec(
            num_scalar_prefetch=3,
            grid=(r // tm,),
            in_specs=[pl.BlockSpec(memory_space=pl.ANY),
                      pl.BlockSpec((1, d), lambda i, be, st, nu: (0, 0)),
                      pl.BlockSpec((None, d, dff), wmap, pipeline_mode=resident),
                      pl.BlockSpec((None, d, dff), wmap, pipeline_mode=resident),
                      pl.BlockSpec((None, dff, d), wmap, pipeline_mode=resident)],
            out_specs=pl.BlockSpec((tm, d), lambda i, be, st, nu: (i, 0)),
            scratch_shapes=[pltpu.VMEM((2, tm, d), F32), pltpu.SemaphoreType.DMA((2,))],
        ),
        out_shape=jax.ShapeDtypeStruct((r, d), F32),
        compiler_params=_params("arbitrary"),
        name="experts",
    )(block_expert, src_tok, n_used, x, g.reshape(1, d), wg_bf16, wu_bf16, wd_bf16)


def _combine_body(d0_ref, d1_ref, x_ref, gate_ref, g_ref, src_ref, *rest, n_prompt_tiles):
    i = pl.program_id(0)
    n = pl.num_programs(0)
    buf, sem = rest[-2:]

    def copies(block, r, slot):
        t = block * GATHER_ROWS + r
        return (pltpu.make_async_copy(src_ref.at[pl.ds(d0_ref[t], 1), :], buf.at[slot, 0, pl.ds(r, 1), :],
                                      sem.at[slot]),
                pltpu.make_async_copy(src_ref.at[pl.ds(d1_ref[t], 1), :], buf.at[slot, 1, pl.ds(r, 1), :],
                                      sem.at[slot]))

    def issue(block, slot):
        def body(r, c):
            for cp in copies(block, r, slot):
                cp.start()
            return c
        lax.fori_loop(0, GATHER_ROWS, body, 0, unroll=4)

    @pl.when(i == 0)
    def _():
        issue(0, 0)

    @pl.when(i + 1 < n)
    def _():
        issue(i + 1, (i + 1) % 2)

    def wait(r, c):
        for cp in copies(0, r, i % 2):
            cp.wait()
        return c

    lax.fori_loop(0, GATHER_ROWS, wait, 0, unroll=4)
    gate = gate_ref[...]
    y = x_ref[...] + gate[:, 0:1] * buf[i % 2, 0] + gate[:, 1:2] * buf[i % 2, 1]
    if n_prompt_tiles is None:
        rest[0][...] = y
    else:
        y = _rms(y, g_ref[...])

        @pl.when(i < n_prompt_tiles)
        def _():
            rest[0][...] = y

        @pl.when(i >= n_prompt_tiles)
        def _():
            rest[1][...] = y


def _combine(x, gate, expert_rows, d0, d1, final_g=None, n_prompt=None):
    nt, d = x.shape
    tm = GATHER_ROWS
    row = lambda i, a, b: (i, 0)
    if final_g is None:
        g = jnp.ones((d,), F32)
        npt = None
        out_specs = pl.BlockSpec((tm, d), row)
        out_shape = jax.ShapeDtypeStruct((nt, d), F32)
    else:
        g = final_g
        npt = n_prompt // tm
        out_specs = [pl.BlockSpec((tm, d), lambda i, a, b: (jnp.minimum(i, npt - 1), 0)),
                     pl.BlockSpec((tm, d), lambda i, a, b: (jnp.maximum(i - npt, 0), 0))]
        out_shape = [jax.ShapeDtypeStruct((n_prompt, d), F32), jax.ShapeDtypeStruct((nt - n_prompt, d), F32)]
    return pl.pallas_call(
        functools.partial(_combine_body, n_prompt_tiles=npt),
        grid_spec=pltpu.PrefetchScalarGridSpec(
            num_scalar_prefetch=2,
            grid=(nt // tm,),
            in_specs=[pl.BlockSpec((tm, d), row),
                      pl.BlockSpec((tm, LANES), row),
                      pl.BlockSpec((1, d), lambda i, a, b: (0, 0)),
                      pl.BlockSpec(memory_space=pl.ANY)],
            out_specs=out_specs,
            scratch_shapes=[pltpu.VMEM((2, 2, tm, d), F32), pltpu.SemaphoreType.DMA((2,))],
        ),
        out_shape=out_shape,
        compiler_params=_params("arbitrary"),
        name="combine",
    )(d0, d1, x, gate, g.reshape(1, d), expert_rows)


def _moe(x, g, w_router, wg_bf16, wu_bf16, wd_bf16, final_g=None, n_prompt=None):
    nt, d = x.shape
    n_exp = w_router.shape[1]
    tm = MOE_ROWS
    idx, gate = _route(x, g, w_router)
    e_flat = idx[:, :TOP_K].reshape(-1)
    n_assign = nt * TOP_K
    onehot = (e_flat[:, None] == jnp.arange(n_exp, dtype=jnp.int32)[None, :]).astype(jnp.int32)
    rank = jnp.cumsum(onehot, axis=0) - onehot
    pos = jnp.sum(rank * onehot, axis=1)
    counts = jnp.sum(onehot, axis=0)
    padded = (counts + tm - 1) // tm * tm
    pend = jnp.cumsum(padded)
    dest = ((pend - padded)[e_flat] + pos).astype(jnp.int32)
    n_rows = (n_assign // tm + n_exp) * tm
    tok = jnp.arange(n_assign, dtype=jnp.int32) // TOP_K
    src_tok = jnp.zeros((n_rows,), jnp.int32).at[dest].set(tok, unique_indices=True)
    block_expert = jnp.minimum(
        jnp.searchsorted(pend, jnp.arange(n_rows // tm, dtype=jnp.int32) * tm, side='right'), n_exp - 1
    ).astype(jnp.int32)
    n_used = (pend[n_exp - 1:] // tm).astype(jnp.int32)
    y_rows = _experts(x, src_tok, g, n_used, block_expert, wg_bf16, wu_bf16, wd_bf16)
    dest2 = dest.reshape(nt, TOP_K)
    return _combine(x, gate, y_rows, dest2[:, 0], dest2[:, 1], final_g, n_prompt)


def _norm_body(x_ref, g_ref, o_ref):
    o_ref[...] = _rms(x_ref[...], g_ref[...])


def _final_norm(x, g):
    nt, d = x.shape
    tm = ROW_TILE
    return pl.pallas_call(
        _norm_body,
        grid=(nt // tm,),
        in_specs=[pl.BlockSpec((tm, d), lambda i: (i, 0)), pl.BlockSpec((1, d), lambda i: (0, 0))],
        out_specs=pl.BlockSpec((tm, d), lambda i: (i, 0)),
        out_shape=jax.ShapeDtypeStruct((nt, d), F32),
        compiler_params=_params("parallel"),
        name="final_norm",
    )(x, g.reshape(1, d))


def kernel(x_prompt, x_sample, cache_k, cache_v, state_conv, page_table, rel_bias, norm_mix, w_in, w_gate,
           w_sgu, b_sgu, g_sgu, w_conv, w_branch, w_out, norm_ffn, w_ff_gate, w_ff_up, w_ff_down,
           w_router, w_exp_gate, w_exp_up, w_exp_down, norm_final):
    batch, seq, d = x_prompt.shape
    n_seq, t_new, _ = x_sample.shape
    depth = w_in.shape[0]
    n_p = batch * seq
    n_s = n_seq * t_new
    assert t_new == T_NEW and seq % MOBA_BLOCK == 0 and seq // MOBA_BLOCK <= LANES
    assert n_p % FFN_ROWS == 0 and n_s % ROW_TILE == 0 and (n_p + n_s) % FFN_ROWS == 0
    n_kb = seq // MOBA_BLOCK
    group_w = SGU_WIDTH // SGU_GROUPS

    x = jnp.concatenate([x_prompt.reshape(n_p, d), x_sample.reshape(n_s, d)], axis=0)
    bt = rel_bias.T.astype(F32)
    moba_tiles = _moba_bias_tiles(bt)
    far_bias = bt[:, REL_BUCKETS - 1]

    outs = {k: [] for k in ("kp", "vp", "ks", "vs", "cp", "cs", "sv")}
    for l in range(depth):
        q, k, v, uv, bch, kb, vb, km = _project(x, norm_mix[l], w_in[l].astype(BF16))

        kmean = jnp.pad(km[:n_p // MOBA_BLOCK, 0, :].reshape(batch, n_kb, QKV), ((0, 0), (0, LANES - n_kb), (0, 0)))
        o_a = _moba_prompt(q, kb, vb, kmean, moba_tiles, far_bias, batch, seq)
        o_c = _sb_prompt(q, kb, vb, batch, seq)
        ws = jnp.tril(w_sgu[l]).astype(BF16)
        bs = jnp.repeat(b_sgu[l].T, group_w, axis=1)
        o_b, o_d, z_p, _ = _mix(uv, bch, None, ws, bs, g_sgu[l], w_conv[l], seq // SGU_CHUNK, n_p)

        o_att = _attn_sample(q[n_p:], k[n_p:], v[n_p:], cache_k, cache_v, l, page_table, bt)
        per_chunk = SGU_CHUNK // T_NEW
        w4 = jnp.tril(w_sgu[l][:, :T_NEW, :T_NEW])
        ws_s = jnp.einsum('ab,gts->gatbs', jnp.eye(per_chunk, dtype=F32), w4)
        ws_s = ws_s.reshape(SGU_GROUPS, SGU_CHUNK, SGU_CHUNK).astype(BF16)
        bs_s = jnp.tile(jnp.repeat(b_sgu[l][:, :T_NEW].T, group_w, axis=1), (per_chunk, 1))
        prev = state_conv[l]
        zero = jnp.zeros((n_seq, 1, CONV_DIM), F32)
        p1 = jnp.concatenate([prev[:, 1:2], zero, zero, zero], axis=1).reshape(n_s, CONV_DIM)
        p2 = jnp.concatenate([prev[:, 0:1], prev[:, 1:2], zero, zero], axis=1).reshape(n_s, CONV_DIM)
        s_b, s_d, z_s, vn_s = _mix(uv[n_p:], bch[n_p:], (p1, p2), ws_s, bs_s, g_sgu[l], w_conv[l], None, n_s)

        prompt_br = (o_a, o_b, o_c, o_d)
        sample_br = (o_att[:, :QKV // 2], s_b, o_att[:, QKV // 2:], s_d)
        x = _merge(x, norm_mix[l], prompt_br, sample_br, w_gate[l].astype(BF16), w_branch[l].astype(BF16),
                   w_out[l].astype(BF16), n_p)

        i = l // 2
        if l % 2 == 0:
            x = _ffn(x, norm_ffn[l], w_ff_gate[i].astype(BF16), w_ff_up[i].astype(BF16), w_ff_down[i].astype(BF16))
        else:
            x = _moe(x, norm_ffn[l], w_router[i], w_exp_gate[i].astype(BF16), w_exp_up[i].astype(BF16),
                     w_exp_down[i].astype(BF16), norm_final if l == depth - 1 else None, n_p)

        outs["kp"].append(k[:n_p].reshape(batch, seq, N_HEADS, HEAD_DIM))
        outs["vp"].append(v[:n_p].reshape(batch, seq, N_HEADS, HEAD_DIM))
        outs["ks"].append(k[n_p:].reshape(n_seq, t_new, N_HEADS, HEAD_DIM))
        outs["vs"].append(v[n_p:].reshape(n_seq, t_new, N_HEADS, HEAD_DIM))
        outs["cp"].append(z_p.reshape(batch, seq, CONV_DIM)[:, seq - (CONV_WIDTH - 1):])
        outs["cs"].append(z_s.reshape(n_seq, t_new, CONV_DIM)[:, t_new - (CONV_WIDTH - 1):])
        outs["sv"].append(vn_s.reshape(n_seq, t_new, SGU_WIDTH))

    if depth % 2 == 0:
        y_p, y_s = x
    else:
        y = _final_norm(x, norm_final)
        y_p, y_s = y[:n_p], y[n_p:]
    return (y_p.reshape(batch, seq, d), y_s.reshape(n_seq, t_new, d),
            jnp.stack(outs["kp"]), jnp.stack(outs["vp"]), jnp.stack(outs["ks"]), jnp.stack(outs["vs"]),
            jnp.stack(outs["cp"]), jnp.stack(outs["cs"]), jnp.stack(outs["sv"]))
```

```python
import functools
import math

import numpy as np
import jax
import jax.numpy as jnp
from jax import lax
from jax.experimental import pallas as pl
from jax.experimental.pallas import tpu as pltpu

F32 = jnp.float32
BF16 = jnp.bfloat16

NORM_EPS = 1e-6
HEAD_DIM = 64
N_HEADS = 8
QKV = N_HEADS * HEAD_DIM
MOBA_BLOCK = 256
MOBA_TOPK = 3
Q_BLOCK = 128
SGU_CHUNK = 128
SGU_WIDTH = 256
SGU_GROUPS = 4
CONV_DIM = 256
CONV_WIDTH = 3
N_BRANCH = 4
BRANCH_DIM = 256
PAGE = 128
REL_BUCKETS = 32
REL_MAX_DIST = 128
TOP_K = 2
ATT_SCALE = HEAD_DIM ** -0.5
MASKED = -1e30
SB_DROP = -105.0
LANES = 128
ROW_TILE = 256
MOE_ROWS = 512
MOE_FF_TILE = 896
FFN_ROWS = 512
FFN_FF_TILE = 256
GATHER_ROWS = 256
V7X_VMEM_LIMIT = 56 * 1024 * 1024


def _params(*sem):
    return pltpu.CompilerParams(dimension_semantics=sem, vmem_limit_bytes=V7X_VMEM_LIMIT)


def _dot(a, b):
    return jnp.dot(a, b, preferred_element_type=F32)


def _dot_nt(a, b):
    return lax.dot_general(a, b, (((1,), (1,)), ((), ())), preferred_element_type=F32)


def _split_bf16(x):
    hi = x.astype(BF16)
    lo = (x - hi.astype(F32)).astype(BF16)
    return hi, lo


def _dot_nt_f32(a, b):
    ah, al = _split_bf16(a)
    bh, bl = _split_bf16(b)
    return _dot_nt(ah, bh) + _dot_nt(ah, bl) + _dot_nt(al, bh)


def _dot_f32(a, b):
    ah, al = _split_bf16(a)
    bh, bl = _split_bf16(b)
    return _dot(ah, bh) + _dot(ah, bl) + _dot(al, bh)


def _rms(x, g):
    return x * lax.rsqrt(jnp.mean(x * x, axis=-1, keepdims=True) + NORM_EPS) * g


def _gelu_tanh(x):
    return 0.5 * x * (1.0 + jnp.tanh(math.sqrt(2.0 / math.pi) * (x + 0.044715 * (x * x * x))))


def _sigmoid(x):
    return 1.0 / (1.0 + jnp.exp(-x))


def _softplus(z):
    return jnp.maximum(z, 0.0) + jnp.log1p(jnp.exp(-jnp.abs(z)))


def _suffix_sum(lk, u):
    n = lk.shape[0]
    hi, lo = _split_bf16(lk)
    r = _dot(jnp.concatenate([hi, lo], axis=0), u)
    return r[:n] + r[n:]


def _strict_upper(n):
    j = np.arange(n)[:, None]
    s = np.arange(n)[None, :]
    return jnp.asarray((j > s).astype(np.float32), dtype=BF16)


def _t5_bucket_table(max_dist):
    d = np.arange(max_dist + 1)
    max_exact = REL_BUCKETS // 2
    large = max_exact + (np.log(np.maximum(d, 1).astype(np.float32) / np.float32(max_exact))
                         / np.float32(math.log(REL_MAX_DIST / max_exact))
                         * np.float32(REL_BUCKETS - max_exact)).astype(np.int32)
    large = np.minimum(large, REL_BUCKETS - 1)
    return np.where(d < max_exact, d, large).astype(np.int32)


def _bias_by_dist(bt_rows, dist, valid):
    table = _t5_bucket_table(2 * REL_MAX_DIST)
    b = jnp.broadcast_to(bt_rows[:, 0:1], dist.shape)
    for k in range(1, REL_BUCKETS):
        first = int(np.argmax(table >= k))
        b = jnp.where(dist >= first, bt_rows[:, k:k + 1], b)
    return jnp.where(valid, b, MASKED)


def _proj_body(x_ref, g_ref, w_ref, q_ref, ks_ref, vs_ref, kt_ref, vt_ref, uv_ref, bch_ref, kb_ref, vb_ref, km_ref,
               *, n_prompt_tiles):
    i = pl.program_id(0)
    h = _rms(x_ref[...], g_ref[...]).astype(BF16)

    def mm(lo, hi):
        return _dot(h, w_ref[:, lo:hi])

    q_ref[...] = mm(0, QKV)
    k = mm(QKV, 2 * QKV)
    kb_ref[...] = k.astype(BF16)
    km_ref[...] = jnp.broadcast_to(jnp.mean(k, axis=0, keepdims=True), km_ref.shape)
    v = mm(2 * QKV, 3 * QKV)
    vb_ref[...] = v.astype(BF16)
    uv_ref[...] = mm(3 * QKV, 3 * QKV + 2 * SGU_WIDTH)
    bch_ref[...] = mm(3 * QKV + 2 * SGU_WIDTH, 3 * QKV + 2 * SGU_WIDTH + 3 * CONV_DIM)

    @pl.when(i < n_prompt_tiles)
    def _():
        kt_ref[...] = k.T
        vt_ref[...] = v.T

    @pl.when(i >= n_prompt_tiles)
    def _():
        ks_ref[...] = k
        vs_ref[...] = v


def _project(x, g, w_bf16, batch, seq):
    nt, d = x.shape
    in_dim = w_bf16.shape[1]
    tm = MOBA_BLOCK
    nb = nt // tm
    per_seq = seq // tm
    npt = batch * per_seq
    row = lambda i: (i, 0)
    srow = lambda i: (jnp.maximum(i - npt, 0), 0)
    tcol = lambda i: (jnp.minimum(i, npt - 1) // per_seq, 0, jnp.minimum(i, npt - 1) % per_seq)
    return pl.pallas_call(
        functools.partial(_proj_body, n_prompt_tiles=npt),
        grid=(nb,),
        in_specs=[pl.BlockSpec((tm, d), row),
                  pl.BlockSpec((1, d), lambda i: (0, 0)),
                  pl.BlockSpec((d, in_dim), lambda i: (0, 0))],
        out_specs=[pl.BlockSpec((tm, QKV), row), pl.BlockSpec((tm, QKV), srow), pl.BlockSpec((tm, QKV), srow),
                   pl.BlockSpec((None, QKV, tm), tcol), pl.BlockSpec((None, QKV, tm), tcol),
                   pl.BlockSpec((tm, 2 * SGU_WIDTH), row), pl.BlockSpec((tm, 3 * CONV_DIM), row),
                   pl.BlockSpec((tm, QKV), row), pl.BlockSpec((tm, QKV), row),
                   pl.BlockSpec((None, 8, QKV), lambda i: (i, 0, 0))],
        out_shape=[jax.ShapeDtypeStruct((nt, QKV), F32),
                   jax.ShapeDtypeStruct((nt - npt * tm, QKV), F32), jax.ShapeDtypeStruct((nt - npt * tm, QKV), F32),
                   jax.ShapeDtypeStruct((batch, QKV, seq), F32), jax.ShapeDtypeStruct((batch, QKV, seq), F32),
                   jax.ShapeDtypeStruct((nt, 2 * SGU_WIDTH), F32), jax.ShapeDtypeStruct((nt, 3 * CONV_DIM), F32),
                   jax.ShapeDtypeStruct((nt, QKV), BF16), jax.ShapeDtypeStruct((nt, QKV), BF16),
                   jax.ShapeDtypeStruct((nb, 8, QKV), F32)],
        compiler_params=_params("arbitrary"),
        name="proj",
    )(x, g.reshape(1, d), w_bf16)


def _top_blocks(gate, lane, lane_mask):
    lane = lane.astype(F32)
    g = jnp.where(lane_mask, gate, -jnp.inf)
    picks = []
    for _ in range(MOBA_TOPK):
        m = jnp.max(g, axis=-1, keepdims=True)
        idx = jnp.min(jnp.where(g == m, lane, 1e9), axis=-1, keepdims=True)
        picks.append((idx, m > -jnp.inf))
        g = jnp.where(lane == idx, -jnp.inf, g)
    return picks


def _is_picked(picks, j):
    j = jnp.asarray(j).astype(F32)
    c = (picks[0][0] == j) & picks[0][1]
    for idx, ok in picks[1:]:
        c = c | ((idx == j) & ok)
    return c


def _moba_prompt_body(fb_ref, q_ref, k_ref, v_ref, km_ref, bias_ref, spread_ref, o_ref, *, chunk, gate_rows):
    pr = pl.program_id(1)
    qi = pl.program_id(2)
    cur = qi
    rows = 2 * MOBA_BLOCK
    lane = lax.broadcasted_iota(jnp.int32, (MOBA_BLOCK, LANES), 1)
    q = q_ref[...] * ATT_SCALE
    qs = jnp.concatenate([jnp.where(lane < HEAD_DIM, q, 0.0), jnp.where(lane >= HEAD_DIM, q, 0.0)], axis=0)
    qsb = qs.astype(BF16)
    first_head = lax.broadcasted_iota(jnp.int32, (rows, 1), 0) < MOBA_BLOCK
    far_bias = jnp.where(first_head, fb_ref[2 * pr], fb_ref[2 * pr + 1])

    blk = lax.broadcasted_iota(jnp.int32, (gate_rows, rows), 0)
    blk_f = blk.astype(F32)
    g = jnp.where(blk < cur, _dot_nt_f32(km_ref[0:gate_rows, :], qs), -jnp.inf)
    picked = None
    for _ in range(MOBA_TOPK):
        best = jnp.max(g, axis=0, keepdims=True)
        hit = blk_f == jnp.min(jnp.where(g == best, blk_f, 1e9), axis=0, keepdims=True)
        chosen = hit & (best > -jnp.inf)
        picked = chosen if picked is None else picked | chosen
        g = jnp.where(hit, -jnp.inf, g)
    fill = jnp.full((LANES // 2 - gate_rows, rows), MASKED, F32)
    older = jnp.where(picked & (blk < cur - 1), 0.0, MASKED)
    prev = jnp.where(picked & (blk == cur - 1), 0.0, MASKED)
    add_rows = jnp.concatenate([older, fill, prev, fill], axis=0).T.astype(BF16)

    start = pl.multiple_of(cur * MOBA_BLOCK, MOBA_BLOCK)
    start_p = pl.multiple_of(jnp.maximum(cur - 1, 0) * MOBA_BLOCK, MOBA_BLOCK)
    q_and_mask = jnp.concatenate([qsb, add_rows], axis=1)
    spread_prev = lax.broadcasted_iota(jnp.int32, (MOBA_BLOCK, LANES), 1) == LANES // 2 + cur - 1
    keys_prev = jnp.concatenate([k_ref[pl.ds(start_p, MOBA_BLOCK), :], spread_prev.astype(BF16)], axis=1)
    s_own = _dot_nt(qsb, k_ref[pl.ds(start, MOBA_BLOCK), :]) + bias_ref[0]
    s_prev = _dot_nt(q_and_mask, keys_prev) + bias_ref[1]
    m = jnp.maximum(jnp.max(s_own, axis=-1, keepdims=True), jnp.max(s_prev, axis=-1, keepdims=True))
    p_own = jnp.exp(s_own - m)
    p_prev = jnp.exp(s_prev - m)
    l = jnp.sum(p_own, axis=-1, keepdims=True) + jnp.sum(p_prev, axis=-1, keepdims=True)
    acc = (_dot(p_own.astype(BF16), v_ref[pl.ds(start, MOBA_BLOCK), :])
           + _dot(p_prev.astype(BF16), v_ref[pl.ds(start_p, MOBA_BLOCK), :]))

    def chunk_step(c, carry):
        m, l, acc = carry
        start = pl.multiple_of(c * (chunk * MOBA_BLOCK), chunk * MOBA_BLOCK)
        keys = jnp.concatenate([k_ref[pl.ds(start, chunk * MOBA_BLOCK), :], spread_ref[c]], axis=1)
        s = _dot_nt(q_and_mask, keys) + far_bias
        m_new = jnp.maximum(m, jnp.max(s, axis=-1, keepdims=True))
        alpha = jnp.exp(m - m_new)
        p = jnp.exp(s - m_new)
        l = alpha * l + jnp.sum(p, axis=-1, keepdims=True)
        acc = alpha * acc + _dot(p.astype(BF16), v_ref[pl.ds(start, chunk * MOBA_BLOCK), :])
        return m_new, l, acc

    n_older = jnp.maximum(cur - 1, 0)
    m, l, acc = lax.fori_loop(0, (n_older + chunk - 1) // chunk, chunk_step, (m, l, acc))
    out = acc / l
    o_ref[...] = jnp.where(lane < HEAD_DIM, out[:MOBA_BLOCK], out[MOBA_BLOCK:])


def _moba_bias_tiles(bt):
    n_pairs = bt.shape[0] // 2
    shape = (n_pairs * 2 * MOBA_BLOCK, MOBA_BLOCK)
    i = lax.broadcasted_iota(jnp.int32, shape, 0) & (MOBA_BLOCK - 1)
    c = lax.broadcasted_iota(jnp.int32, shape, 1)
    bt_rows = jnp.repeat(bt, MOBA_BLOCK, axis=0)
    tiles = []
    for delta in (0, MOBA_BLOCK):
        dist = delta + i - c
        tiles.append(_bias_by_dist(bt_rows, dist, dist >= 0).reshape(n_pairs, 2 * MOBA_BLOCK, MOBA_BLOCK))
    return jnp.stack(tiles, axis=1)


def _moba_prompt(q, kb, vb, kmean, bias_tiles, far_bias, batch, seq):
    n_kb = seq // MOBA_BLOCK
    nq = n_kb
    chunk = next(c for c in (4, 2, 1) if n_kb % c == 0)
    gate_rows = -(-n_kb // 8) * 8
    assert gate_rows <= LANES // 2
    blk = np.arange(LANES)[None, None, :]
    key = np.arange(chunk * MOBA_BLOCK)[None, :, None]
    cid = np.arange(n_kb // chunk)[:, None, None]
    spread = jnp.asarray((blk == cid * chunk + key // MOBA_BLOCK).astype(np.float32), dtype=BF16)
    return pl.pallas_call(
        functools.partial(_moba_prompt_body, chunk=chunk, gate_rows=gate_rows),
        grid_spec=pltpu.PrefetchScalarGridSpec(
            num_scalar_prefetch=1,
            grid=(batch, 2, nq),
            in_specs=[pl.BlockSpec((MOBA_BLOCK, LANES), lambda b, p, i, fb: (b * nq + i, p)),
                      pl.BlockSpec((seq, LANES), lambda b, p, i, fb: (b, p)),
                      pl.BlockSpec((seq, LANES), lambda b, p, i, fb: (b, p)),
                      pl.BlockSpec((None, LANES, LANES), lambda b, p, i, fb: (b, 0, p)),
                      pl.BlockSpec((None, 2, 2 * MOBA_BLOCK, MOBA_BLOCK), lambda b, p, i, fb: (p, 0, 0, 0)),
                      pl.BlockSpec(spread.shape, lambda b, p, i, fb: (0, 0, 0))],
            out_specs=pl.BlockSpec((MOBA_BLOCK, LANES), lambda b, p, i, fb: (b * nq + i, p)),
        ),
        out_shape=jax.ShapeDtypeStruct((batch * seq, 2 * LANES), F32),
        compiler_params=_params("parallel", "parallel", "arbitrary"),
        name="moba_prompt",
    )(far_bias, q, kb, vb, kmean, bias_tiles, spread)


def _sb_prompt_body(q_ref, k_ref, v_ref, u_ref, o_ref):
    qi = pl.program_id(2)
    rows = 2 * Q_BLOCK
    lane = lax.broadcasted_iota(jnp.int32, (Q_BLOCK, LANES), 1)
    q = q_ref[...] * ATT_SCALE
    qsb = jnp.concatenate([jnp.where(lane < HEAD_DIM, q, 0.0), jnp.where(lane >= HEAD_DIM, q, 0.0)],
                          axis=0).astype(BF16)
    u = u_ref[...]
    col = lax.broadcasted_iota(jnp.int32, (rows, Q_BLOCK), 1)
    qrow = lax.broadcasted_iota(jnp.int32, (rows, Q_BLOCK), 0) & (Q_BLOCK - 1)

    def tile(j, keep, r_acc, acc):
        start = pl.multiple_of(jnp.maximum(j, 0) * Q_BLOCK, Q_BLOCK)
        z = _dot_nt(qsb, k_ref[pl.ds(start, Q_BLOCK), :])
        sp = _softplus(z)
        lk = -sp if keep is None else jnp.where(keep, -sp, 0.0)
        a = jnp.exp(z - sp + _suffix_sum(lk, u) + r_acc)
        if keep is not None:
            a = jnp.where(keep, a, 0.0)
        acc = acc + _dot(a.astype(BF16), v_ref[pl.ds(start, Q_BLOCK), :])
        return r_acc + jnp.sum(lk, axis=-1, keepdims=True), acc

    def exists(j):
        return jnp.broadcast_to(j >= 0, (rows, Q_BLOCK))

    r_acc, acc = tile(qi, col < qrow, jnp.zeros((rows, 1), F32), jnp.zeros((rows, LANES), F32))
    r_acc, acc = tile(qi - 1, exists(qi - 1), r_acc, acc)
    r_acc, acc = tile(qi - 2, exists(qi - 2), r_acc, acc)

    def more(state):
        it, go, _, _ = state
        return (qi - 3 - 2 * it >= 0) & (go > 0)

    def older(state):
        it, _, r_acc, acc = state
        j = qi - 3 - 2 * it
        r_acc, acc = tile(j, None, r_acc, acc)
        r_acc, acc = tile(j - 1, exists(j - 1), r_acc, acc)
        return it + 1, (jnp.max(r_acc) > SB_DROP).astype(jnp.int32), r_acc, acc

    go = (jnp.max(r_acc) > SB_DROP).astype(jnp.int32)
    _, _, _, acc = lax.while_loop(more, older, (jnp.int32(0), go, r_acc, acc))
    o_ref[...] = jnp.where(lane < HEAD_DIM, acc[:Q_BLOCK], acc[Q_BLOCK:])


def _sb_prompt(q, kb, vb, batch, seq):
    nq = seq // Q_BLOCK
    return pl.pallas_call(
        _sb_prompt_body,
        grid=(batch, 2, nq),
        in_specs=[pl.BlockSpec((Q_BLOCK, LANES), lambda b, p, i: (b * nq + i, 2 + p)),
                  pl.BlockSpec((seq, LANES), lambda b, p, i: (b, 2 + p)),
                  pl.BlockSpec((seq, LANES), lambda b, p, i: (b, 2 + p)),
                  pl.BlockSpec((Q_BLOCK, Q_BLOCK), lambda b, p, i: (0, 0))],
        out_specs=pl.BlockSpec((Q_BLOCK, LANES), lambda b, p, i: (b * nq + i, p)),
        out_shape=jax.ShapeDtypeStruct((batch * seq, 2 * LANES), F32),
        compiler_params=_params("parallel", "parallel", "arbitrary"),
        name="sb_prompt",
    )(q, kb, vb, _strict_upper(Q_BLOCK))


T_NEW = 4
SOFTMAX_ROWS = 4 * T_NEW


def _attn_sample_body(pt_ref, qx_ref, knew_ref, vnew_ref, bias_ref, bias_new_ref, u_ref, *rest, n_pages):
    del pt_ref
    k_refs = rest[:n_pages]
    v_refs = rest[n_pages:2 * n_pages]
    o_ref = rest[2 * n_pages]
    n_blocks = n_pages // 2
    sr = SOFTMAX_ROWS
    n_rows = N_HEADS * T_NEW
    past = n_pages * PAGE

    qx = qx_ref[...]
    qxb = qx.astype(BF16)

    blk_lane = lax.broadcasted_iota(jnp.int32, (QKV, LANES), 1)
    km = jnp.zeros((QKV, LANES), F32)
    parts = []
    for p in range(n_pages):
        kp = k_refs[p][...].reshape(QKV, PAGE)
        parts.append(_dot(qxb, kp.astype(BF16)))
        if p % 2 == 0:
            ksum = kp
        else:
            mean = jnp.sum(ksum + kp, axis=1, keepdims=True) * (1.0 / MOBA_BLOCK)
            km = jnp.where(blk_lane == p // 2, mean, km)
    s = jnp.concatenate(parts, axis=1)

    pad_rows = jnp.zeros((LANES - 8, QKV), F32)
    knew = jnp.concatenate([knew_ref[...], pad_rows], axis=0).astype(BF16)
    vnew = jnp.concatenate([vnew_ref[...], pad_rows], axis=0).astype(BF16)
    s_new = _dot_nt(qxb, knew)

    lane_s = lax.broadcasted_iota(jnp.int32, (sr, LANES), 1)
    picks = _top_blocks(_dot_f32(qx[:sr], km), lane_s, lane_s < n_blocks)
    key_block = (lax.broadcasted_iota(jnp.int32, (sr, past), 1) >> 8).astype(F32)
    la = jnp.where(_is_picked(picks, key_block), s[:sr] + bias_ref[...], MASKED)
    la_new = s_new[:sr] + bias_new_ref[...]
    m = jnp.maximum(jnp.max(la, axis=-1, keepdims=True), jnp.max(la_new, axis=-1, keepdims=True))
    pa = jnp.exp(la - m)
    pa_new = jnp.exp(la_new - m)
    denom = jnp.sum(pa, axis=-1, keepdims=True) + jnp.sum(pa_new, axis=-1, keepdims=True)

    u = u_ref[...]
    zb_new = s_new[sr:]
    keep_new = lane_s < (lax.broadcasted_iota(jnp.int32, (sr, LANES), 0) & (T_NEW - 1))
    sp_new = _softplus(zb_new)
    lk_new = jnp.where(keep_new, -sp_new, 0.0)
    a_new = jnp.where(keep_new, jnp.exp(zb_new - sp_new + _suffix_sum(lk_new, u)), 0.0)
    zb = s[sr:]
    sp = _softplus(zb)
    lk = -sp
    stacked = jnp.concatenate([lk[:, p * PAGE:(p + 1) * PAGE] for p in range(n_pages)], axis=0)
    within = _suffix_sum(stacked, u)
    after = jnp.sum(lk_new, axis=-1, keepdims=True)
    cols = [None] * n_pages
    for p in range(n_pages - 1, -1, -1):
        cols[p] = within[p * sr:(p + 1) * sr] + after
        after = after + jnp.sum(lk[:, p * PAGE:(p + 1) * PAGE], axis=-1, keepdims=True)
    a = jnp.exp(zb - sp + jnp.concatenate(cols, axis=1))

    w = jnp.concatenate([pa, a], axis=0).astype(BF16)
    w_new = jnp.concatenate([pa_new, a_new], axis=0).astype(BF16)
    o = _dot(w_new, vnew)
    for p in range(n_pages):
        o = o + _dot_nt(w[:, p * PAGE:(p + 1) * PAGE], v_refs[p][...].reshape(QKV, PAGE).astype(BF16))
    row_head = lax.broadcasted_iota(jnp.int32, (n_rows, QKV), 0) >> 2
    col_head = lax.broadcasted_iota(jnp.int32, (n_rows, QKV), 1) >> 6
    o = jnp.where(row_head == col_head, o, 0.0)
    out = o[:, 0:HEAD_DIM]
    for h in range(1, N_HEADS):
        out = out + o[:, h * HEAD_DIM:(h + 1) * HEAD_DIM]
    scale = jnp.concatenate([1.0 / denom, jnp.ones((n_rows - sr, 1), F32)], axis=0)
    o_ref[...] = out * scale


def _attn_sample(q_s, k_s, v_s, cache_k, cache_v, layer, page_table, bt):
    n_seq, n_pages = page_table.shape
    assert q_s.shape[0] == n_seq * T_NEW and n_pages % 2 == 0
    past = n_pages * PAGE
    n_rows = N_HEADS * T_NEW
    q4 = (q_s * ATT_SCALE).reshape(n_seq, T_NEW, N_HEADS, 1, HEAD_DIM).transpose(0, 2, 1, 3, 4)
    eye = jnp.eye(N_HEADS, dtype=F32).reshape(1, N_HEADS, 1, N_HEADS, 1)
    qx = (q4 * eye).reshape(n_seq, n_rows, QKV)
    pad = ((0, 0), (0, 8 - T_NEW), (0, 0))
    knew = jnp.pad(k_s.reshape(n_seq, T_NEW, QKV), pad)
    vnew = jnp.pad(v_s.reshape(n_seq, T_NEW, QKV), pad)
    ck = cache_k.transpose(0, 1, 3, 4, 2)
    cv = cache_v.transpose(0, 1, 3, 4, 2)

    bt_rows = jnp.repeat(bt[:4], T_NEW, axis=0)

    def key_bias(width, first_key_dist, new_keys):
        r = lax.broadcasted_iota(jnp.int32, (SOFTMAX_ROWS, width), 0)
        c = lax.broadcasted_iota(jnp.int32, (SOFTMAX_ROWS, width), 1)
        dist = first_key_dist + (r & (T_NEW - 1)) - c
        valid = (dist >= 0) & (c < T_NEW) if new_keys else dist >= 0
        return _bias_by_dist(bt_rows, dist, valid)

    bias_past = key_bias(past, past, False)
    bias_new = key_bias(LANES, 0, True)

    def page_spec(p):
        return pl.BlockSpec((None, None, N_HEADS, HEAD_DIM, PAGE),
                            lambda b, pt, p=p: (layer, pt[b * n_pages + p], 0, 0, 0))

    seq3 = lambda b, pt: (b, 0, 0)
    const2 = lambda b, pt: (0, 0)
    out = pl.pallas_call(
        functools.partial(_attn_sample_body, n_pages=n_pages),
        grid_spec=pltpu.PrefetchScalarGridSpec(
            num_scalar_prefetch=1,
            grid=(n_seq,),
            in_specs=[pl.BlockSpec((None, n_rows, QKV), seq3),
                      pl.BlockSpec((None, 8, QKV), seq3),
                      pl.BlockSpec((None, 8, QKV), seq3),
                      pl.BlockSpec((SOFTMAX_ROWS, past), const2),
                      pl.BlockSpec((SOFTMAX_ROWS, LANES), const2),
                      pl.BlockSpec((PAGE, PAGE), const2)]
                     + [page_spec(p) for p in range(n_pages)] * 2,
            out_specs=pl.BlockSpec((None, n_rows, HEAD_DIM), seq3),
        ),
        out_shape=jax.ShapeDtypeStruct((n_seq, n_rows, HEAD_DIM), F32),
        compiler_params=_params("parallel"),
        name="attn_sample",
    )(page_table.reshape(-1), qx, knew, vnew, bias_past, bias_new, _strict_upper(PAGE),
      *([ck] * n_pages), *([cv] * n_pages))
    return out.reshape(n_seq, N_HEADS, T_NEW, HEAD_DIM).transpose(0, 2, 1, 3).reshape(n_seq * T_NEW, QKV)


def _mix_body(*refs, chunks_per_seq):
    if chunks_per_seq is None:
        uv_ref, bch_ref, p1_ref, p2_ref, ws_ref, bs_ref, gs_ref, wc_ref, ob_ref, od_ref, z_ref, vn_ref = refs
    else:
        uv_ref, bch_ref, prev_ref, ws_ref, bs_ref, gs_ref, wc_ref, ob_ref, od_ref, z_ref, vn_ref = refs
    uv = uv_ref[...]
    x = _gelu_tanh(uv[:, SGU_WIDTH:])
    xc = x - jnp.mean(x, axis=-1, keepdims=True)
    vn = xc * lax.rsqrt(jnp.mean(xc * xc, axis=-1, keepdims=True) + NORM_EPS) * gs_ref[...]
    vn_ref[...] = vn
    vnb = vn.astype(BF16)
    group = lax.broadcasted_iota(jnp.int32, (SGU_CHUNK, SGU_WIDTH), 1) >> 6
    s = jnp.zeros((SGU_CHUNK, SGU_WIDTH), F32)
    for g in range(SGU_GROUPS):
        s = jnp.where(group == g, _dot(ws_ref[g], vnb), s)
    ob_ref[...] = _gelu_tanh(uv[:, :SGU_WIDTH]) * (s + bs_ref[...])

    bch = bch_ref[...]
    z = bch[:, CONV_DIM:2 * CONV_DIM] * bch[:, 2 * CONV_DIM:]
    z_ref[...] = z
    row = lax.broadcasted_iota(jnp.int32, (SGU_CHUNK, CONV_DIM), 0)
    r1 = pltpu.roll(z, 1, 0)
    r2 = pltpu.roll(z, 2, 0)
    if chunks_per_seq is None:
        rr = row & (T_NEW - 1)
        z1 = jnp.where(rr >= 1, r1, p1_ref[...])
        z2 = jnp.where(rr >= 2, r2, p2_ref[...])
    else:
        pb = prev_ref[...]
        zp = pb[:, CONV_DIM:2 * CONV_DIM] * pb[:, 2 * CONV_DIM:]
        zp = jnp.where(pl.program_id(0) % chunks_per_seq == 0, 0.0, zp)
        z1 = jnp.where(row >= 1, r1, zp[7:8, :])
        z2 = jnp.where(row >= 2, r2, jnp.where(row == 1, zp[7:8, :], zp[6:7, :]))
    wc = wc_ref[...]
    od_ref[...] = bch[:, :CONV_DIM] * (wc[0:1, :] * z2 + wc[1:2, :] * z1 + wc[2:3, :] * z)


def _mix(uv, bch, history, ws_bf16, bs, gs, wc, chunks_per_seq, n_rows):
    nc = n_rows // SGU_CHUNK
    row = lambda i: (i, 0)
    const = lambda i: (0, 0)
    if chunks_per_seq is None:
        hist_specs = [pl.BlockSpec((SGU_CHUNK, CONV_DIM), row)] * 2
        hist = list(history)
    else:
        sub = SGU_CHUNK // 8
        hist_specs = [pl.BlockSpec((8, 3 * CONV_DIM), lambda i: (jnp.maximum(i * sub - 1, 0), 0))]
        hist = [bch]
    return pl.pallas_call(
        functools.partial(_mix_body, chunks_per_seq=chunks_per_seq),
        grid=(nc,),
        in_specs=[pl.BlockSpec((SGU_CHUNK, 2 * SGU_WIDTH), row), pl.BlockSpec((SGU_CHUNK, 3 * CONV_DIM), row)]
                 + hist_specs
                 + [pl.BlockSpec((SGU_GROUPS, SGU_CHUNK, SGU_CHUNK), lambda i: (0, 0, 0)),
                    pl.BlockSpec((SGU_CHUNK, SGU_WIDTH), const),
                    pl.BlockSpec((1, SGU_WIDTH), const),
                    pl.BlockSpec((8, CONV_DIM), const)],
        out_specs=[pl.BlockSpec((SGU_CHUNK, SGU_WIDTH), row)] * 4,
        out_shape=[jax.ShapeDtypeStruct((n_rows, SGU_WIDTH), F32)] * 4,
        compiler_params=_params("parallel"),
        name="mix_sample" if chunks_per_seq is None else "mix_prompt",
    )(uv, bch, *hist, ws_bf16, bs, gs.reshape(1, SGU_WIDTH), jnp.pad(wc, ((0, 8 - CONV_WIDTH), (0, 0))))


def _merge_body(x_ref, g_ref, pa_ref, pb_ref, pc_ref, pd_ref, sa_ref, sb_ref, sc_ref, sd_ref,
                wg_ref, wb_ref, wo_ref, o_ref, *, n_prompt_tiles):
    x = x_ref[...]
    d = x.shape[1]
    h = _rms(x, g_ref[...]).astype(BF16)
    is_prompt = pl.program_id(0) < n_prompt_tiles
    merged = jnp.zeros(x.shape, F32)
    for n, (p_ref, s_ref) in enumerate(((pa_ref, sa_ref), (pb_ref, sb_ref), (pc_ref, sc_ref), (pd_ref, sd_ref))):
        br = jnp.where(is_prompt, p_ref[...], s_ref[...]).astype(BF16)
        up = _dot(br, wb_ref[n])
        gate = _sigmoid(_dot(h, wg_ref[:, n * d:(n + 1) * d]))
        merged = merged + gate * up
    o_ref[...] = x + _dot(merged.astype(BF16), wo_ref[...])


def _merge(x, g, prompt_br, sample_br, wg_bf16, wb_bf16, wo_bf16, n_prompt):
    nt, d = x.shape
    tm = ROW_TILE
    npt = n_prompt // tm
    row = lambda i: (i, 0)
    p_spec = pl.BlockSpec((tm, BRANCH_DIM), lambda i: (jnp.minimum(i, npt - 1), 0))
    s_spec = pl.BlockSpec((tm, BRANCH_DIM), lambda i: (jnp.maximum(i - npt, 0), 0))
    return pl.pallas_call(
        functools.partial(_merge_body, n_prompt_tiles=npt),
        grid=(nt // tm,),
        in_specs=[pl.BlockSpec((tm, d), row), pl.BlockSpec((1, d), lambda i: (0, 0))]
                 + [p_spec] * N_BRANCH + [s_spec] * N_BRANCH
                 + [pl.BlockSpec((d, N_BRANCH * d), lambda i: (0, 0)),
                    pl.BlockSpec((N_BRANCH, BRANCH_DIM, d), lambda i: (0, 0, 0)),
                    pl.BlockSpec((d, d), lambda i: (0, 0))],
        out_specs=pl.BlockSpec((tm, d), row),
        out_shape=jax.ShapeDtypeStruct((nt, d), F32),
        compiler_params=_params("parallel"),
        name="merge",
    )(x, g.reshape(1, d), *prompt_br, *sample_br, wg_bf16, wb_bf16, wo_bf16)


def _silu(x):
    return x * _sigmoid(x)


def _ffn_body(x_ref, g_ref, wg_ref, wu_ref, wd_ref, o_ref):
    x = x_ref[...]
    h = _rms(x, g_ref[...]).astype(BF16)
    dff = wg_ref.shape[1]
    acc = x
    for c in range(dff // FFN_FF_TILE):
        cols = slice(c * FFN_FF_TILE, (c + 1) * FFN_FF_TILE)
        a = _silu(_dot(h, wg_ref[:, cols])) * _dot(h, wu_ref[:, cols])
        acc = acc + _dot(a.astype(BF16), wd_ref[cols, :])
    o_ref[...] = acc


def _ffn(x, g, wg_bf16, wu_bf16, wd_bf16):
    nt, d = x.shape
    dff = wg_bf16.shape[1]
    tm = FFN_ROWS
    assert dff % FFN_FF_TILE == 0
    resident = pl.Buffered(1)
    return pl.pallas_call(
        _ffn_body,
        grid=(nt // tm,),
        in_specs=[pl.BlockSpec((tm, d), lambda i: (i, 0)), pl.BlockSpec((1, d), lambda i: (0, 0)),
                  pl.BlockSpec((d, dff), lambda i: (0, 0), pipeline_mode=resident),
                  pl.BlockSpec((d, dff), lambda i: (0, 0), pipeline_mode=resident),
                  pl.BlockSpec((dff, d), lambda i: (0, 0), pipeline_mode=resident)],
        out_specs=pl.BlockSpec((tm, d), lambda i: (i, 0)),
        out_shape=jax.ShapeDtypeStruct((nt, d), F32),
        compiler_params=_params("parallel"),
        name="ffn",
    )(x, g.reshape(1, d), wg_bf16, wu_bf16, wd_bf16)


def _route_body(x_ref, g_ref, wr_ref, idx_ref, gate_ref, *, n_exp):
    h = _rms(x_ref[...], g_ref[...])
    logits = _dot_nt_f32(h, wr_ref[...])
    lane_i = lax.broadcasted_iota(jnp.int32, logits.shape, 1)
    lane = lane_i.astype(F32)
    lg = jnp.where(lane_i < n_exp, logits, -jnp.inf)
    m1 = jnp.max(lg, axis=-1, keepdims=True)
    i1 = jnp.min(jnp.where(lg == m1, lane, float(LANES)), axis=-1, keepdims=True)
    lg2 = jnp.where(lane == i1, -jnp.inf, lg)
    m2 = jnp.max(lg2, axis=-1, keepdims=True)
    i2 = jnp.min(jnp.where(lg2 == m2, lane, float(LANES)), axis=-1, keepdims=True)
    e2 = jnp.exp(m2 - m1)
    idx_ref[...] = jnp.where(lane_i == 0, i1, jnp.where(lane_i == 1, i2, 0.0)).astype(jnp.int32)
    gate_ref[...] = jnp.where(lane_i == 0, 1.0 / (1.0 + e2), jnp.where(lane_i == 1, e2 / (1.0 + e2), 0.0))


def _route(x, g, w_router):
    nt, d = x.shape
    n_exp = w_router.shape[1]
    assert n_exp <= LANES
    wr = jnp.pad(w_router.T, ((0, LANES - n_exp), (0, 0)))
    tm = ROW_TILE
    row = lambda i: (i, 0)
    return pl.pallas_call(
        functools.partial(_route_body, n_exp=n_exp),
        grid=(nt // tm,),
        in_specs=[pl.BlockSpec((tm, d), row), pl.BlockSpec((1, d), lambda i: (0, 0)),
                  pl.BlockSpec((LANES, d), lambda i: (0, 0))],
        out_specs=[pl.BlockSpec((tm, LANES), row)] * 2,
        out_shape=[jax.ShapeDtypeStruct((nt, LANES), jnp.int32), jax.ShapeDtypeStruct((nt, LANES), F32)],
        compiler_params=_params("parallel"),
        name="route",
    )(x, g.reshape(1, d), wr)


def _expert_body(be_ref, src_ref, nu_ref, x_hbm, g_ref, wg_ref, wu_ref, wd_ref, o_ref, xbuf, sem):
    del be_ref
    i = pl.program_id(0)
    nb = pl.num_programs(0)
    used = i < nu_ref[0]
    tm = xbuf.shape[1]
    dff = wg_ref.shape[1]
    tf = MOE_FF_TILE

    def row_copy(block, r, slot):
        tok = src_ref[block * tm + r]
        return pltpu.make_async_copy(x_hbm.at[pl.ds(tok, 1), :], xbuf.at[slot, pl.ds(r, 1), :], sem.at[slot])

    def wait_block(slot):
        def body(r, c):
            row_copy(0, r, slot).wait()
            return c
        lax.fori_loop(0, tm, body, 0, unroll=8)

    @pl.when(i == 0)
    def _():
        def body(r, c):
            row_copy(0, r, 0).start()
            return c
        lax.fori_loop(0, tm, body, 0, unroll=8)

    wait_block(i % 2)
    nxt = jnp.minimum(i + 1, nb - 1)

    @pl.when(jnp.logical_not(used))
    def _():
        def body(r, c):
            row_copy(nxt, r, (i + 1) % 2).start()
            return c
        lax.fori_loop(0, tm, body, 0, unroll=8)
        o_ref[...] = jnp.zeros_like(o_ref)

    @pl.when(used)
    def _():
        for r in range(tm):
            row_copy(nxt, r, (i + 1) % 2).start()
        h = _rms(xbuf[i % 2], g_ref[...]).astype(BF16)
        acc = jnp.zeros(o_ref.shape, F32)
        for c in range(dff // tf):
            cols = slice(c * tf, (c + 1) * tf)
            a = _silu(_dot(h, wg_ref[:, cols])) * _dot(h, wu_ref[:, cols])
            acc = acc + _dot(a.astype(BF16), wd_ref[cols, :])
        o_ref[...] = acc

    @pl.when(i == nb - 1)
    def _():
        wait_block((i + 1) % 2)


def _experts(x, src_tok, g, n_used, block_expert, wg_bf16, wu_bf16, wd_bf16):
    r = src_tok.shape[0]
    d = x.shape[1]
    dff = wg_bf16.shape[2]
    tm = MOE_ROWS
    assert dff % MOE_FF_TILE == 0
    resident = pl.Buffered(1)
    wmap = lambda i, be, st, nu: (be[i], 0, 0)
    return pl.pallas_call(
        _expert_body,
        grid_spec=pltpu.PrefetchScalarGridSpec(
            num_scalar_prefetch=3,
            grid=(r // tm,),
            in_specs=[pl.BlockSpec(memory_space=pl.ANY),
                      pl.BlockSpec((1, d), lambda i, be, st, nu: (0, 0)),
                      pl.BlockSpec((None, d, dff), wmap, pipeline_mode=resident),
                      pl.BlockSpec((None, d, dff), wmap, pipeline_mode=resident),
                      pl.BlockSpec((None, dff, d), wmap, pipeline_mode=resident)],
            out_specs=pl.BlockSpec((tm, d), lambda i, be, st, nu: (i, 0)),
            scratch_shapes=[pltpu.VMEM((2, tm, d), F32), pltpu.SemaphoreType.DMA((2,))],
        ),
        out_shape=jax.ShapeDtypeStruct((r, d), F32),
        compiler_params=_params("arbitrary"),
        name="experts",
    )(block_expert, src_tok, n_used, x, g.reshape(1, d), wg_bf16, wu_bf16, wd_bf16)


def _combine_body(d0_ref, d1_ref, x_ref, gate_ref, g_ref, src_ref, *rest, n_prompt_tiles):
    i = pl.program_id(0)
    n = pl.num_programs(0)
    buf, sem = rest[-2:]

    def copies(block, r, slot):
        t = block * GATHER_ROWS + r
        return (pltpu.make_async_copy(src_ref.at[pl.ds(d0_ref[t], 1), :], buf.at[slot, 0, pl.ds(r, 1), :],
                                      sem.at[slot]),
                pltpu.make_async_copy(src_ref.at[pl.ds(d1_ref[t], 1), :], buf.at[slot, 1, pl.ds(r, 1), :],
                                      sem.at[slot]))

    def issue(block, slot):
        def body(r, c):
            for cp in copies(block, r, slot):
                cp.start()
            return c
        lax.fori_loop(0, GATHER_ROWS, body, 0, unroll=4)

    @pl.when(i == 0)
    def _():
        issue(0, 0)

    @pl.when(i + 1 < n)
    def _():
        issue(i + 1, (i + 1) % 2)

    def wait(r, c):
        for cp in copies(0, r, i % 2):
            cp.wait()
        return c

    lax.fori_loop(0, GATHER_ROWS, wait, 0, unroll=4)
    gate = gate_ref[...]
    y = x_ref[...] + gate[:, 0:1] * buf[i % 2, 0] + gate[:, 1:2] * buf[i % 2, 1]
    if n_prompt_tiles is None:
        rest[0][...] = y
    else:
        y = _rms(y, g_ref[...])

        @pl.when(i < n_prompt_tiles)
        def _():
            rest[0][...] = y

        @pl.when(i >= n_prompt_tiles)
        def _():
            rest[1][...] = y


def _combine(x, gate, expert_rows, d0, d1, final_g=None, n_prompt=None):
    nt, d = x.shape
    tm = GATHER_ROWS
    row = lambda i, a, b: (i, 0)
    if final_g is None:
        g = jnp.ones((d,), F32)
        npt = None
        out_specs = pl.BlockSpec((tm, d), row)
        out_shape = jax.ShapeDtypeStruct((nt, d), F32)
    else:
        g = final_g
        npt = n_prompt // tm
        out_specs = [pl.BlockSpec((tm, d), lambda i, a, b: (jnp.minimum(i, npt - 1), 0)),
                     pl.BlockSpec((tm, d), lambda i, a, b: (jnp.maximum(i - npt, 0), 0))]
        out_shape = [jax.ShapeDtypeStruct((n_prompt, d), F32), jax.ShapeDtypeStruct((nt - n_prompt, d), F32)]
    return pl.pallas_call(
        functools.partial(_combine_body, n_prompt_tiles=npt),
        grid_spec=pltpu.PrefetchScalarGridSpec(
            num_scalar_prefetch=2,
            grid=(nt // tm,),
            in_specs=[pl.BlockSpec((tm, d), row),
                      pl.BlockSpec((tm, LANES), row),
                      pl.BlockSpec((1, d), lambda i, a, b: (0, 0)),
                      pl.BlockSpec(memory_space=pl.ANY)],
            out_specs=out_specs,
            scratch_shapes=[pltpu.VMEM((2, 2, tm, d), F32), pltpu.SemaphoreType.DMA((2,))],
        ),
        out_shape=out_shape,
        compiler_params=_params("arbitrary"),
        name="combine",
    )(d0, d1, x, gate, g.reshape(1, d), expert_rows)


def _moe(x, g, w_router, wg_bf16, wu_bf16, wd_bf16, final_g=None, n_prompt=None):
    nt, d = x.shape
    n_exp = w_router.shape[1]
    tm = MOE_ROWS
    idx, gate = _route(x, g, w_router)
    e_flat = idx[:, :TOP_K].reshape(-1)
    n_assign = nt * TOP_K
    onehot = (e_flat[:, None] == jnp.arange(n_exp, dtype=jnp.int32)[None, :]).astype(jnp.int32)
    rank = jnp.cumsum(onehot, axis=0) - onehot
    pos = jnp.sum(rank * onehot, axis=1)
    counts = jnp.sum(onehot, axis=0)
    padded = (counts + tm - 1) // tm * tm
    pend = jnp.cumsum(padded)
    dest = ((pend - padded)[e_flat] + pos).astype(jnp.int32)
    n_rows = (n_assign // tm + n_exp) * tm
    tok = jnp.arange(n_assign, dtype=jnp.int32) // TOP_K
    src_tok = jnp.zeros((n_rows,), jnp.int32).at[dest].set(tok, unique_indices=True)
    block_expert = jnp.minimum(
        jnp.searchsorted(pend, jnp.arange(n_rows // tm, dtype=jnp.int32) * tm, side='right'), n_exp - 1
    ).astype(jnp.int32)
    n_used = (pend[n_exp - 1:] // tm).astype(jnp.int32)
    y_rows = _experts(x, src_tok, g, n_used, block_expert, wg_bf16, wu_bf16, wd_bf16)
    dest2 = dest.reshape(nt, TOP_K)
    return _combine(x, gate, y_rows, dest2[:, 0], dest2[:, 1], final_g, n_prompt)


def _norm_body(x_ref, g_ref, o_ref):
    o_ref[...] = _rms(x_ref[...], g_ref[...])


def _final_norm(x, g):
    nt, d = x.shape
    tm = ROW_TILE
    return pl.pallas_call(
        _norm_body,
        grid=(nt // tm,),
        in_specs=[pl.BlockSpec((tm, d), lambda i: (i, 0)), pl.BlockSpec((1, d), lambda i: (0, 0))],
        out_specs=pl.BlockSpec((tm, d), lambda i: (i, 0)),
        out_shape=jax.ShapeDtypeStruct((nt, d), F32),
        compiler_params=_params("parallel"),
        name="final_norm",
    )(x, g.reshape(1, d))


def kernel(x_prompt, x_sample, cache_k, cache_v, state_conv, page_table, rel_bias, norm_mix, w_in, w_gate,
           w_sgu, b_sgu, g_sgu, w_conv, w_branch, w_out, norm_ffn, w_ff_gate, w_ff_up, w_ff_down,
           w_router, w_exp_gate, w_exp_up, w_exp_down, norm_final):
    batch, seq, d = x_prompt.shape
    n_seq, t_new, _ = x_sample.shape
    depth = w_in.shape[0]
    n_p = batch * seq
    n_s = n_seq * t_new
    assert t_new == T_NEW and seq % MOBA_BLOCK == 0 and seq // MOBA_BLOCK <= LANES
    assert n_p % FFN_ROWS == 0 and n_s % ROW_TILE == 0 and (n_p + n_s) % FFN_ROWS == 0
    n_kb = seq // MOBA_BLOCK
    group_w = SGU_WIDTH // SGU_GROUPS

    x = jnp.concatenate([x_prompt.reshape(n_p, d), x_sample.reshape(n_s, d)], axis=0)
    bt = rel_bias.T.astype(F32)
    moba_tiles = _moba_bias_tiles(bt)
    far_bias = bt[:, REL_BUCKETS - 1]

    outs = {k: [] for k in ("kp", "vp", "ks", "vs", "cp", "cs", "sv")}
    for l in range(depth):
        q, k_s, v_s, kt, vt, uv, bch, kb, vb, km = _project(x, norm_mix[l], w_in[l].astype(BF16), batch, seq)

        kmean = jnp.pad(km[:n_p // MOBA_BLOCK, 0, :].reshape(batch, n_kb, QKV), ((0, 0), (0, LANES - n_kb), (0, 0)))
        o_a = _moba_prompt(q, kb, vb, kmean, moba_tiles, far_bias, batch, seq)
        o_c = _sb_prompt(q, kb, vb, batch, seq)
        ws = jnp.tril(w_sgu[l]).astype(BF16)
        bs = jnp.repeat(b_sgu[l].T, group_w, axis=1)
        o_b, o_d, z_p, _ = _mix(uv, bch, None, ws, bs, g_sgu[l], w_conv[l], seq // SGU_CHUNK, n_p)

        o_att = _attn_sample(q[n_p:], k_s, v_s, cache_k, cache_v, l, page_table, bt)
        per_chunk = SGU_CHUNK // T_NEW
        w4 = jnp.tril(w_sgu[l][:, :T_NEW, :T_NEW])
        ws_s = jnp.einsum('ab,gts->gatbs', jnp.eye(per_chunk, dtype=F32), w4)
        ws_s = ws_s.reshape(SGU_GROUPS, SGU_CHUNK, SGU_CHUNK).astype(BF16)
        bs_s = jnp.tile(jnp.repeat(b_sgu[l][:, :T_NEW].T, group_w, axis=1), (per_chunk, 1))
        prev = state_conv[l]
        zero = jnp.zeros((n_seq, 1, CONV_DIM), F32)
        p1 = jnp.concatenate([prev[:, 1:2], zero, zero, zero], axis=1).reshape(n_s, CONV_DIM)
        p2 = jnp.concatenate([prev[:, 0:1], prev[:, 1:2], zero, zero], axis=1).reshape(n_s, CONV_DIM)
        s_b, s_d, z_s, vn_s = _mix(uv[n_p:], bch[n_p:], (p1, p2), ws_s, bs_s, g_sgu[l], w_conv[l], None, n_s)

        prompt_br = (o_a, o_b, o_c, o_d)
        sample_br = (o_att[:, :QKV // 2], s_b, o_att[:, QKV // 2:], s_d)
        x = _merge(x, norm_mix[l], prompt_br, sample_br, w_gate[l].astype(BF16), w_branch[l].astype(BF16),
                   w_out[l].astype(BF16), n_p)

        i = l // 2
        if l % 2 == 0:
            x = _ffn(x, norm_ffn[l], w_ff_gate[i].astype(BF16), w_ff_up[i].astype(BF16), w_ff_down[i].astype(BF16))
        else:
            x = _moe(x, norm_ffn[l], w_router[i], w_exp_gate[i].astype(BF16), w_exp_up[i].astype(BF16),
                     w_exp_down[i].astype(BF16), norm_final if l == depth - 1 else None, n_p)

        outs["kp"].append(kt.reshape(batch, N_HEADS, HEAD_DIM, seq).transpose(0, 3, 1, 2))
        outs["vp"].append(vt.reshape(batch, N_HEADS, HEAD_DIM, seq).transpose(0, 3, 1, 2))
        outs["ks"].append(k_s.reshape(n_seq, t_new, N_HEADS, HEAD_DIM))
        outs["vs"].append(v_s.reshape(n_seq, t_new, N_HEADS, HEAD_DIM))
        outs["cp"].append(z_p.reshape(batch, seq, CONV_DIM)[:, seq - (CONV_WIDTH - 1):])
        outs["cs"].append(z_s.reshape(n_seq, t_new, CONV_DIM)[:, t_new - (CONV_WIDTH - 1):])
        outs["sv"].append(vn_s.reshape(n_seq, t_new, SGU_WIDTH))

    if depth % 2 == 0:
        y_p, y_s = x
    else:
        y = _final_norm(x, norm_final)
        y_p, y_s = y[:n_p], y[n_p:]
    return (y_p.reshape(batch, seq, d), y_s.reshape(n_seq, t_new, d),
            jnp.stack(outs["kp"]), jnp.stack(outs["vp"]), jnp.stack(outs["ks"]), jnp.stack(outs["vs"]),
            jnp.stack(outs["cp"]), jnp.stack(outs["cs"]), jnp.stack(outs["sv"]))
```

```python
import functools
import math

import numpy as np
import jax
import jax.numpy as jnp
from jax import lax
from jax.experimental import pallas as pl
from jax.experimental.pallas import tpu as pltpu

F32 = jnp.float32
BF16 = jnp.bfloat16

NORM_EPS = 1e-6
HEAD_DIM = 64
N_HEADS = 8
QKV = N_HEADS * HEAD_DIM
MOBA_BLOCK = 256
MOBA_TOPK = 3
Q_BLOCK = 128
SGU_CHUNK = 128
SGU_WIDTH = 256
SGU_GROUPS = 4
CONV_DIM = 256
CONV_WIDTH = 3
N_BRANCH = 4
BRANCH_DIM = 256
PAGE = 128
REL_BUCKETS = 32
REL_MAX_DIST = 128
TOP_K = 2
ATT_SCALE = HEAD_DIM ** -0.5
MASKED = -1e30
SB_DROP = -105.0
LANES = 128
ROW_TILE = 256
MOE_ROWS = 512
MOE_FF_TILE = 896
FFN_ROWS = 512
FFN_FF_TILE = 256
GATHER_ROWS = 256
V7X_VMEM_LIMIT = 56 * 1024 * 1024


def _params(*sem):
    return pltpu.CompilerParams(dimension_semantics=sem, vmem_limit_bytes=V7X_VMEM_LIMIT)


def _dot(a, b):
    return jnp.dot(a, b, preferred_element_type=F32)


def _dot_nt(a, b):
    return lax.dot_general(a, b, (((1,), (1,)), ((), ())), preferred_element_type=F32)


def _split_bf16(x):
    hi = x.astype(BF16)
    lo = (x - hi.astype(F32)).astype(BF16)
    return hi, lo


def _dot_nt_f32(a, b):
    ah, al = _split_bf16(a)
    bh, bl = _split_bf16(b)
    return _dot_nt(ah, bh) + _dot_nt(ah, bl) + _dot_nt(al, bh)


def _dot_f32(a, b):
    ah, al = _split_bf16(a)
    bh, bl = _split_bf16(b)
    return _dot(ah, bh) + _dot(ah, bl) + _dot(al, bh)


def _rms(x, g):
    return x * lax.rsqrt(jnp.mean(x * x, axis=-1, keepdims=True) + NORM_EPS) * g


def _gelu_tanh(x):
    return 0.5 * x * (1.0 + jnp.tanh(math.sqrt(2.0 / math.pi) * (x + 0.044715 * (x * x * x))))


def _sigmoid(x):
    return 1.0 / (1.0 + jnp.exp(-x))


def _softplus(z):
    return jnp.maximum(z, 0.0) + jnp.log1p(jnp.exp(-jnp.abs(z)))


def _suffix_sum(lk, u):
    n = lk.shape[0]
    hi, lo = _split_bf16(lk)
    r = _dot(jnp.concatenate([hi, lo], axis=0), u)
    return r[:n] + r[n:]


def _strict_upper(n):
    j = np.arange(n)[:, None]
    s = np.arange(n)[None, :]
    return jnp.asarray((j > s).astype(np.float32), dtype=BF16)


def _t5_bucket_table(max_dist):
    d = np.arange(max_dist + 1)
    max_exact = REL_BUCKETS // 2
    large = max_exact + (np.log(np.maximum(d, 1).astype(np.float32) / np.float32(max_exact))
                         / np.float32(math.log(REL_MAX_DIST / max_exact))
                         * np.float32(REL_BUCKETS - max_exact)).astype(np.int32)
    large = np.minimum(large, REL_BUCKETS - 1)
    return np.where(d < max_exact, d, large).astype(np.int32)


def _bias_by_dist(bt_rows, dist, valid):
    table = _t5_bucket_table(2 * REL_MAX_DIST)
    b = jnp.broadcast_to(bt_rows[:, 0:1], dist.shape)
    for k in range(1, REL_BUCKETS):
        first = int(np.argmax(table >= k))
        b = jnp.where(dist >= first, bt_rows[:, k:k + 1], b)
    return jnp.where(valid, b, MASKED)


def _proj_body(x_ref, g_ref, w_ref, q_ref, ks_ref, vs_ref, kt_ref, vt_ref, uv_ref, bch_ref, kb_ref, vb_ref, km_ref,
               *, n_prompt_tiles):
    i = pl.program_id(0)
    h = _rms(x_ref[...], g_ref[...]).astype(BF16)

    def mm(lo, hi):
        return _dot(h, w_ref[:, lo:hi])

    q_ref[...] = mm(0, QKV)
    k = mm(QKV, 2 * QKV)
    kb_ref[...] = k.astype(BF16)
    km_ref[...] = jnp.broadcast_to(jnp.mean(k, axis=0, keepdims=True), km_ref.shape)
    v = mm(2 * QKV, 3 * QKV)
    vb_ref[...] = v.astype(BF16)
    uv_ref[...] = mm(3 * QKV, 3 * QKV + 2 * SGU_WIDTH)
    bch_ref[...] = mm(3 * QKV + 2 * SGU_WIDTH, 3 * QKV + 2 * SGU_WIDTH + 3 * CONV_DIM)

    @pl.when(i < n_prompt_tiles)
    def _():
        kt_ref[...] = k.T
        vt_ref[...] = v.T

    @pl.when(i >= n_prompt_tiles)
    def _():
        ks_ref[...] = k
        vs_ref[...] = v


def _project(x, g, w_bf16, batch, seq):
    nt, d = x.shape
    in_dim = w_bf16.shape[1]
    tm = MOBA_BLOCK
    nb = nt // tm
    per_seq = seq // tm
    npt = batch * per_seq
    row = lambda i: (i, 0)
    srow = lambda i: (jnp.maximum(i - npt, 0), 0)
    tcol = lambda i: (jnp.minimum(i, npt - 1) // per_seq, 0, jnp.minimum(i, npt - 1) % per_seq)
    return pl.pallas_call(
        functools.partial(_proj_body, n_prompt_tiles=npt),
        grid=(nb,),
        in_specs=[pl.BlockSpec((tm, d), row),
                  pl.BlockSpec((1, d), lambda i: (0, 0)),
                  pl.BlockSpec((d, in_dim), lambda i: (0, 0))],
        out_specs=[pl.BlockSpec((tm, QKV), row), pl.BlockSpec((tm, QKV), srow), pl.BlockSpec((tm, QKV), srow),
                   pl.BlockSpec((None, QKV, tm), tcol), pl.BlockSpec((None, QKV, tm), tcol),
                   pl.BlockSpec((tm, 2 * SGU_WIDTH), row), pl.BlockSpec((tm, 3 * CONV_DIM), row),
                   pl.BlockSpec((tm, QKV), row), pl.BlockSpec((tm, QKV), row),
                   pl.BlockSpec((None, 8, QKV), lambda i: (i, 0, 0))],
        out_shape=[jax.ShapeDtypeStruct((nt, QKV), F32),
                   jax.ShapeDtypeStruct((nt - npt * tm, QKV), F32), jax.ShapeDtypeStruct((nt - npt * tm, QKV), F32),
                   jax.ShapeDtypeStruct((batch, QKV, seq), F32), jax.ShapeDtypeStruct((batch, QKV, seq), F32),
                   jax.ShapeDtypeStruct((nt, 2 * SGU_WIDTH), F32), jax.ShapeDtypeStruct((nt, 3 * CONV_DIM), F32),
                   jax.ShapeDtypeStruct((nt, QKV), BF16), jax.ShapeDtypeStruct((nt, QKV), BF16),
                   jax.ShapeDtypeStruct((nb, 8, QKV), F32)],
        compiler_params=_params("arbitrary"),
        name="proj",
    )(x, g.reshape(1, d), w_bf16)


def _top_blocks(gate, lane, lane_mask):
    lane = lane.astype(F32)
    g = jnp.where(lane_mask, gate, -jnp.inf)
    picks = []
    for _ in range(MOBA_TOPK):
        m = jnp.max(g, axis=-1, keepdims=True)
        idx = jnp.min(jnp.where(g == m, lane, 1e9), axis=-1, keepdims=True)
        picks.append((idx, m > -jnp.inf))
        g = jnp.where(lane == idx, -jnp.inf, g)
    return picks


def _is_picked(picks, j):
    j = jnp.asarray(j).astype(F32)
    c = (picks[0][0] == j) & picks[0][1]
    for idx, ok in picks[1:]:
        c = c | ((idx == j) & ok)
    return c


def _moba_prompt_body(fb_ref, q_ref, k_ref, v_ref, km_ref, bias_ref, spread_ref, o_ref, *, chunk, gate_rows):
    pr = pl.program_id(1)
    qi = pl.program_id(2)
    cur = qi
    rows = 2 * MOBA_BLOCK
    lane = lax.broadcasted_iota(jnp.int32, (MOBA_BLOCK, LANES), 1)
    q = q_ref[...] * ATT_SCALE
    qs = jnp.concatenate([jnp.where(lane < HEAD_DIM, q, 0.0), jnp.where(lane >= HEAD_DIM, q, 0.0)], axis=0)
    qsb = qs.astype(BF16)
    first_head = lax.broadcasted_iota(jnp.int32, (rows, 1), 0) < MOBA_BLOCK
    far_bias = jnp.where(first_head, fb_ref[2 * pr], fb_ref[2 * pr + 1])

    blk = lax.broadcasted_iota(jnp.int32, (gate_rows, rows), 0)
    blk_f = blk.astype(F32)
    g = jnp.where(blk < cur, _dot_nt_f32(km_ref[0:gate_rows, :], qs), -jnp.inf)
    picked = None
    for _ in range(MOBA_TOPK):
        best = jnp.max(g, axis=0, keepdims=True)
        hit = blk_f == jnp.min(jnp.where(g == best, blk_f, 1e9), axis=0, keepdims=True)
        chosen = hit & (best > -jnp.inf)
        picked = chosen if picked is None else picked | chosen
        g = jnp.where(hit, -jnp.inf, g)
    fill = jnp.full((LANES // 2 - gate_rows, rows), MASKED, F32)
    older = jnp.where(picked & (blk < cur - 1), 0.0, MASKED)
    prev = jnp.where(picked & (blk == cur - 1), 0.0, MASKED)
    add_rows = jnp.concatenate([older, fill, prev, fill], axis=0).T.astype(BF16)

    start = pl.multiple_of(cur * MOBA_BLOCK, MOBA_BLOCK)
    start_p = pl.multiple_of(jnp.maximum(cur - 1, 0) * MOBA_BLOCK, MOBA_BLOCK)
    q_and_mask = jnp.concatenate([qsb, add_rows], axis=1)
    spread_prev = lax.broadcasted_iota(jnp.int32, (MOBA_BLOCK, LANES), 1) == LANES // 2 + cur - 1
    keys_prev = jnp.concatenate([k_ref[pl.ds(start_p, MOBA_BLOCK), :], spread_prev.astype(BF16)], axis=1)
    s_own = _dot_nt(qsb, k_ref[pl.ds(start, MOBA_BLOCK), :]) + bias_ref[0]
    s_prev = _dot_nt(q_and_mask, keys_prev) + bias_ref[1]
    m = jnp.maximum(jnp.max(s_own, axis=-1, keepdims=True), jnp.max(s_prev, axis=-1, keepdims=True))
    p_own = jnp.exp(s_own - m)
    p_prev = jnp.exp(s_prev - m)
    l = jnp.sum(p_own, axis=-1, keepdims=True) + jnp.sum(p_prev, axis=-1, keepdims=True)
    acc = (_dot(p_own.astype(BF16), v_ref[pl.ds(start, MOBA_BLOCK), :])
           + _dot(p_prev.astype(BF16), v_ref[pl.ds(start_p, MOBA_BLOCK), :]))

    def chunk_step(c, carry):
        m, l, acc = carry
        start = pl.multiple_of(c * (chunk * MOBA_BLOCK), chunk * MOBA_BLOCK)
        keys = jnp.concatenate([k_ref[pl.ds(start, chunk * MOBA_BLOCK), :], spread_ref[c]], axis=1)
        s = _dot_nt(q_and_mask, keys) + far_bias
        m_new = jnp.maximum(m, jnp.max(s, axis=-1, keepdims=True))
        alpha = jnp.exp(m - m_new)
        p = jnp.exp(s - m_new)
        l = alpha * l + jnp.sum(p, axis=-1, keepdims=True)
        acc = alpha * acc + _dot(p.astype(BF16), v_ref[pl.ds(start, chunk * MOBA_BLOCK), :])
        return m_new, l, acc

    n_older = jnp.maximum(cur - 1, 0)
    m, l, acc = lax.fori_loop(0, (n_older + chunk - 1) // chunk, chunk_step, (m, l, acc))
    out = acc / l
    o_ref[...] = jnp.where(lane < HEAD_DIM, out[:MOBA_BLOCK], out[MOBA_BLOCK:])


def _moba_bias_tiles(bt):
    n_pairs = bt.shape[0] // 2
    shape = (n_pairs * 2 * MOBA_BLOCK, MOBA_BLOCK)
    i = lax.broadcasted_iota(jnp.int32, shape, 0) & (MOBA_BLOCK - 1)
    c = lax.broadcasted_iota(jnp.int32, shape, 1)
    bt_rows = jnp.repeat(bt, MOBA_BLOCK, axis=0)
    tiles = []
    for delta in (0, MOBA_BLOCK):
        dist = delta + i - c
        tiles.append(_bias_by_dist(bt_rows, dist, dist >= 0).reshape(n_pairs, 2 * MOBA_BLOCK, MOBA_BLOCK))
    return jnp.stack(tiles, axis=1)


def _moba_prompt(q, kb, vb, kmean, bias_tiles, far_bias, batch, seq):
    n_kb = seq // MOBA_BLOCK
    nq = n_kb
    chunk = next(c for c in (4, 2, 1) if n_kb % c == 0)
    gate_rows = -(-n_kb // 8) * 8
    assert gate_rows <= LANES // 2
    blk = np.arange(LANES)[None, None, :]
    key = np.arange(chunk * MOBA_BLOCK)[None, :, None]
    cid = np.arange(n_kb // chunk)[:, None, None]
    spread = jnp.asarray((blk == cid * chunk + key // MOBA_BLOCK).astype(np.float32), dtype=BF16)
    return pl.pallas_call(
        functools.partial(_moba_prompt_body, chunk=chunk, gate_rows=gate_rows),
        grid_spec=pltpu.PrefetchScalarGridSpec(
            num_scalar_prefetch=1,
            grid=(batch, 2, nq),
            in_specs=[pl.BlockSpec((MOBA_BLOCK, LANES), lambda b, p, i, fb: (b * nq + i, p)),
                      pl.BlockSpec((seq, LANES), lambda b, p, i, fb: (b, p)),
                      pl.BlockSpec((seq, LANES), lambda b, p, i, fb: (b, p)),
                      pl.BlockSpec((None, LANES, LANES), lambda b, p, i, fb: (b, 0, p)),
                      pl.BlockSpec((None, 2, 2 * MOBA_BLOCK, MOBA_BLOCK), lambda b, p, i, fb: (p, 0, 0, 0)),
                      pl.BlockSpec(spread.shape, lambda b, p, i, fb: (0, 0, 0))],
            out_specs=pl.BlockSpec((MOBA_BLOCK, LANES), lambda b, p, i, fb: (b * nq + i, p)),
        ),
        out_shape=jax.ShapeDtypeStruct((batch * seq, 2 * LANES), F32),
        compiler_params=_params("parallel", "parallel", "arbitrary"),
        name="moba_prompt",
    )(far_bias, q, kb, vb, kmean, bias_tiles, spread)


def _sb_prompt_body(q_ref, k_ref, v_ref, u_ref, o_ref):
    qi = pl.program_id(2)
    rows = 2 * Q_BLOCK
    lane = lax.broadcasted_iota(jnp.int32, (Q_BLOCK, LANES), 1)
    q = q_ref[...] * ATT_SCALE
    qsb = jnp.concatenate([jnp.where(lane < HEAD_DIM, q, 0.0), jnp.where(lane >= HEAD_DIM, q, 0.0)],
                          axis=0).astype(BF16)
    u = u_ref[...]
    col = lax.broadcasted_iota(jnp.int32, (rows, Q_BLOCK), 1)
    qrow = lax.broadcasted_iota(jnp.int32, (rows, Q_BLOCK), 0) & (Q_BLOCK - 1)

    def tile(j, keep, r_acc, acc):
        start = pl.multiple_of(jnp.maximum(j, 0) * Q_BLOCK, Q_BLOCK)
        z = _dot_nt(qsb, k_ref[pl.ds(start, Q_BLOCK), :])
        sp = _softplus(z)
        lk = -sp if keep is None else jnp.where(keep, -sp, 0.0)
        a = jnp.exp(z - sp + _suffix_sum(lk, u) + r_acc)
        if keep is not None:
            a = jnp.where(keep, a, 0.0)
        acc = acc + _dot(a.astype(BF16), v_ref[pl.ds(start, Q_BLOCK), :])
        return r_acc + jnp.sum(lk, axis=-1, keepdims=True), acc

    def exists(j):
        return jnp.broadcast_to(j >= 0, (rows, Q_BLOCK))

    r_acc, acc = tile(qi, col < qrow, jnp.zeros((rows, 1), F32), jnp.zeros((rows, LANES), F32))
    r_acc, acc = tile(qi - 1, exists(qi - 1), r_acc, acc)
    r_acc, acc = tile(qi - 2, exists(qi - 2), r_acc, acc)

    def more(state):
        it, go, _, _ = state
        return (qi - 3 - 2 * it >= 0) & (go > 0)

    def older(state):
        it, _, r_acc, acc = state
        j = qi - 3 - 2 * it
        r_acc, acc = tile(j, None, r_acc, acc)
        r_acc, acc = tile(j - 1, exists(j - 1), r_acc, acc)
        return it + 1, (jnp.max(r_acc) > SB_DROP).astype(jnp.int32), r_acc, acc

    go = (jnp.max(r_acc) > SB_DROP).astype(jnp.int32)
    _, _, _, acc = lax.while_loop(more, older, (jnp.int32(0), go, r_acc, acc))
    o_ref[...] = jnp.where(lane < HEAD_DIM, acc[:Q_BLOCK], acc[Q_BLOCK:])


def _sb_prompt(q, kb, vb, batch, seq):
    nq = seq // Q_BLOCK
    return pl.pallas_call(
        _sb_prompt_body,
        grid=(batch, 2, nq),
        in_specs=[pl.BlockSpec((Q_BLOCK, LANES), lambda b, p, i: (b * nq + i, 2 + p)),
                  pl.BlockSpec((seq, LANES), lambda b, p, i: (b, 2 + p)),
                  pl.BlockSpec((seq, LANES), lambda b, p, i: (b, 2 + p)),
                  pl.BlockSpec((Q_BLOCK, Q_BLOCK), lambda b, p, i: (0, 0))],
        out_specs=pl.BlockSpec((Q_BLOCK, LANES), lambda b, p, i: (b * nq + i, p)),
        out_shape=jax.ShapeDtypeStruct((batch * seq, 2 * LANES), F32),
        compiler_params=_params("parallel", "parallel", "arbitrary"),
        name="sb_prompt",
    )(q, kb, vb, _strict_upper(Q_BLOCK))


T_NEW = 4
SOFTMAX_ROWS = 4 * T_NEW


SB_STATIC_PAGES = 3
HALF = QKV // 2


def _own_head_fold(o):
    row_head = lax.broadcasted_iota(jnp.int32, o.shape, 0) >> 2
    col_head = lax.broadcasted_iota(jnp.int32, o.shape, 1) >> 6
    o = jnp.where(row_head == col_head, o, 0.0)
    out = o[:, 0:HEAD_DIM]
    for h in range(1, HALF // HEAD_DIM):
        out = out + o[:, h * HEAD_DIM:(h + 1) * HEAD_DIM]
    return out


def _attn_sample_body(pt_ref, qa_ref, qb_ref, knew_ref, vnew_ref, bias_ref, bias_new_ref, u_ref, ck_ref, cv_ref,
                      *rest, n_pages, n_static, layer):
    ka_refs = rest[:n_pages]
    va_refs = rest[n_pages:2 * n_pages]
    kb_refs = rest[2 * n_pages:2 * n_pages + n_static]
    vb_refs = rest[2 * n_pages + n_static:2 * n_pages + 2 * n_static]
    o_ref, kbuf, vbuf, sem = rest[2 * n_pages + 2 * n_static:]
    seq = pl.program_id(0)
    n_blocks = n_pages // 2
    sr = SOFTMAX_ROWS
    past = n_pages * PAGE

    qa = qa_ref[...]
    qab = qa.astype(BF16)
    qbb = qb_ref[...].astype(BF16)

    blk_lane = lax.broadcasted_iota(jnp.int32, (HALF, LANES), 1)
    km = jnp.zeros((HALF, LANES), F32)
    parts = []
    for p in range(n_pages):
        kp = ka_refs[p][...].reshape(HALF, PAGE)
        parts.append(_dot(qab, kp.astype(BF16)))
        if p % 2 == 0:
            ksum = kp
        else:
            mean = jnp.sum(ksum + kp, axis=1, keepdims=True) * (1.0 / MOBA_BLOCK)
            km = jnp.where(blk_lane == p // 2, mean, km)
    s = jnp.concatenate(parts, axis=1)

    pad_rows = jnp.zeros((LANES - 8, QKV), F32)
    knew = jnp.concatenate([knew_ref[...], pad_rows], axis=0).astype(BF16)
    vnew = jnp.concatenate([vnew_ref[...], pad_rows], axis=0).astype(BF16)

    lane_s = lax.broadcasted_iota(jnp.int32, (sr, LANES), 1)
    picks = _top_blocks(_dot_f32(qa, km), lane_s, lane_s < n_blocks)
    key_block = (lax.broadcasted_iota(jnp.int32, (sr, past), 1) >> 8).astype(F32)
    la = jnp.where(_is_picked(picks, key_block), s + bias_ref[...], MASKED)
    la_new = _dot_nt(qab, knew[:, :HALF]) + bias_new_ref[...]
    m = jnp.maximum(jnp.max(la, axis=-1, keepdims=True), jnp.max(la_new, axis=-1, keepdims=True))
    pa = jnp.exp(la - m)
    pa_new = jnp.exp(la_new - m)
    denom = jnp.sum(pa, axis=-1, keepdims=True) + jnp.sum(pa_new, axis=-1, keepdims=True)
    pa = pa.astype(BF16)
    o_a = _dot(pa_new.astype(BF16), vnew[:, :HALF])
    for p in range(n_pages):
        o_a = o_a + _dot_nt(pa[:, p * PAGE:(p + 1) * PAGE], va_refs[p][...].reshape(HALF, PAGE).astype(BF16))

    u = u_ref[...]

    def sticks(zb, r_acc, keep=None):
        sp = _softplus(zb)
        lk = -sp if keep is None else jnp.where(keep, -sp, 0.0)
        a = jnp.exp(zb - sp + _suffix_sum(lk, u) + r_acc)
        if keep is not None:
            a = jnp.where(keep, a, 0.0)
        return a.astype(BF16), r_acc + jnp.sum(lk, axis=-1, keepdims=True)

    keep_new = lane_s < (lax.broadcasted_iota(jnp.int32, (sr, LANES), 0) & (T_NEW - 1))
    zs = [_dot_nt(qbb, knew[:, HALF:])] + [_dot(qbb, kb_refs[t][...].reshape(HALF, PAGE).astype(BF16))
                                          for t in range(n_static)]
    sps = [_softplus(z) for z in zs]
    lks = [jnp.where(keep_new, -sps[0], 0.0)] + [-sp for sp in sps[1:]]
    within = _suffix_sum(jnp.concatenate(lks, axis=0), u)
    r_acc = jnp.zeros((sr, 1), F32)
    o_b = None
    for t, (z, sp, lk) in enumerate(zip(zs, sps, lks)):
        a = jnp.exp(z - sp + within[t * sr:(t + 1) * sr] + r_acc)
        r_acc = r_acc + jnp.sum(lk, axis=-1, keepdims=True)
        if t == 0:
            o_b = _dot(jnp.where(keep_new, a, 0.0).astype(BF16), vnew[:, HALF:])
        else:
            o_b = o_b + _dot_nt(a.astype(BF16), vb_refs[t - 1][...].reshape(HALF, PAGE).astype(BF16))

    def more(state):
        p, go, _, _ = state
        return (p >= 0) & (go > 0)

    def older(state):
        p, _, r_acc, o_b = state
        page = pt_ref[seq * n_pages + p]
        copies = (pltpu.make_async_copy(ck_ref.at[layer, page, pl.ds(N_HEADS // 2, N_HEADS // 2)], kbuf, sem.at[0]),
                  pltpu.make_async_copy(cv_ref.at[layer, page, pl.ds(N_HEADS // 2, N_HEADS // 2)], vbuf, sem.at[1]))
        for cp in copies:
            cp.start()
        for cp in copies:
            cp.wait()
        a, r_acc = sticks(_dot(qbb, kbuf[...].reshape(HALF, PAGE).astype(BF16)), r_acc)
        o_b = o_b + _dot_nt(a, vbuf[...].reshape(HALF, PAGE).astype(BF16))
        return p - 1, (jnp.max(r_acc) > SB_DROP).astype(jnp.int32), r_acc, o_b

    go = (jnp.max(r_acc) > SB_DROP).astype(jnp.int32)
    _, _, _, o_b = lax.while_loop(more, older, (jnp.int32(n_pages - n_static - 1), go, r_acc, o_b))

    o_ref[...] = jnp.concatenate([_own_head_fold(o_a) * (1.0 / denom), _own_head_fold(o_b)], axis=0)


def _attn_sample(q_s, k_s, v_s, cache_k, cache_v, layer, page_table, bt):
    n_seq, n_pages = page_table.shape
    assert q_s.shape[0] == n_seq * T_NEW and n_pages % 2 == 0
    past = n_pages * PAGE
    n_rows = N_HEADS * T_NEW
    n_static = min(SB_STATIC_PAGES, n_pages)
    half_heads = N_HEADS // 2
    q4 = (q_s * ATT_SCALE).reshape(n_seq, T_NEW, 2, half_heads, 1, HEAD_DIM).transpose(2, 0, 3, 1, 4, 5)
    eye = jnp.eye(half_heads, dtype=F32).reshape(1, 1, half_heads, 1, half_heads, 1)
    qx = (q4 * eye).reshape(2, n_seq, SOFTMAX_ROWS, HALF)
    pad = ((0, 0), (0, 8 - T_NEW), (0, 0))
    knew = jnp.pad(k_s.reshape(n_seq, T_NEW, QKV), pad)
    vnew = jnp.pad(v_s.reshape(n_seq, T_NEW, QKV), pad)
    ck = cache_k.transpose(0, 1, 3, 4, 2)
    cv = cache_v.transpose(0, 1, 3, 4, 2)

    bt_rows = jnp.repeat(bt[:4], T_NEW, axis=0)

    def key_bias(width, first_key_dist, new_keys):
        r = lax.broadcasted_iota(jnp.int32, (SOFTMAX_ROWS, width), 0)
        c = lax.broadcasted_iota(jnp.int32, (SOFTMAX_ROWS, width), 1)
        dist = first_key_dist + (r & (T_NEW - 1)) - c
        valid = (dist >= 0) & (c < T_NEW) if new_keys else dist >= 0
        return _bias_by_dist(bt_rows, dist, valid)

    bias_past = key_bias(past, past, False)
    bias_new = key_bias(LANES, 0, True)

    def page_spec(p, head_group):
        return pl.BlockSpec((None, None, half_heads, HEAD_DIM, PAGE),
                            lambda b, pt, p=p: (layer, pt[b * n_pages + p], head_group, 0, 0))

    softmax_pages = [page_spec(p, 0) for p in range(n_pages)]
    stick_pages = [page_spec(n_pages - 1 - t, 1) for t in range(n_static)]
    seq3 = lambda b, pt: (b, 0, 0)
    const2 = lambda b, pt: (0, 0)
    out = pl.pallas_call(
        functools.partial(_attn_sample_body, n_pages=n_pages, n_static=n_static, layer=layer),
        grid_spec=pltpu.PrefetchScalarGridSpec(
            num_scalar_prefetch=1,
            grid=(n_seq,),
            in_specs=[pl.BlockSpec((None, SOFTMAX_ROWS, HALF), seq3),
                      pl.BlockSpec((None, SOFTMAX_ROWS, HALF), seq3),
                      pl.BlockSpec((None, 8, QKV), seq3),
                      pl.BlockSpec((None, 8, QKV), seq3),
                      pl.BlockSpec((SOFTMAX_ROWS, past), const2),
                      pl.BlockSpec((SOFTMAX_ROWS, LANES), const2),
                      pl.BlockSpec((PAGE, PAGE), const2),
                      pl.BlockSpec(memory_space=pl.ANY),
                      pl.BlockSpec(memory_space=pl.ANY)]
                     + softmax_pages * 2 + stick_pages * 2,
            out_specs=pl.BlockSpec((None, n_rows, HEAD_DIM), seq3),
            scratch_shapes=[pltpu.VMEM((half_heads, HEAD_DIM, PAGE), F32),
                            pltpu.VMEM((half_heads, HEAD_DIM, PAGE), F32),
                            pltpu.SemaphoreType.DMA((2,))],
        ),
        out_shape=jax.ShapeDtypeStruct((n_seq, n_rows, HEAD_DIM), F32),
        compiler_params=_params("arbitrary"),
        name="attn_sample",
    )(page_table.reshape(-1), qx[0], qx[1], knew, vnew, bias_past, bias_new, _strict_upper(PAGE), ck, cv,
      *([ck] * n_pages), *([cv] * n_pages), *([ck] * n_static), *([cv] * n_static))
    return out.reshape(n_seq, N_HEADS, T_NEW, HEAD_DIM).transpose(0, 2, 1, 3).reshape(n_seq * T_NEW, QKV)


def _mix_body(*refs, chunks_per_seq):
    if chunks_per_seq is None:
        uv_ref, bch_ref, p1_ref, p2_ref, ws_ref, bs_ref, gs_ref, wc_ref, ob_ref, od_ref, z_ref, vn_ref = refs
    else:
        uv_ref, bch_ref, prev_ref, ws_ref, bs_ref, gs_ref, wc_ref, ob_ref, od_ref, z_ref, vn_ref = refs
    uv = uv_ref[...]
    x = _gelu_tanh(uv[:, SGU_WIDTH:])
    xc = x - jnp.mean(x, axis=-1, keepdims=True)
    vn = xc * lax.rsqrt(jnp.mean(xc * xc, axis=-1, keepdims=True) + NORM_EPS) * gs_ref[...]
    vn_ref[...] = vn
    vnb = vn.astype(BF16)
    group = lax.broadcasted_iota(jnp.int32, (SGU_CHUNK, SGU_WIDTH), 1) >> 6
    s = jnp.zeros((SGU_CHUNK, SGU_WIDTH), F32)
    for g in range(SGU_GROUPS):
        s = jnp.where(group == g, _dot(ws_ref[g], vnb), s)
    ob_ref[...] = _gelu_tanh(uv[:, :SGU_WIDTH]) * (s + bs_ref[...])

    bch = bch_ref[...]
    z = bch[:, CONV_DIM:2 * CONV_DIM] * bch[:, 2 * CONV_DIM:]
    z_ref[...] = z
    row = lax.broadcasted_iota(jnp.int32, (SGU_CHUNK, CONV_DIM), 0)
    r1 = pltpu.roll(z, 1, 0)
    r2 = pltpu.roll(z, 2, 0)
    if chunks_per_seq is None:
        rr = row & (T_NEW - 1)
        z1 = jnp.where(rr >= 1, r1, p1_ref[...])
        z2 = jnp.where(rr >= 2, r2, p2_ref[...])
    else:
        pb = prev_ref[...]
        zp = pb[:, CONV_DIM:2 * CONV_DIM] * pb[:, 2 * CONV_DIM:]
        zp = jnp.where(pl.program_id(0) % chunks_per_seq == 0, 0.0, zp)
        z1 = jnp.where(row >= 1, r1, zp[7:8, :])
        z2 = jnp.where(row >= 2, r2, jnp.where(row == 1, zp[7:8, :], zp[6:7, :]))
    wc = wc_ref[...]
    od_ref[...] = bch[:, :CONV_DIM] * (wc[0:1, :] * z2 + wc[1:2, :] * z1 + wc[2:3, :] * z)


def _mix(uv, bch, history, ws_bf16, bs, gs, wc, chunks_per_seq, n_rows):
    nc = n_rows // SGU_CHUNK
    row = lambda i: (i, 0)
    const = lambda i: (0, 0)
    if chunks_per_seq is None:
        hist_specs = [pl.BlockSpec((SGU_CHUNK, CONV_DIM), row)] * 2
        hist = list(history)
    else:
        sub = SGU_CHUNK // 8
        hist_specs = [pl.BlockSpec((8, 3 * CONV_DIM), lambda i: (jnp.maximum(i * sub - 1, 0), 0))]
        hist = [bch]
    return pl.pallas_call(
        functools.partial(_mix_body, chunks_per_seq=chunks_per_seq),
        grid=(nc,),
        in_specs=[pl.BlockSpec((SGU_CHUNK, 2 * SGU_WIDTH), row), pl.BlockSpec((SGU_CHUNK, 3 * CONV_DIM), row)]
                 + hist_specs
                 + [pl.BlockSpec((SGU_GROUPS, SGU_CHUNK, SGU_CHUNK), lambda i: (0, 0, 0)),
                    pl.BlockSpec((SGU_CHUNK, SGU_WIDTH), const),
                    pl.BlockSpec((1, SGU_WIDTH), const),
                    pl.BlockSpec((8, CONV_DIM), const)],
        out_specs=[pl.BlockSpec((SGU_CHUNK, SGU_WIDTH), row)] * 4,
        out_shape=[jax.ShapeDtypeStruct((n_rows, SGU_WIDTH), F32)] * 4,
        compiler_params=_params("parallel"),
        name="mix_sample" if chunks_per_seq is None else "mix_prompt",
    )(uv, bch, *hist, ws_bf16, bs, gs.reshape(1, SGU_WIDTH), jnp.pad(wc, ((0, 8 - CONV_WIDTH), (0, 0))))


def _merge_body(x_ref, g_ref, pa_ref, pb_ref, pc_ref, pd_ref, sa_ref, sb_ref, sc_ref, sd_ref,
                wg_ref, wb_ref, wo_ref, o_ref, *, n_prompt_tiles):
    x = x_ref[...]
    d = x.shape[1]
    h = _rms(x, g_ref[...]).astype(BF16)
    is_prompt = pl.program_id(0) < n_prompt_tiles
    merged = jnp.zeros(x.shape, F32)
    for n, (p_ref, s_ref) in enumerate(((pa_ref, sa_ref), (pb_ref, sb_ref), (pc_ref, sc_ref), (pd_ref, sd_ref))):
        br = jnp.where(is_prompt, p_ref[...], s_ref[...]).astype(BF16)
        up = _dot(br, wb_ref[n])
        gate = _sigmoid(_dot(h, wg_ref[:, n * d:(n + 1) * d]))
        merged = merged + gate * up
    o_ref[...] = x + _dot(merged.astype(BF16), wo_ref[...])


def _merge(x, g, prompt_br, sample_br, wg_bf16, wb_bf16, wo_bf16, n_prompt):
    nt, d = x.shape
    tm = ROW_TILE
    npt = n_prompt // tm
    row = lambda i: (i, 0)
    p_spec = pl.BlockSpec((tm, BRANCH_DIM), lambda i: (jnp.minimum(i, npt - 1), 0))
    s_spec = pl.BlockSpec((tm, BRANCH_DIM), lambda i: (jnp.maximum(i - npt, 0), 0))
    return pl.pallas_call(
        functools.partial(_merge_body, n_prompt_tiles=npt),
        grid=(nt // tm,),
        in_specs=[pl.BlockSpec((tm, d), row), pl.BlockSpec((1, d), lambda i: (0, 0))]
                 + [p_spec] * N_BRANCH + [s_spec] * N_BRANCH
                 + [pl.BlockSpec((d, N_BRANCH * d), lambda i: (0, 0)),
                    pl.BlockSpec((N_BRANCH, BRANCH_DIM, d), lambda i: (0, 0, 0)),
                    pl.BlockSpec((d, d), lambda i: (0, 0))],
        out_specs=pl.BlockSpec((tm, d), row),
        out_shape=jax.ShapeDtypeStruct((nt, d), F32),
        compiler_params=_params("parallel"),
        name="merge",
    )(x, g.reshape(1, d), *prompt_br, *sample_br, wg_bf16, wb_bf16, wo_bf16)


def _silu(x):
    return x * _sigmoid(x)


def _ffn_body(x_ref, g_ref, wg_ref, wu_ref, wd_ref, o_ref):
    x = x_ref[...]
    h = _rms(x, g_ref[...]).astype(BF16)
    dff = wg_ref.shape[1]
    acc = x
    for c in range(dff // FFN_FF_TILE):
        cols = slice(c * FFN_FF_TILE, (c + 1) * FFN_FF_TILE)
        a = _silu(_dot(h, wg_ref[:, cols])) * _dot(h, wu_ref[:, cols])
        acc = acc + _dot(a.astype(BF16), wd_ref[cols, :])
    o_ref[...] = acc


def _ffn(x, g, wg_bf16, wu_bf16, wd_bf16):
    nt, d = x.shape
    dff = wg_bf16.shape[1]
    tm = FFN_ROWS
    assert dff % FFN_FF_TILE == 0
    resident = pl.Buffered(1)
    return pl.pallas_call(
        _ffn_body,
        grid=(nt // tm,),
        in_specs=[pl.BlockSpec((tm, d), lambda i: (i, 0)), pl.BlockSpec((1, d), lambda i: (0, 0)),
                  pl.BlockSpec((d, dff), lambda i: (0, 0), pipeline_mode=resident),
                  pl.BlockSpec((d, dff), lambda i: (0, 0), pipeline_mode=resident),
                  pl.BlockSpec((dff, d), lambda i: (0, 0), pipeline_mode=resident)],
        out_specs=pl.BlockSpec((tm, d), lambda i: (i, 0)),
        out_shape=jax.ShapeDtypeStruct((nt, d), F32),
        compiler_params=_params("parallel"),
        name="ffn",
    )(x, g.reshape(1, d), wg_bf16, wu_bf16, wd_bf16)


def _route_body(x_ref, g_ref, wr_ref, idx_ref, gate_ref, *, n_exp):
    h = _rms(x_ref[...], g_ref[...])
    logits = _dot_nt_f32(h, wr_ref[...])
    lane_i = lax.broadcasted_iota(jnp.int32, logits.shape, 1)
    lane = lane_i.astype(F32)
    lg = jnp.where(lane_i < n_exp, logits, -jnp.inf)
    m1 = jnp.max(lg, axis=-1, keepdims=True)
    i1 = jnp.min(jnp.where(lg == m1, lane, float(LANES)), axis=-1, keepdims=True)
    lg2 = jnp.where(lane == i1, -jnp.inf, lg)
    m2 = jnp.max(lg2, axis=-1, keepdims=True)
    i2 = jnp.min(jnp.where(lg2 == m2, lane, float(LANES)), axis=-1, keepdims=True)
    e2 = jnp.exp(m2 - m1)
    idx_ref[...] = jnp.where(lane_i == 0, i1, jnp.where(lane_i == 1, i2, 0.0)).astype(jnp.int32)
    gate_ref[...] = jnp.where(lane_i == 0, 1.0 / (1.0 + e2), jnp.where(lane_i == 1, e2 / (1.0 + e2), 0.0))


def _route(x, g, w_router):
    nt, d = x.shape
    n_exp = w_router.shape[1]
    assert n_exp <= LANES
    wr = jnp.pad(w_router.T, ((0, LANES - n_exp), (0, 0)))
    tm = ROW_TILE
    row = lambda i: (i, 0)
    return pl.pallas_call(
        functools.partial(_route_body, n_exp=n_exp),
        grid=(nt // tm,),
        in_specs=[pl.BlockSpec((tm, d), row), pl.BlockSpec((1, d), lambda i: (0, 0)),
                  pl.BlockSpec((LANES, d), lambda i: (0, 0))],
        out_specs=[pl.BlockSpec((tm, LANES), row)] * 2,
        out_shape=[jax.ShapeDtypeStruct((nt, LANES), jnp.int32), jax.ShapeDtypeStruct((nt, LANES), F32)],
        compiler_params=_params("parallel"),
        name="route",
    )(x, g.reshape(1, d), wr)


def _expert_body(be_ref, src_ref, nu_ref, x_hbm, g_ref, wg_ref, wu_ref, wd_ref, o_ref, xbuf, sem):
    del be_ref
    i = pl.program_id(0)
    nb = pl.num_programs(0)
    used = i < nu_ref[0]
    tm = xbuf.shape[1]
    dff = wg_ref.shape[1]
    tf = MOE_FF_TILE

    def row_copy(block, r, slot):
        tok = src_ref[block * tm + r]
        return pltpu.make_async_copy(x_hbm.at[pl.ds(tok, 1), :], xbuf.at[slot, pl.ds(r, 1), :], sem.at[slot])

    def wait_block(slot):
        def body(r, c):
            row_copy(0, r, slot).wait()
            return c
        lax.fori_loop(0, tm, body, 0, unroll=8)

    @pl.when(i == 0)
    def _():
        def body(r, c):
            row_copy(0, r, 0).start()
            row_copy(jnp.minimum(1, nb - 1), r, 1).start()
            return c
        lax.fori_loop(0, tm, body, 0, unroll=8)

    wait_block(i % 3)
    nxt = jnp.minimum(i + 2, nb - 1)
    nxt_slot = (i + 2) % 3

    @pl.when(jnp.logical_not(used))
    def _():
        def body(r, c):
            row_copy(nxt, r, nxt_slot).start()
            return c
        lax.fori_loop(0, tm, body, 0, unroll=8)
        o_ref[...] = jnp.zeros_like(o_ref)

    @pl.when(used)
    def _():
        for r in range(tm):
            row_copy(nxt, r, nxt_slot).start()
        h = _rms(xbuf[i % 3], g_ref[...]).astype(BF16)
        acc = jnp.zeros(o_ref.shape, F32)
        for c in range(dff // tf):
            cols = slice(c * tf, (c + 1) * tf)
            a = _silu(_dot(h, wg_ref[:, cols])) * _dot(h, wu_ref[:, cols])
            acc = acc + _dot(a.astype(BF16), wd_ref[cols, :])
        o_ref[...] = acc

    @pl.when(i == nb - 1)
    def _():
        wait_block((i + 1) % 3)
        wait_block((i + 2) % 3)


def _experts(x, src_tok, g, n_used, block_expert, wg_bf16, wu_bf16, wd_bf16):
    r = src_tok.shape[0]
    d = x.shape[1]
    dff = wg_bf16.shape[2]
    tm = MOE_ROWS
    assert dff % MOE_FF_TILE == 0
    resident = pl.Buffered(1)
    wmap = lambda i, be, st, nu: (be[i], 0, 0)
    return pl.pallas_call(
        _expert_body,
        grid_spec=pltpu.PrefetchScalarGridSpec(
            num_scalar_prefetch=3,
            grid=(r // tm,),
            in_specs=[pl.BlockSpec(memory_space=pl.ANY),
                      pl.BlockSpec((1, d), lambda i, be, st, nu: (0, 0)),
                      pl.BlockSpec((None, d, dff), wmap, pipeline_mode=resident),
                      pl.BlockSpec((None, d, dff), wmap, pipeline_mode=resident),
                      pl.BlockSpec((None, dff, d), wmap, pipeline_mode=resident)],
            out_specs=pl.BlockSpec((tm, d), lambda i, be, st, nu: (i, 0)),
            scratch_shapes=[pltpu.VMEM((3, tm, d), F32), pltpu.SemaphoreType.DMA((3,))],
        ),
        out_shape=jax.ShapeDtypeStruct((r, d), F32),
        compiler_params=_params("arbitrary"),
        name="experts",
    )(block_expert, src_tok, n_used, x, g.reshape(1, d), wg_bf16, wu_bf16, wd_bf16)


def _combine_body(d0_ref, d1_ref, x_ref, gate_ref, g_ref, src_ref, *rest, n_prompt_tiles):
    i = pl.program_id(0)
    n = pl.num_programs(0)
    buf, sem = rest[-2:]

    def copies(block, r, slot):
        t = block * GATHER_ROWS + r
        return (pltpu.make_async_copy(src_ref.at[pl.ds(d0_ref[t], 1), :], buf.at[slot, 0, pl.ds(r, 1), :],
                                      sem.at[slot]),
                pltpu.make_async_copy(src_ref.at[pl.ds(d1_ref[t], 1), :], buf.at[slot, 1, pl.ds(r, 1), :],
                                      sem.at[slot]))

    def issue(block, slot):
        def body(r, c):
            for cp in copies(block, r, slot):
                cp.start()
            return c
        lax.fori_loop(0, GATHER_ROWS, body, 0, unroll=4)

    @pl.when(i == 0)
    def _():
        issue(0, 0)

    @pl.when(i + 1 < n)
    def _():
        issue(i + 1, (i + 1) % 2)

    def wait(r, c):
        for cp in copies(0, r, i % 2):
            cp.wait()
        return c

    lax.fori_loop(0, GATHER_ROWS, wait, 0, unroll=4)
    gate = gate_ref[...]
    y = x_ref[...] + gate[:, 0:1] * buf[i % 2, 0] + gate[:, 1:2] * buf[i % 2, 1]
    if n_prompt_tiles is None:
        rest[0][...] = y
    else:
        y = _rms(y, g_ref[...])

        @pl.when(i < n_prompt_tiles)
        def _():
            rest[0][...] = y

        @pl.when(i >= n_prompt_tiles)
        def _():
            rest[1][...] = y


def _combine(x, gate, expert_rows, d0, d1, final_g=None, n_prompt=None):
    nt, d = x.shape
    tm = GATHER_ROWS
    row = lambda i, a, b: (i, 0)
    if final_g is None:
        g = jnp.ones((d,), F32)
        npt = None
        out_specs = pl.BlockSpec((tm, d), row)
        out_shape = jax.ShapeDtypeStruct((nt, d), F32)
    else:
        g = final_g
        npt = n_prompt // tm
        out_specs = [pl.BlockSpec((tm, d), lambda i, a, b: (jnp.minimum(i, npt - 1), 0)),
                     pl.BlockSpec((tm, d), lambda i, a, b: (jnp.maximum(i - npt, 0), 0))]
        out_shape = [jax.ShapeDtypeStruct((n_prompt, d), F32), jax.ShapeDtypeStruct((nt - n_prompt, d), F32)]
    return pl.pallas_call(
        functools.partial(_combine_body, n_prompt_tiles=npt),
        grid_spec=pltpu.PrefetchScalarGridSpec(
            num_scalar_prefetch=2,
            grid=(nt // tm,),
            in_specs=[pl.BlockSpec((tm, d), row),
                      pl.BlockSpec((tm, LANES), row),
                      pl.BlockSpec((1, d), lambda i, a, b: (0, 0)),
                      pl.BlockSpec(memory_space=pl.ANY)],
            out_specs=out_specs,
            scratch_shapes=[pltpu.VMEM((2, 2, tm, d), F32), pltpu.SemaphoreType.DMA((2,))],
        ),
        out_shape=out_shape,
        compiler_params=_params("arbitrary"),
        name="combine",
    )(d0, d1, x, gate, g.reshape(1, d), expert_rows)


def _moe(x, g, w_router, wg_bf16, wu_bf16, wd_bf16, final_g=None, n_prompt=None):
    nt, d = x.shape
    n_exp = w_router.shape[1]
    tm = MOE_ROWS
    idx, gate = _route(x, g, w_router)
    e_flat = idx[:, :TOP_K].reshape(-1)
    n_assign = nt * TOP_K
    onehot = (e_flat[:, None] == jnp.arange(n_exp, dtype=jnp.int32)[None, :]).astype(jnp.int32)
    rank = jnp.cumsum(onehot, axis=0) - onehot
    pos = jnp.sum(rank * onehot, axis=1)
    counts = jnp.sum(onehot, axis=0)
    padded = (counts + tm - 1) // tm * tm
    pend = jnp.cumsum(padded)
    dest = ((pend - padded)[e_flat] + pos).astype(jnp.int32)
    n_rows = (n_assign // tm + n_exp) * tm
    tok = jnp.arange(n_assign, dtype=jnp.int32) // TOP_K
    src_tok = jnp.zeros((n_rows,), jnp.int32).at[dest].set(tok, unique_indices=True)
    block_expert = jnp.minimum(
        jnp.searchsorted(pend, jnp.arange(n_rows // tm, dtype=jnp.int32) * tm, side='right'), n_exp - 1
    ).astype(jnp.int32)
    n_used = (pend[n_exp - 1:] // tm).astype(jnp.int32)
    y_rows = _experts(x, src_tok, g, n_used, block_expert, wg_bf16, wu_bf16, wd_bf16)
    dest2 = dest.reshape(nt, TOP_K)
    return _combine(x, gate, y_rows, dest2[:, 0], dest2[:, 1], final_g, n_prompt)


def _norm_body(x_ref, g_ref, o_ref):
    o_ref[...] = _rms(x_ref[...], g_ref[...])


def _final_norm(x, g):
    nt, d = x.shape
    tm = ROW_TILE
    return pl.pallas_call(
        _norm_body,
        grid=(nt // tm,),
        in_specs=[pl.BlockSpec((tm, d), lambda i: (i, 0)), pl.BlockSpec((1, d), lambda i: (0, 0))],
        out_specs=pl.BlockSpec((tm, d), lambda i: (i, 0)),
        out_shape=jax.ShapeDtypeStruct((nt, d), F32),
        compiler_params=_params("parallel"),
        name="final_norm",
    )(x, g.reshape(1, d))


def kernel(x_prompt, x_sample, cache_k, cache_v, state_conv, page_table, rel_bias, norm_mix, w_in, w_gate,
           w_sgu, b_sgu, g_sgu, w_conv, w_branch, w_out, norm_ffn, w_ff_gate, w_ff_up, w_ff_down,
           w_router, w_exp_gate, w_exp_up, w_exp_down, norm_final):
    batch, seq, d = x_prompt.shape
    n_seq, t_new, _ = x_sample.shape
    depth = w_in.shape[0]
    n_p = batch * seq
    n_s = n_seq * t_new
    assert t_new == T_NEW and seq % MOBA_BLOCK == 0 and seq // MOBA_BLOCK <= LANES
    assert n_p % FFN_ROWS == 0 and n_s % ROW_TILE == 0 and (n_p + n_s) % FFN_ROWS == 0
    n_kb = seq // MOBA_BLOCK
    group_w = SGU_WIDTH // SGU_GROUPS

    x = jnp.concatenate([x_prompt.reshape(n_p, d), x_sample.reshape(n_s, d)], axis=0)
    bt = rel_bias.T.astype(F32)
    moba_tiles = _moba_bias_tiles(bt)
    far_bias = bt[:, REL_BUCKETS - 1]

    outs = {k: [] for k in ("kp", "vp", "ks", "vs", "cp", "cs", "sv")}
    for l in range(depth):
        q, k_s, v_s, kt, vt, uv, bch, kb, vb, km = _project(x, norm_mix[l], w_in[l].astype(BF16), batch, seq)

        kmean = jnp.pad(km[:n_p // MOBA_BLOCK, 0, :].reshape(batch, n_kb, QKV), ((0, 0), (0, LANES - n_kb), (0, 0)))
        o_a = _moba_prompt(q, kb, vb, kmean, moba_tiles, far_bias, batch, seq)
        o_c = _sb_prompt(q, kb, vb, batch, seq)
        ws = jnp.tril(w_sgu[l]).astype(BF16)
        bs = jnp.repeat(b_sgu[l].T, group_w, axis=1)
        o_b, o_d, z_p, _ = _mix(uv, bch, None, ws, bs, g_sgu[l], w_conv[l], seq // SGU_CHUNK, n_p)

        o_att = _attn_sample(q[n_p:], k_s, v_s, cache_k, cache_v, l, page_table, bt)
        per_chunk = SGU_CHUNK // T_NEW
        w4 = jnp.tril(w_sgu[l][:, :T_NEW, :T_NEW])
        ws_s = jnp.einsum('ab,gts->gatbs', jnp.eye(per_chunk, dtype=F32), w4)
        ws_s = ws_s.reshape(SGU_GROUPS, SGU_CHUNK, SGU_CHUNK).astype(BF16)
        bs_s = jnp.tile(jnp.repeat(b_sgu[l][:, :T_NEW].T, group_w, axis=1), (per_chunk, 1))
        prev = state_conv[l]
        zero = jnp.zeros((n_seq, 1, CONV_DIM), F32)
        p1 = jnp.concatenate([prev[:, 1:2], zero, zero, zero], axis=1).reshape(n_s, CONV_DIM)
        p2 = jnp.concatenate([prev[:, 0:1], prev[:, 1:2], zero, zero], axis=1).reshape(n_s, CONV_DIM)
        s_b, s_d, z_s, vn_s = _mix(uv[n_p:], bch[n_p:], (p1, p2), ws_s, bs_s, g_sgu[l], w_conv[l], None, n_s)

        prompt_br = (o_a, o_b, o_c, o_d)
        sample_br = (o_att[:, :QKV // 2], s_b, o_att[:, QKV // 2:], s_d)
        x = _merge(x, norm_mix[l], prompt_br, sample_br, w_gate[l].astype(BF16), w_branch[l].astype(BF16),
                   w_out[l].astype(BF16), n_p)

        i = l // 2
        if l % 2 == 0:
            x = _ffn(x, norm_ffn[l], w_ff_gate[i].astype(BF16), w_ff_up[i].astype(BF16), w_ff_down[i].astype(BF16))
        else:
            x = _moe(x, norm_ffn[l], w_router[i], w_exp_gate[i].astype(BF16), w_exp_up[i].astype(BF16),
                     w_exp_down[i].astype(BF16), norm_final if l == depth - 1 else None, n_p)

        outs["kp"].append(kt.reshape(batch, N_HEADS, HEAD_DIM, seq).transpose(0, 3, 1, 2))
        outs["vp"].append(vt.reshape(batch, N_HEADS, HEAD_DIM, seq).transpose(0, 3, 1, 2))
        outs["ks"].append(k_s.reshape(n_seq, t_new, N_HEADS, HEAD_DIM))
        outs["vs"].append(v_s.reshape(n_seq, t_new, N_HEADS, HEAD_DIM))
        outs["cp"].append(z_p.reshape(batch, seq, CONV_DIM)[:, seq - (CONV_WIDTH - 1):])
        outs["cs"].append(z_s.reshape(n_seq, t_new, CONV_DIM)[:, t_new - (CONV_WIDTH - 1):])
        outs["sv"].append(vn_s.reshape(n_seq, t_new, SGU_WIDTH))

    if depth % 2 == 0:
        y_p, y_s = x
    else:
        y = _final_norm(x, norm_final)
        y_p, y_s = y[:n_p], y[n_p:]
    return (y_p.reshape(batch, seq, d), y_s.reshape(n_seq, t_new, d),
            jnp.stack(outs["kp"]), jnp.stack(outs["vp"]), jnp.stack(outs["ks"]), jnp.stack(outs["vs"]),
            jnp.stack(outs["cp"]), jnp.stack(outs["cs"]), jnp.stack(outs["sv"]))
```

```python
import functools
import math

import numpy as np
import jax
import jax.numpy as jnp
from jax import lax
from jax.experimental import pallas as pl
from jax.experimental.pallas import tpu as pltpu

F32 = jnp.float32
BF16 = jnp.bfloat16

NORM_EPS = 1e-6
HEAD_DIM = 64
N_HEADS = 8
QKV = N_HEADS * HEAD_DIM
MOBA_BLOCK = 256
MOBA_TOPK = 3
Q_BLOCK = 128
SGU_CHUNK = 128
SGU_WIDTH = 256
SGU_GROUPS = 4
CONV_DIM = 256
CONV_WIDTH = 3
N_BRANCH = 4
BRANCH_DIM = 256
PAGE = 128
REL_BUCKETS = 32
REL_MAX_DIST = 128
TOP_K = 2
ATT_SCALE = HEAD_DIM ** -0.5
MASKED = -1e30
SB_DROP = -105.0
LANES = 128
ROW_TILE = 256
MOE_ROWS = 512
MOE_FF_TILE = 896
FFN_ROWS = 512
FFN_FF_TILE = 256
GATHER_ROWS = 256
V7X_VMEM_LIMIT = 56 * 1024 * 1024


def _params(*sem):
    return pltpu.CompilerParams(dimension_semantics=sem, vmem_limit_bytes=V7X_VMEM_LIMIT)


def _dot(a, b):
    return jnp.dot(a, b, preferred_element_type=F32)


def _dot_nt(a, b):
    return lax.dot_general(a, b, (((1,), (1,)), ((), ())), preferred_element_type=F32)


def _split_bf16(x):
    hi = x.astype(BF16)
    lo = (x - hi.astype(F32)).astype(BF16)
    return hi, lo


def _dot_nt_f32(a, b):
    ah, al = _split_bf16(a)
    bh, bl = _split_bf16(b)
    return _dot_nt(ah, bh) + _dot_nt(ah, bl) + _dot_nt(al, bh)


def _dot_f32(a, b):
    ah, al = _split_bf16(a)
    bh, bl = _split_bf16(b)
    return _dot(ah, bh) + _dot(ah, bl) + _dot(al, bh)


def _rms(x, g):
    return x * lax.rsqrt(jnp.mean(x * x, axis=-1, keepdims=True) + NORM_EPS) * g


def _gelu_tanh(x):
    return 0.5 * x * (1.0 + jnp.tanh(math.sqrt(2.0 / math.pi) * (x + 0.044715 * (x * x * x))))


def _sigmoid(x):
    return 1.0 / (1.0 + jnp.exp(-x))


def _softplus(z):
    return jnp.maximum(z, 0.0) + jnp.log1p(jnp.exp(-jnp.abs(z)))


def _suffix_sum(lk, u):
    n = lk.shape[0]
    hi, lo = _split_bf16(lk)
    r = _dot(jnp.concatenate([hi, lo], axis=0), u)
    return r[:n] + r[n:]


def _strict_upper(n):
    j = np.arange(n)[:, None]
    s = np.arange(n)[None, :]
    return jnp.asarray((j > s).astype(np.float32), dtype=BF16)


def _t5_bucket_table(max_dist):
    d = np.arange(max_dist + 1)
    max_exact = REL_BUCKETS // 2
    large = max_exact + (np.log(np.maximum(d, 1).astype(np.float32) / np.float32(max_exact))
                         / np.float32(math.log(REL_MAX_DIST / max_exact))
                         * np.float32(REL_BUCKETS - max_exact)).astype(np.int32)
    large = np.minimum(large, REL_BUCKETS - 1)
    return np.where(d < max_exact, d, large).astype(np.int32)


def _bias_by_dist(bt_rows, dist, valid):
    table = _t5_bucket_table(2 * REL_MAX_DIST)
    b = jnp.broadcast_to(bt_rows[:, 0:1], dist.shape)
    for k in range(1, REL_BUCKETS):
        first = int(np.argmax(table >= k))
        b = jnp.where(dist >= first, bt_rows[:, k:k + 1], b)
    return jnp.where(valid, b, MASKED)


def _proj_body(x_ref, g_ref, w_ref, q_ref, ks_ref, vs_ref, kt_ref, vt_ref, uv_ref, bch_ref, kb_ref, vb_ref, km_ref,
               *, n_prompt_tiles):
    i = pl.program_id(0)
    h = _rms(x_ref[...], g_ref[...]).astype(BF16)

    def mm(lo, hi):
        return _dot(h, w_ref[:, lo:hi])

    q_ref[...] = mm(0, QKV)
    k = mm(QKV, 2 * QKV)
    kb_ref[...] = k.astype(BF16)
    km_ref[...] = jnp.broadcast_to(jnp.mean(k, axis=0, keepdims=True), km_ref.shape)
    v = mm(2 * QKV, 3 * QKV)
    vb_ref[...] = v.astype(BF16)
    uv_ref[...] = mm(3 * QKV, 3 * QKV + 2 * SGU_WIDTH)
    bch_ref[...] = mm(3 * QKV + 2 * SGU_WIDTH, 3 * QKV + 2 * SGU_WIDTH + 3 * CONV_DIM)

    @pl.when(i < n_prompt_tiles)
    def _():
        kt_ref[...] = k.T
        vt_ref[...] = v.T

    @pl.when(i >= n_prompt_tiles)
    def _():
        ks_ref[...] = k
        vs_ref[...] = v


def _project(x, g, w_bf16, batch, seq):
    nt, d = x.shape
    in_dim = w_bf16.shape[1]
    tm = MOBA_BLOCK
    nb = nt // tm
    per_seq = seq // tm
    npt = batch * per_seq
    row = lambda i: (i, 0)
    srow = lambda i: (jnp.maximum(i - npt, 0), 0)
    tcol = lambda i: (jnp.minimum(i, npt - 1) // per_seq, 0, jnp.minimum(i, npt - 1) % per_seq)
    return pl.pallas_call(
        functools.partial(_proj_body, n_prompt_tiles=npt),
        grid=(nb,),
        in_specs=[pl.BlockSpec((tm, d), row),
                  pl.BlockSpec((1, d), lambda i: (0, 0)),
                  pl.BlockSpec((d, in_dim), lambda i: (0, 0))],
        out_specs=[pl.BlockSpec((tm, QKV), row), pl.BlockSpec((tm, QKV), srow), pl.BlockSpec((tm, QKV), srow),
                   pl.BlockSpec((None, QKV, tm), tcol), pl.BlockSpec((None, QKV, tm), tcol),
                   pl.BlockSpec((tm, 2 * SGU_WIDTH), row), pl.BlockSpec((tm, 3 * CONV_DIM), row),
                   pl.BlockSpec((tm, QKV), row), pl.BlockSpec((tm, QKV), row),
                   pl.BlockSpec((None, 8, QKV), lambda i: (i, 0, 0))],
        out_shape=[jax.ShapeDtypeStruct((nt, QKV), F32),
                   jax.ShapeDtypeStruct((nt - npt * tm, QKV), F32), jax.ShapeDtypeStruct((nt - npt * tm, QKV), F32),
                   jax.ShapeDtypeStruct((batch, QKV, seq), F32), jax.ShapeDtypeStruct((batch, QKV, seq), F32),
                   jax.ShapeDtypeStruct((nt, 2 * SGU_WIDTH), F32), jax.ShapeDtypeStruct((nt, 3 * CONV_DIM), F32),
                   jax.ShapeDtypeStruct((nt, QKV), BF16), jax.ShapeDtypeStruct((nt, QKV), BF16),
                   jax.ShapeDtypeStruct((nb, 8, QKV), F32)],
        compiler_params=_params("arbitrary"),
        name="proj",
    )(x, g.reshape(1, d), w_bf16)


def _top_blocks(gate, lane, lane_mask):
    lane = lane.astype(F32)
    g = jnp.where(lane_mask, gate, -jnp.inf)
    picks = []
    for _ in range(MOBA_TOPK):
        m = jnp.max(g, axis=-1, keepdims=True)
        idx = jnp.min(jnp.where(g == m, lane, 1e9), axis=-1, keepdims=True)
        picks.append((idx, m > -jnp.inf))
        g = jnp.where(lane == idx, -jnp.inf, g)
    return picks


def _is_picked(picks, j):
    j = jnp.asarray(j).astype(F32)
    c = (picks[0][0] == j) & picks[0][1]
    for idx, ok in picks[1:]:
        c = c | ((idx == j) & ok)
    return c


def _moba_prompt_body(fb_ref, q_ref, k_ref, v_ref, km_ref, bias_ref, spread_ref, o_ref, *, chunk, gate_rows):
    pr = pl.program_id(1)
    qi = pl.program_id(2)
    cur = qi
    rows = 2 * MOBA_BLOCK
    lane = lax.broadcasted_iota(jnp.int32, (MOBA_BLOCK, LANES), 1)
    q = q_ref[...] * ATT_SCALE
    qs = jnp.concatenate([jnp.where(lane < HEAD_DIM, q, 0.0), jnp.where(lane >= HEAD_DIM, q, 0.0)], axis=0)
    qsb = qs.astype(BF16)
    first_head = lax.broadcasted_iota(jnp.int32, (rows, 1), 0) < MOBA_BLOCK
    far_bias = jnp.where(first_head, fb_ref[2 * pr], fb_ref[2 * pr + 1])

    blk = lax.broadcasted_iota(jnp.int32, (gate_rows, rows), 0)
    blk_f = blk.astype(F32)
    g = jnp.where(blk < cur, _dot_nt_f32(km_ref[0:gate_rows, :], qs), -jnp.inf)
    picked = None
    for _ in range(MOBA_TOPK):
        best = jnp.max(g, axis=0, keepdims=True)
        hit = blk_f == jnp.min(jnp.where(g == best, blk_f, 1e9), axis=0, keepdims=True)
        chosen = hit & (best > -jnp.inf)
        picked = chosen if picked is None else picked | chosen
        g = jnp.where(hit, -jnp.inf, g)
    fill = jnp.full((LANES // 2 - gate_rows, rows), MASKED, F32)
    older = jnp.where(picked & (blk < cur - 1), 0.0, MASKED)
    prev = jnp.where(picked & (blk == cur - 1), 0.0, MASKED)
    add_rows = jnp.concatenate([older, fill, prev, fill], axis=0).T.astype(BF16)

    start = pl.multiple_of(cur * MOBA_BLOCK, MOBA_BLOCK)
    start_p = pl.multiple_of(jnp.maximum(cur - 1, 0) * MOBA_BLOCK, MOBA_BLOCK)
    q_and_mask = jnp.concatenate([qsb, add_rows], axis=1)
    spread_prev = lax.broadcasted_iota(jnp.int32, (MOBA_BLOCK, LANES), 1) == LANES // 2 + cur - 1
    keys_prev = jnp.concatenate([k_ref[pl.ds(start_p, MOBA_BLOCK), :], spread_prev.astype(BF16)], axis=1)
    s_own = _dot_nt(qsb, k_ref[pl.ds(start, MOBA_BLOCK), :]) + bias_ref[0]
    s_prev = _dot_nt(q_and_mask, keys_prev) + bias_ref[1]
    m = jnp.maximum(jnp.max(s_own, axis=-1, keepdims=True), jnp.max(s_prev, axis=-1, keepdims=True))
    p_own = jnp.exp(s_own - m)
    p_prev = jnp.exp(s_prev - m)
    l = jnp.sum(p_own, axis=-1, keepdims=True) + jnp.sum(p_prev, axis=-1, keepdims=True)
    acc = (_dot(p_own.astype(BF16), v_ref[pl.ds(start, MOBA_BLOCK), :])
           + _dot(p_prev.astype(BF16), v_ref[pl.ds(start_p, MOBA_BLOCK), :]))

    def chunk_step(c, carry):
        m, l, acc = carry
        start = pl.multiple_of(c * (chunk * MOBA_BLOCK), chunk * MOBA_BLOCK)
        keys = jnp.concatenate([k_ref[pl.ds(start, chunk * MOBA_BLOCK), :], spread_ref[c]], axis=1)
        s = _dot_nt(q_and_mask, keys) + far_bias
        m_new = jnp.maximum(m, jnp.max(s, axis=-1, keepdims=True))
        alpha = jnp.exp(m - m_new)
        p = jnp.exp(s - m_new)
        l = alpha * l + jnp.sum(p, axis=-1, keepdims=True)
        acc = alpha * acc + _dot(p.astype(BF16), v_ref[pl.ds(start, chunk * MOBA_BLOCK), :])
        return m_new, l, acc

    n_older = jnp.maximum(cur - 1, 0)
    m, l, acc = lax.fori_loop(0, (n_older + chunk - 1) // chunk, chunk_step, (m, l, acc))
    out = acc / l
    o_ref[...] = jnp.where(lane < HEAD_DIM, out[:MOBA_BLOCK], out[MOBA_BLOCK:])


def _moba_bias_tiles(bt):
    n_pairs = bt.shape[0] // 2
    shape = (n_pairs * 2 * MOBA_BLOCK, MOBA_BLOCK)
    i = lax.broadcasted_iota(jnp.int32, shape, 0) & (MOBA_BLOCK - 1)
    c = lax.broadcasted_iota(jnp.int32, shape, 1)
    bt_rows = jnp.repeat(bt, MOBA_BLOCK, axis=0)
    tiles = []
    for delta in (0, MOBA_BLOCK):
        dist = delta + i - c
        tiles.append(_bias_by_dist(bt_rows, dist, dist >= 0).reshape(n_pairs, 2 * MOBA_BLOCK, MOBA_BLOCK))
    return jnp.stack(tiles, axis=1)


def _moba_prompt(q, kb, vb, kmean, bias_tiles, far_bias, batch, seq):
    n_kb = seq // MOBA_BLOCK
    nq = n_kb
    chunk = next(c for c in (4, 2, 1) if n_kb % c == 0)
    gate_rows = -(-n_kb // 8) * 8
    assert gate_rows <= LANES // 2
    blk = np.arange(LANES)[None, None, :]
    key = np.arange(chunk * MOBA_BLOCK)[None, :, None]
    cid = np.arange(n_kb // chunk)[:, None, None]
    spread = jnp.asarray((blk == cid * chunk + key // MOBA_BLOCK).astype(np.float32), dtype=BF16)
    return pl.pallas_call(
        functools.partial(_moba_prompt_body, chunk=chunk, gate_rows=gate_rows),
        grid_spec=pltpu.PrefetchScalarGridSpec(
            num_scalar_prefetch=1,
            grid=(batch, 2, nq),
            in_specs=[pl.BlockSpec((MOBA_BLOCK, LANES), lambda b, p, i, fb: (b * nq + i, p)),
                      pl.BlockSpec((seq, LANES), lambda b, p, i, fb: (b, p)),
                      pl.BlockSpec((seq, LANES), lambda b, p, i, fb: (b, p)),
                      pl.BlockSpec((None, LANES, LANES), lambda b, p, i, fb: (b, 0, p)),
                      pl.BlockSpec((None, 2, 2 * MOBA_BLOCK, MOBA_BLOCK), lambda b, p, i, fb: (p, 0, 0, 0)),
                      pl.BlockSpec(spread.shape, lambda b, p, i, fb: (0, 0, 0))],
            out_specs=pl.BlockSpec((MOBA_BLOCK, LANES), lambda b, p, i, fb: (b * nq + i, p)),
        ),
        out_shape=jax.ShapeDtypeStruct((batch * seq, 2 * LANES), F32),
        compiler_params=_params("parallel", "parallel", "arbitrary"),
        name="moba_prompt",
    )(far_bias, q, kb, vb, kmean, bias_tiles, spread)


def _sb_prompt_body(q_ref, k_ref, v_ref, u_ref, o_ref):
    qi = pl.program_id(2)
    rows = 2 * Q_BLOCK
    lane = lax.broadcasted_iota(jnp.int32, (Q_BLOCK, LANES), 1)
    q = q_ref[...] * ATT_SCALE
    qsb = jnp.concatenate([jnp.where(lane < HEAD_DIM, q, 0.0), jnp.where(lane >= HEAD_DIM, q, 0.0)],
                          axis=0).astype(BF16)
    u = u_ref[...]
    col = lax.broadcasted_iota(jnp.int32, (rows, Q_BLOCK), 1)
    qrow = lax.broadcasted_iota(jnp.int32, (rows, Q_BLOCK), 0) & (Q_BLOCK - 1)

    def tile(j, keep, r_acc, acc):
        start = pl.multiple_of(jnp.maximum(j, 0) * Q_BLOCK, Q_BLOCK)
        z = _dot_nt(qsb, k_ref[pl.ds(start, Q_BLOCK), :])
        sp = _softplus(z)
        lk = -sp if keep is None else jnp.where(keep, -sp, 0.0)
        a = jnp.exp(z - sp + _suffix_sum(lk, u) + r_acc)
        if keep is not None:
            a = jnp.where(keep, a, 0.0)
        acc = acc + _dot(a.astype(BF16), v_ref[pl.ds(start, Q_BLOCK), :])
        return r_acc + jnp.sum(lk, axis=-1, keepdims=True), acc

    def exists(j):
        return jnp.broadcast_to(j >= 0, (rows, Q_BLOCK))

    r_acc, acc = tile(qi, col < qrow, jnp.zeros((rows, 1), F32), jnp.zeros((rows, LANES), F32))
    r_acc, acc = tile(qi - 1, exists(qi - 1), r_acc, acc)
    r_acc, acc = tile(qi - 2, exists(qi - 2), r_acc, acc)

    def more(state):
        it, go, _, _ = state
        return (qi - 3 - 2 * it >= 0) & (go > 0)

    def older(state):
        it, _, r_acc, acc = state
        j = qi - 3 - 2 * it
        r_acc, acc = tile(j, None, r_acc, acc)
        r_acc, acc = tile(j - 1, exists(j - 1), r_acc, acc)
        return it + 1, (jnp.max(r_acc) > SB_DROP).astype(jnp.int32), r_acc, acc

    go = (jnp.max(r_acc) > SB_DROP).astype(jnp.int32)
    _, _, _, acc = lax.while_loop(more, older, (jnp.int32(0), go, r_acc, acc))
    o_ref[...] = jnp.where(lane < HEAD_DIM, acc[:Q_BLOCK], acc[Q_BLOCK:])


def _sb_prompt(q, kb, vb, batch, seq):
    nq = seq // Q_BLOCK
    return pl.pallas_call(
        _sb_prompt_body,
        grid=(batch, 2, nq),
        in_specs=[pl.BlockSpec((Q_BLOCK, LANES), lambda b, p, i: (b * nq + i, 2 + p)),
                  pl.BlockSpec((seq, LANES), lambda b, p, i: (b, 2 + p)),
                  pl.BlockSpec((seq, LANES), lambda b, p, i: (b, 2 + p)),
                  pl.BlockSpec((Q_BLOCK, Q_BLOCK), lambda b, p, i: (0, 0))],
        out_specs=pl.BlockSpec((Q_BLOCK, LANES), lambda b, p, i: (b * nq + i, p)),
        out_shape=jax.ShapeDtypeStruct((batch * seq, 2 * LANES), F32),
        compiler_params=_params("parallel", "parallel", "arbitrary"),
        name="sb_prompt",
    )(q, kb, vb, _strict_upper(Q_BLOCK))


T_NEW = 4
SOFTMAX_ROWS = 4 * T_NEW


SB_STATIC_PAGES = 3
HALF = QKV // 2


def _own_head_fold(o):
    row_head = lax.broadcasted_iota(jnp.int32, o.shape, 0) >> 2
    col_head = lax.broadcasted_iota(jnp.int32, o.shape, 1) >> 6
    o = jnp.where(row_head == col_head, o, 0.0)
    out = o[:, 0:HEAD_DIM]
    for h in range(1, HALF // HEAD_DIM):
        out = out + o[:, h * HEAD_DIM:(h + 1) * HEAD_DIM]
    return out


def _attn_sample_body(pt_ref, qa_ref, qb_ref, knew_ref, vnew_ref, bias_ref, bias_new_ref, u_ref, ck_ref, cv_ref,
                      *rest, n_pages, n_static, layer):
    ka_refs = rest[:n_pages]
    va_refs = rest[n_pages:2 * n_pages]
    kb_refs = rest[2 * n_pages:2 * n_pages + n_static]
    vb_refs = rest[2 * n_pages + n_static:2 * n_pages + 2 * n_static]
    o_ref, kbuf, vbuf, sem = rest[2 * n_pages + 2 * n_static:]
    seq = pl.program_id(0)
    n_blocks = n_pages // 2
    sr = SOFTMAX_ROWS
    past = n_pages * PAGE

    qa = qa_ref[...]
    qab = qa.astype(BF16)
    qbb = qb_ref[...].astype(BF16)

    blk_lane = lax.broadcasted_iota(jnp.int32, (HALF, LANES), 1)
    km = jnp.zeros((HALF, LANES), F32)
    parts = []
    for p in range(n_pages):
        kp = ka_refs[p][...].reshape(HALF, PAGE)
        parts.append(_dot(qab, kp.astype(BF16)))
        if p % 2 == 0:
            ksum = kp
        else:
            mean = jnp.sum(ksum + kp, axis=1, keepdims=True) * (1.0 / MOBA_BLOCK)
            km = jnp.where(blk_lane == p // 2, mean, km)
    s = jnp.concatenate(parts, axis=1)

    pad_rows = jnp.zeros((LANES - 8, QKV), F32)
    knew = jnp.concatenate([knew_ref[...], pad_rows], axis=0).astype(BF16)
    vnew = jnp.concatenate([vnew_ref[...], pad_rows], axis=0).astype(BF16)

    lane_s = lax.broadcasted_iota(jnp.int32, (sr, LANES), 1)
    picks = _top_blocks(_dot_f32(qa, km), lane_s, lane_s < n_blocks)
    key_block = (lax.broadcasted_iota(jnp.int32, (sr, past), 1) >> 8).astype(F32)
    la = jnp.where(_is_picked(picks, key_block), s + bias_ref[...], MASKED)
    la_new = _dot_nt(qab, knew[:, :HALF]) + bias_new_ref[...]
    m = jnp.maximum(jnp.max(la, axis=-1, keepdims=True), jnp.max(la_new, axis=-1, keepdims=True))
    pa = jnp.exp(la - m)
    pa_new = jnp.exp(la_new - m)
    denom = jnp.sum(pa, axis=-1, keepdims=True) + jnp.sum(pa_new, axis=-1, keepdims=True)
    pa = pa.astype(BF16)
    o_a = _dot(pa_new.astype(BF16), vnew[:, :HALF])
    for p in range(n_pages):
        o_a = o_a + _dot_nt(pa[:, p * PAGE:(p + 1) * PAGE], va_refs[p][...].reshape(HALF, PAGE).astype(BF16))

    u = u_ref[...]

    def sticks(zb, r_acc, keep=None):
        sp = _softplus(zb)
        lk = -sp if keep is None else jnp.where(keep, -sp, 0.0)
        a = jnp.exp(zb - sp + _suffix_sum(lk, u) + r_acc)
        if keep is not None:
            a = jnp.where(keep, a, 0.0)
        return a.astype(BF16), r_acc + jnp.sum(lk, axis=-1, keepdims=True)

    keep_new = lane_s < (lax.broadcasted_iota(jnp.int32, (sr, LANES), 0) & (T_NEW - 1))
    zs = [_dot_nt(qbb, knew[:, HALF:])] + [_dot(qbb, kb_refs[t][...].reshape(HALF, PAGE).astype(BF16))
                                          for t in range(n_static)]
    sps = [_softplus(z) for z in zs]
    lks = [jnp.where(keep_new, -sps[0], 0.0)] + [-sp for sp in sps[1:]]
    within = _suffix_sum(jnp.concatenate(lks, axis=0), u)
    r_acc = jnp.zeros((sr, 1), F32)
    o_b = None
    for t, (z, sp, lk) in enumerate(zip(zs, sps, lks)):
        a = jnp.exp(z - sp + within[t * sr:(t + 1) * sr] + r_acc)
        r_acc = r_acc + jnp.sum(lk, axis=-1, keepdims=True)
        if t == 0:
            o_b = _dot(jnp.where(keep_new, a, 0.0).astype(BF16), vnew[:, HALF:])
        else:
            o_b = o_b + _dot_nt(a.astype(BF16), vb_refs[t - 1][...].reshape(HALF, PAGE).astype(BF16))

    def more(state):
        p, go, _, _ = state
        return (p >= 0) & (go > 0)

    def older(state):
        p, _, r_acc, o_b = state
        page = pt_ref[seq * n_pages + p]
        copies = (pltpu.make_async_copy(ck_ref.at[layer, page, pl.ds(N_HEADS // 2, N_HEADS // 2)], kbuf, sem.at[0]),
                  pltpu.make_async_copy(cv_ref.at[layer, page, pl.ds(N_HEADS // 2, N_HEADS // 2)], vbuf, sem.at[1]))
        for cp in copies:
            cp.start()
        for cp in copies:
            cp.wait()
        a, r_acc = sticks(_dot(qbb, kbuf[...].reshape(HALF, PAGE).astype(BF16)), r_acc)
        o_b = o_b + _dot_nt(a, vbuf[...].reshape(HALF, PAGE).astype(BF16))
        return p - 1, (jnp.max(r_acc) > SB_DROP).astype(jnp.int32), r_acc, o_b

    go = (jnp.max(r_acc) > SB_DROP).astype(jnp.int32)
    _, _, _, o_b = lax.while_loop(more, older, (jnp.int32(n_pages - n_static - 1), go, r_acc, o_b))

    o_ref[...] = jnp.concatenate([_own_head_fold(o_a) * (1.0 / denom), _own_head_fold(o_b)], axis=0)


def _attn_sample(q_s, k_s, v_s, cache_k, cache_v, layer, page_table, bt):
    n_seq, n_pages = page_table.shape
    assert q_s.shape[0] == n_seq * T_NEW and n_pages % 2 == 0
    past = n_pages * PAGE
    n_rows = N_HEADS * T_NEW
    n_static = min(SB_STATIC_PAGES, n_pages)
    half_heads = N_HEADS // 2
    q4 = (q_s * ATT_SCALE).reshape(n_seq, T_NEW, 2, half_heads, 1, HEAD_DIM).transpose(2, 0, 3, 1, 4, 5)
    eye = jnp.eye(half_heads, dtype=F32).reshape(1, 1, half_heads, 1, half_heads, 1)
    qx = (q4 * eye).reshape(2, n_seq, SOFTMAX_ROWS, HALF)
    pad = ((0, 0), (0, 8 - T_NEW), (0, 0))
    knew = jnp.pad(k_s.reshape(n_seq, T_NEW, QKV), pad)
    vnew = jnp.pad(v_s.reshape(n_seq, T_NEW, QKV), pad)
    ck = cache_k.transpose(0, 1, 3, 4, 2)
    cv = cache_v.transpose(0, 1, 3, 4, 2)

    bt_rows = jnp.repeat(bt[:4], T_NEW, axis=0)

    def key_bias(width, first_key_dist, new_keys):
        r = lax.broadcasted_iota(jnp.int32, (SOFTMAX_ROWS, width), 0)
        c = lax.broadcasted_iota(jnp.int32, (SOFTMAX_ROWS, width), 1)
        dist = first_key_dist + (r & (T_NEW - 1)) - c
        valid = (dist >= 0) & (c < T_NEW) if new_keys else dist >= 0
        return _bias_by_dist(bt_rows, dist, valid)

    bias_past = key_bias(past, past, False)
    bias_new = key_bias(LANES, 0, True)

    def page_spec(p, head_group):
        return pl.BlockSpec((None, None, half_heads, HEAD_DIM, PAGE),
                            lambda b, pt, p=p: (layer, pt[b * n_pages + p], head_group, 0, 0))

    softmax_pages = [page_spec(p, 0) for p in range(n_pages)]
    stick_pages = [page_spec(n_pages - 1 - t, 1) for t in range(n_static)]
    seq3 = lambda b, pt: (b, 0, 0)
    const2 = lambda b, pt: (0, 0)
    out = pl.pallas_call(
        functools.partial(_attn_sample_body, n_pages=n_pages, n_static=n_static, layer=layer),
        grid_spec=pltpu.PrefetchScalarGridSpec(
            num_scalar_prefetch=1,
            grid=(n_seq,),
            in_specs=[pl.BlockSpec((None, SOFTMAX_ROWS, HALF), seq3),
                      pl.BlockSpec((None, SOFTMAX_ROWS, HALF), seq3),
                      pl.BlockSpec((None, 8, QKV), seq3),
                      pl.BlockSpec((None, 8, QKV), seq3),
                      pl.BlockSpec((SOFTMAX_ROWS, past), const2),
                      pl.BlockSpec((SOFTMAX_ROWS, LANES), const2),
                      pl.BlockSpec((PAGE, PAGE), const2),
                      pl.BlockSpec(memory_space=pl.ANY),
                      pl.BlockSpec(memory_space=pl.ANY)]
                     + softmax_pages * 2 + stick_pages * 2,
            out_specs=pl.BlockSpec((None, n_rows, HEAD_DIM), seq3),
            scratch_shapes=[pltpu.VMEM((half_heads, HEAD_DIM, PAGE), F32),
                            pltpu.VMEM((half_heads, HEAD_DIM, PAGE), F32),
                            pltpu.SemaphoreType.DMA((2,))],
        ),
        out_shape=jax.ShapeDtypeStruct((n_seq, n_rows, HEAD_DIM), F32),
        compiler_params=_params("arbitrary"),
        name="attn_sample",
    )(page_table.reshape(-1), qx[0], qx[1], knew, vnew, bias_past, bias_new, _strict_upper(PAGE), ck, cv,
      *([ck] * n_pages), *([cv] * n_pages), *([ck] * n_static), *([cv] * n_static))
    return out.reshape(n_seq, N_HEADS, T_NEW, HEAD_DIM).transpose(0, 2, 1, 3).reshape(n_seq * T_NEW, QKV)


def _mix_body(*refs, steps_per_seq):
    if steps_per_seq is None:
        uv_ref, bch_ref, p1_ref, p2_ref, ws_ref, bs_ref, gs_ref, wc_ref, ob_ref, od_ref, z_ref, vn_ref = refs
    else:
        uv_ref, bch_ref, prev_ref, ws_ref, bs_ref, gs_ref, wc_ref, ob_ref, od_ref, z_ref, vn_ref = refs
    uv = uv_ref[...]
    x = _gelu_tanh(uv[:, SGU_WIDTH:])
    xc = x - jnp.mean(x, axis=-1, keepdims=True)
    vn = xc * lax.rsqrt(jnp.mean(xc * xc, axis=-1, keepdims=True) + NORM_EPS) * gs_ref[...]
    vn_ref[...] = vn
    vnb = vn.astype(BF16)
    n_rows = uv.shape[0]
    group = lax.broadcasted_iota(jnp.int32, (SGU_CHUNK, SGU_WIDTH), 1) >> 6
    gated = []
    for c in range(n_rows // SGU_CHUNK):
        vc = vnb[c * SGU_CHUNK:(c + 1) * SGU_CHUNK]
        s = jnp.zeros((SGU_CHUNK, SGU_WIDTH), F32)
        for g in range(SGU_GROUPS):
            s = jnp.where(group == g, _dot(ws_ref[g], vc), s)
        gated.append(s + bs_ref[...])
    ob_ref[...] = _gelu_tanh(uv[:, :SGU_WIDTH]) * jnp.concatenate(gated, axis=0)

    bch = bch_ref[...]
    z = bch[:, CONV_DIM:2 * CONV_DIM] * bch[:, 2 * CONV_DIM:]
    z_ref[...] = z
    row = lax.broadcasted_iota(jnp.int32, (n_rows, CONV_DIM), 0)
    r1 = pltpu.roll(z, 1, 0)
    r2 = pltpu.roll(z, 2, 0)
    if steps_per_seq is None:
        rr = row & (T_NEW - 1)
        z1 = jnp.where(rr >= 1, r1, p1_ref[...])
        z2 = jnp.where(rr >= 2, r2, p2_ref[...])
    else:
        pb = prev_ref[...]
        zp = pb[:, CONV_DIM:2 * CONV_DIM] * pb[:, 2 * CONV_DIM:]
        zp = jnp.where(pl.program_id(0) % steps_per_seq == 0, 0.0, zp)
        z1 = jnp.where(row >= 1, r1, zp[7:8, :])
        z2 = jnp.where(row >= 2, r2, jnp.where(row == 1, zp[7:8, :], zp[6:7, :]))
    wc = wc_ref[...]
    od_ref[...] = bch[:, :CONV_DIM] * (wc[0:1, :] * z2 + wc[1:2, :] * z1 + wc[2:3, :] * z)


def _mix(uv, bch, history, ws_bf16, bs, gs, wc, chunks_per_seq, n_rows):
    per_step = next(c for c in (4, 2, 1) if n_rows % (c * SGU_CHUNK) == 0
                    and (chunks_per_seq is None or chunks_per_seq % c == 0))
    tm = per_step * SGU_CHUNK
    row = lambda i: (i, 0)
    const = lambda i: (0, 0)
    if chunks_per_seq is None:
        steps_per_seq = None
        hist_specs = [pl.BlockSpec((tm, CONV_DIM), row)] * 2
        hist = list(history)
    else:
        steps_per_seq = chunks_per_seq // per_step
        sub = tm // 8
        hist_specs = [pl.BlockSpec((8, 3 * CONV_DIM), lambda i: (jnp.maximum(i * sub - 1, 0), 0))]
        hist = [bch]
    return pl.pallas_call(
        functools.partial(_mix_body, steps_per_seq=steps_per_seq),
        grid=(n_rows // tm,),
        in_specs=[pl.BlockSpec((tm, 2 * SGU_WIDTH), row), pl.BlockSpec((tm, 3 * CONV_DIM), row)]
                 + hist_specs
                 + [pl.BlockSpec((SGU_GROUPS, SGU_CHUNK, SGU_CHUNK), lambda i: (0, 0, 0)),
                    pl.BlockSpec((SGU_CHUNK, SGU_WIDTH), const),
                    pl.BlockSpec((1, SGU_WIDTH), const),
                    pl.BlockSpec((8, CONV_DIM), const)],
        out_specs=[pl.BlockSpec((tm, SGU_WIDTH), row)] * 4,
        out_shape=[jax.ShapeDtypeStruct((n_rows, SGU_WIDTH), F32)] * 4,
        compiler_params=_params("parallel"),
        name="mix_sample" if chunks_per_seq is None else "mix_prompt",
    )(uv, bch, *hist, ws_bf16, bs, gs.reshape(1, SGU_WIDTH), jnp.pad(wc, ((0, 8 - CONV_WIDTH), (0, 0))))


def _merge_body(x_ref, g_ref, pa_ref, pb_ref, pc_ref, pd_ref, sa_ref, sb_ref, sc_ref, sd_ref,
                wg_ref, wb_ref, wo_ref, o_ref, *, n_prompt_tiles):
    x = x_ref[...]
    d = x.shape[1]
    h = _rms(x, g_ref[...]).astype(BF16)
    is_prompt = pl.program_id(0) < n_prompt_tiles
    merged = jnp.zeros(x.shape, F32)
    for n, (p_ref, s_ref) in enumerate(((pa_ref, sa_ref), (pb_ref, sb_ref), (pc_ref, sc_ref), (pd_ref, sd_ref))):
        br = jnp.where(is_prompt, p_ref[...], s_ref[...]).astype(BF16)
        up = _dot(br, wb_ref[n])
        gate = _sigmoid(_dot(h, wg_ref[:, n * d:(n + 1) * d]))
        merged = merged + gate * up
    o_ref[...] = x + _dot(merged.astype(BF16), wo_ref[...])


def _merge(x, g, prompt_br, sample_br, wg_bf16, wb_bf16, wo_bf16, n_prompt):
    nt, d = x.shape
    tm = ROW_TILE
    npt = n_prompt // tm
    row = lambda i: (i, 0)
    p_spec = pl.BlockSpec((tm, BRANCH_DIM), lambda i: (jnp.minimum(i, npt - 1), 0))
    s_spec = pl.BlockSpec((tm, BRANCH_DIM), lambda i: (jnp.maximum(i - npt, 0), 0))
    return pl.pallas_call(
        functools.partial(_merge_body, n_prompt_tiles=npt),
        grid=(nt // tm,),
        in_specs=[pl.BlockSpec((tm, d), row), pl.BlockSpec((1, d), lambda i: (0, 0))]
                 + [p_spec] * N_BRANCH + [s_spec] * N_BRANCH
                 + [pl.BlockSpec((d, N_BRANCH * d), lambda i: (0, 0)),
                    pl.BlockSpec((N_BRANCH, BRANCH_DIM, d), lambda i: (0, 0, 0)),
                    pl.BlockSpec((d, d), lambda i: (0, 0))],
        out_specs=pl.BlockSpec((tm, d), row),
        out_shape=jax.ShapeDtypeStruct((nt, d), F32),
        compiler_params=_params("parallel"),
        name="merge",
    )(x, g.reshape(1, d), *prompt_br, *sample_br, wg_bf16, wb_bf16, wo_bf16)


def _silu(x):
    return x * _sigmoid(x)


def _ffn_body(x_ref, g_ref, wg_ref, wu_ref, wd_ref, o_ref):
    x = x_ref[...]
    h = _rms(x, g_ref[...]).astype(BF16)
    dff = wg_ref.shape[1]
    acc = x
    for c in range(dff // FFN_FF_TILE):
        cols = slice(c * FFN_FF_TILE, (c + 1) * FFN_FF_TILE)
        a = _silu(_dot(h, wg_ref[:, cols])) * _dot(h, wu_ref[:, cols])
        acc = acc + _dot(a.astype(BF16), wd_ref[cols, :])
    o_ref[...] = acc


def _ffn(x, g, wg_bf16, wu_bf16, wd_bf16):
    nt, d = x.shape
    dff = wg_bf16.shape[1]
    tm = FFN_ROWS
    assert dff % FFN_FF_TILE == 0
    resident = pl.Buffered(1)
    return pl.pallas_call(
        _ffn_body,
        grid=(nt // tm,),
        in_specs=[pl.BlockSpec((tm, d), lambda i: (i, 0)), pl.BlockSpec((1, d), lambda i: (0, 0)),
                  pl.BlockSpec((d, dff), lambda i: (0, 0), pipeline_mode=resident),
                  pl.BlockSpec((d, dff), lambda i: (0, 0), pipeline_mode=resident),
                  pl.BlockSpec((dff, d), lambda i: (0, 0), pipeline_mode=resident)],
        out_specs=pl.BlockSpec((tm, d), lambda i: (i, 0)),
        out_shape=jax.ShapeDtypeStruct((nt, d), F32),
        compiler_params=_params("parallel"),
        name="ffn",
    )(x, g.reshape(1, d), wg_bf16, wu_bf16, wd_bf16)


def _route_body(x_ref, g_ref, wr_ref, idx_ref, gate_ref, *, n_exp):
    h = _rms(x_ref[...], g_ref[...])
    logits = _dot_nt_f32(h, wr_ref[...])
    lane_i = lax.broadcasted_iota(jnp.int32, logits.shape, 1)
    lane = lane_i.astype(F32)
    lg = jnp.where(lane_i < n_exp, logits, -jnp.inf)
    m1 = jnp.max(lg, axis=-1, keepdims=True)
    i1 = jnp.min(jnp.where(lg == m1, lane, float(LANES)), axis=-1, keepdims=True)
    lg2 = jnp.where(lane == i1, -jnp.inf, lg)
    m2 = jnp.max(lg2, axis=-1, keepdims=True)
    i2 = jnp.min(jnp.where(lg2 == m2, lane, float(LANES)), axis=-1, keepdims=True)
    e2 = jnp.exp(m2 - m1)
    idx_ref[...] = jnp.where(lane_i == 0, i1, jnp.where(lane_i == 1, i2, 0.0)).astype(jnp.int32)
    gate_ref[...] = jnp.where(lane_i == 0, 1.0 / (1.0 + e2), jnp.where(lane_i == 1, e2 / (1.0 + e2), 0.0))


def _route(x, g, w_router):
    nt, d = x.shape
    n_exp = w_router.shape[1]
    assert n_exp <= LANES
    wr = jnp.pad(w_router.T, ((0, LANES - n_exp), (0, 0)))
    tm = ROW_TILE
    row = lambda i: (i, 0)
    return pl.pallas_call(
        functools.partial(_route_body, n_exp=n_exp),
        grid=(nt // tm,),
        in_specs=[pl.BlockSpec((tm, d), row), pl.BlockSpec((1, d), lambda i: (0, 0)),
                  pl.BlockSpec((LANES, d), lambda i: (0, 0))],
        out_specs=[pl.BlockSpec((tm, LANES), row)] * 2,
        out_shape=[jax.ShapeDtypeStruct((nt, LANES), jnp.int32), jax.ShapeDtypeStruct((nt, LANES), F32)],
        compiler_params=_params("parallel"),
        name="route",
    )(x, g.reshape(1, d), wr)


def _expert_body(be_ref, src_ref, nu_ref, x_hbm, g_ref, wg_ref, wu_ref, wd_ref, o_ref, xbuf, sem):
    del be_ref
    i = pl.program_id(0)
    nb = pl.num_programs(0)
    used = i < nu_ref[0]
    tm = xbuf.shape[1]
    dff = wg_ref.shape[1]
    tf = MOE_FF_TILE

    def row_copy(block, r, slot):
        tok = src_ref[block * tm + r]
        return pltpu.make_async_copy(x_hbm.at[pl.ds(tok, 1), :], xbuf.at[slot, pl.ds(r, 1), :], sem.at[slot])

    def wait_block(slot):
        def body(r, c):
            row_copy(0, r, slot).wait()
            return c
        lax.fori_loop(0, tm, body, 0, unroll=8)

    @pl.when(i == 0)
    def _():
        def body(r, c):
            row_copy(0, r, 0).start()
            row_copy(jnp.minimum(1, nb - 1), r, 1).start()
            return c
        lax.fori_loop(0, tm, body, 0, unroll=8)

    wait_block(i % 3)
    nxt = jnp.minimum(i + 2, nb - 1)
    nxt_slot = (i + 2) % 3

    @pl.when(jnp.logical_not(used))
    def _():
        def body(r, c):
            row_copy(nxt, r, nxt_slot).start()
            return c
        lax.fori_loop(0, tm, body, 0, unroll=8)
        o_ref[...] = jnp.zeros_like(o_ref)

    @pl.when(used)
    def _():
        for r in range(tm):
            row_copy(nxt, r, nxt_slot).start(priority=1)
        h = _rms(xbuf[i % 3], g_ref[...]).astype(BF16)
        acc = jnp.zeros(o_ref.shape, F32)
        for c in range(dff // tf):
            cols = slice(c * tf, (c + 1) * tf)
            a = _silu(_dot(h, wg_ref[:, cols])) * _dot(h, wu_ref[:, cols])
            acc = acc + _dot(a.astype(BF16), wd_ref[cols, :])
        o_ref[...] = acc

    @pl.when(i == nb - 1)
    def _():
        wait_block((i + 1) % 3)
        wait_block((i + 2) % 3)


def _experts(x, src_tok, g, n_used, block_expert, wg_bf16, wu_bf16, wd_bf16):
    r = src_tok.shape[0]
    d = x.shape[1]
    dff = wg_bf16.shape[2]
    tm = MOE_ROWS
    assert dff % MOE_FF_TILE == 0
    resident = pl.Buffered(1)
    wmap = lambda i, be, st, nu: (be[i], 0, 0)
    return pl.pallas_call(
        _expert_body,
        grid_spec=pltpu.PrefetchScalarGridSpec(
            num_scalar_prefetch=3,
            grid=(r // tm,),
            in_specs=[pl.BlockSpec(memory_space=pl.ANY),
                      pl.BlockSpec((1, d), lambda i, be, st, nu: (0, 0)),
                      pl.BlockSpec((None, d, dff), wmap, pipeline_mode=resident),
                      pl.BlockSpec((None, d, dff), wmap, pipeline_mode=resident),
                      pl.BlockSpec((None, dff, d), wmap, pipeline_mode=resident)],
            out_specs=pl.BlockSpec((tm, d), lambda i, be, st, nu: (i, 0)),
            scratch_shapes=[pltpu.VMEM((3, tm, d), F32), pltpu.SemaphoreType.DMA((3,))],
        ),
        out_shape=jax.ShapeDtypeStruct((r, d), F32),
        compiler_params=_params("arbitrary"),
        name="experts",
    )(block_expert, src_tok, n_used, x, g.reshape(1, d), wg_bf16, wu_bf16, wd_bf16)


def _combine_body(d0_ref, d1_ref, x_ref, gate_ref, g_ref, src_ref, *rest, n_prompt_tiles):
    i = pl.program_id(0)
    n = pl.num_programs(0)
    buf, sem = rest[-2:]

    def copies(block, r, slot):
        t = block * GATHER_ROWS + r
        return (pltpu.make_async_copy(src_ref.at[pl.ds(d0_ref[t], 1), :], buf.at[slot, 0, pl.ds(r, 1), :],
                                      sem.at[slot]),
                pltpu.make_async_copy(src_ref.at[pl.ds(d1_ref[t], 1), :], buf.at[slot, 1, pl.ds(r, 1), :],
                                      sem.at[slot]))

    def issue(block, slot):
        def body(r, c):
            for cp in copies(block, r, slot):
                cp.start()
            return c
        lax.fori_loop(0, GATHER_ROWS, body, 0, unroll=4)

    @pl.when(i == 0)
    def _():
        issue(0, 0)

    @pl.when(i + 1 < n)
    def _():
        issue(i + 1, (i + 1) % 2)

    def wait(r, c):
        for cp in copies(0, r, i % 2):
            cp.wait()
        return c

    lax.fori_loop(0, GATHER_ROWS, wait, 0, unroll=4)
    gate = gate_ref[...]
    y = x_ref[...] + gate[:, 0:1] * buf[i % 2, 0] + gate[:, 1:2] * buf[i % 2, 1]
    if n_prompt_tiles is None:
        rest[0][...] = y
    else:
        y = _rms(y, g_ref[...])

        @pl.when(i < n_prompt_tiles)
        def _():
            rest[0][...] = y

        @pl.when(i >= n_prompt_tiles)
        def _():
            rest[1][...] = y


def _combine(x, gate, expert_rows, d0, d1, final_g=None, n_prompt=None):
    nt, d = x.shape
    tm = GATHER_ROWS
    row = lambda i, a, b: (i, 0)
    if final_g is None:
        g = jnp.ones((d,), F32)
        npt = None
        out_specs = pl.BlockSpec((tm, d), row)
        out_shape = jax.ShapeDtypeStruct((nt, d), F32)
    else:
        g = final_g
        npt = n_prompt // tm
        out_specs = [pl.BlockSpec((tm, d), lambda i, a, b: (jnp.minimum(i, npt - 1), 0)),
                     pl.BlockSpec((tm, d), lambda i, a, b: (jnp.maximum(i - npt, 0), 0))]
        out_shape = [jax.ShapeDtypeStruct((n_prompt, d), F32), jax.ShapeDtypeStruct((nt - n_prompt, d), F32)]
    return pl.pallas_call(
        functools.partial(_combine_body, n_prompt_tiles=npt),
        grid_spec=pltpu.PrefetchScalarGridSpec(
            num_scalar_prefetch=2,
            grid=(nt // tm,),
            in_specs=[pl.BlockSpec((tm, d), row),
                      pl.BlockSpec((tm, LANES), row),
                      pl.BlockSpec((1, d), lambda i, a, b: (0, 0)),
                      pl.BlockSpec(memory_space=pl.ANY)],
            out_specs=out_specs,
            scratch_shapes=[pltpu.VMEM((2, 2, tm, d), F32), pltpu.SemaphoreType.DMA((2,))],
        ),
        out_shape=out_shape,
        compiler_params=_params("arbitrary"),
        name="combine",
    )(d0, d1, x, gate, g.reshape(1, d), expert_rows)


def _moe(x, g, w_router, wg_bf16, wu_bf16, wd_bf16, final_g=None, n_prompt=None):
    nt, d = x.shape
    n_exp = w_router.shape[1]
    tm = MOE_ROWS
    idx, gate = _route(x, g, w_router)
    e_flat = idx[:, :TOP_K].reshape(-1)
    n_assign = nt * TOP_K
    onehot = (e_flat[:, None] == jnp.arange(n_exp, dtype=jnp.int32)[None, :]).astype(jnp.int32)
    rank = jnp.cumsum(onehot, axis=0) - onehot
    pos = jnp.sum(rank * onehot, axis=1)
    counts = jnp.sum(onehot, axis=0)
    padded = (counts + tm - 1) // tm * tm
    pend = jnp.cumsum(padded)
    dest = ((pend - padded)[e_flat] + pos).astype(jnp.int32)
    n_rows = (n_assign // tm + n_exp) * tm
    tok = jnp.arange(n_assign, dtype=jnp.int32) // TOP_K
    src_tok = jnp.zeros((n_rows,), jnp.int32).at[dest].set(tok, unique_indices=True)
    block_expert = jnp.minimum(
        jnp.searchsorted(pend, jnp.arange(n_rows // tm, dtype=jnp.int32) * tm, side='right'), n_exp - 1
    ).astype(jnp.int32)
    n_used = (pend[n_exp - 1:] // tm).astype(jnp.int32)
    y_rows = _experts(x, src_tok, g, n_used, block_expert, wg_bf16, wu_bf16, wd_bf16)
    dest2 = dest.reshape(nt, TOP_K)
    return _combine(x, gate, y_rows, dest2[:, 0], dest2[:, 1], final_g, n_prompt)


def _norm_body(x_ref, g_ref, o_ref):
    o_ref[...] = _rms(x_ref[...], g_ref[...])


def _final_norm(x, g):
    nt, d = x.shape
    tm = ROW_TILE
    return pl.pallas_call(
        _norm_body,
        grid=(nt // tm,),
        in_specs=[pl.BlockSpec((tm, d), lambda i: (i, 0)), pl.BlockSpec((1, d), lambda i: (0, 0))],
        out_specs=pl.BlockSpec((tm, d), lambda i: (i, 0)),
        out_shape=jax.ShapeDtypeStruct((nt, d), F32),
        compiler_params=_params("parallel"),
        name="final_norm",
    )(x, g.reshape(1, d))


def kernel(x_prompt, x_sample, cache_k, cache_v, state_conv, page_table, rel_bias, norm_mix, w_in, w_gate,
           w_sgu, b_sgu, g_sgu, w_conv, w_branch, w_out, norm_ffn, w_ff_gate, w_ff_up, w_ff_down,
           w_router, w_exp_gate, w_exp_up, w_exp_down, norm_final):
    batch, seq, d = x_prompt.shape
    n_seq, t_new, _ = x_sample.shape
    depth = w_in.shape[0]
    n_p = batch * seq
    n_s = n_seq * t_new
    assert t_new == T_NEW and seq % MOBA_BLOCK == 0 and seq // MOBA_BLOCK <= LANES
    assert n_p % FFN_ROWS == 0 and n_s % ROW_TILE == 0 and (n_p + n_s) % FFN_ROWS == 0
    n_kb = seq // MOBA_BLOCK
    group_w = SGU_WIDTH // SGU_GROUPS

    x = jnp.concatenate([x_prompt.reshape(n_p, d), x_sample.reshape(n_s, d)], axis=0)
    bt = rel_bias.T.astype(F32)
    moba_tiles = _moba_bias_tiles(bt)
    far_bias = bt[:, REL_BUCKETS - 1]

    outs = {k: [] for k in ("kp", "vp", "ks", "vs", "cp", "cs", "sv")}
    for l in range(depth):
        q, k_s, v_s, kt, vt, uv, bch, kb, vb, km = _project(x, norm_mix[l], w_in[l].astype(BF16), batch, seq)

        kmean = jnp.pad(km[:n_p // MOBA_BLOCK, 0, :].reshape(batch, n_kb, QKV), ((0, 0), (0, LANES - n_kb), (0, 0)))
        o_a = _moba_prompt(q, kb, vb, kmean, moba_tiles, far_bias, batch, seq)
        o_c = _sb_prompt(q, kb, vb, batch, seq)
        ws = jnp.tril(w_sgu[l]).astype(BF16)
        bs = jnp.repeat(b_sgu[l].T, group_w, axis=1)
        o_b, o_d, z_p, _ = _mix(uv, bch, None, ws, bs, g_sgu[l], w_conv[l], seq // SGU_CHUNK, n_p)

        o_att = _attn_sample(q[n_p:], k_s, v_s, cache_k, cache_v, l, page_table, bt)
        per_chunk = SGU_CHUNK // T_NEW
        w4 = jnp.tril(w_sgu[l][:, :T_NEW, :T_NEW])
        ws_s = jnp.einsum('ab,gts->gatbs', jnp.eye(per_chunk, dtype=F32), w4)
        ws_s = ws_s.reshape(SGU_GROUPS, SGU_CHUNK, SGU_CHUNK).astype(BF16)
        bs_s = jnp.tile(jnp.repeat(b_sgu[l][:, :T_NEW].T, group_w, axis=1), (per_chunk, 1))
        prev = state_conv[l]
        zero = jnp.zeros((n_seq, 1, CONV_DIM), F32)
        p1 = jnp.concatenate([prev[:, 1:2], zero, zero, zero], axis=1).reshape(n_s, CONV_DIM)
        p2 = jnp.concatenate([prev[:, 0:1], prev[:, 1:2], zero, zero], axis=1).reshape(n_s, CONV_DIM)
        s_b, s_d, z_s, vn_s = _mix(uv[n_p:], bch[n_p:], (p1, p2), ws_s, bs_s, g_sgu[l], w_conv[l], None, n_s)

        prompt_br = (o_a, o_b, o_c, o_d)
        sample_br = (o_att[:, :QKV // 2], s_b, o_att[:, QKV // 2:], s_d)
        x = _merge(x, norm_mix[l], prompt_br, sample_br, w_gate[l].astype(BF16), w_branch[l].astype(BF16),
                   w_out[l].astype(BF16), n_p)

        i = l // 2
        if l % 2 == 0:
            x = _ffn(x, norm_ffn[l], w_ff_gate[i].astype(BF16), w_ff_up[i].astype(BF16), w_ff_down[i].astype(BF16))
        else:
            x = _moe(x, norm_ffn[l], w_router[i], w_exp_gate[i].astype(BF16), w_exp_up[i].astype(BF16),
                     w_exp_down[i].astype(BF16), norm_final if l == depth - 1 else None, n_p)

        outs["kp"].append(kt.reshape(batch, N_HEADS, HEAD_DIM, seq).transpose(0, 3, 1, 2))
        outs["vp"].append(vt.reshape(batch, N_HEADS, HEAD_DIM, seq).transpose(0, 3, 1, 2))
        outs["ks"].append(k_s.reshape(n_seq, t_new, N_HEADS, HEAD_DIM))
        outs["vs"].append(v_s.reshape(n_seq, t_new, N_HEADS, HEAD_DIM))
        outs["cp"].append(z_p.reshape(batch, seq, CONV_DIM)[:, seq - (CONV_WIDTH - 1):])
        outs["cs"].append(z_s.reshape(n_seq, t_new, CONV_DIM)[:, t_new - (CONV_WIDTH - 1):])
        outs["sv"].append(vn_s.reshape(n_seq, t_new, SGU_WIDTH))

    if depth % 2 == 0:
        y_p, y_s = x
    else:
        y = _final_norm(x, norm_final)
        y_p, y_s = y[:n_p], y[n_p:]
    return (y_p.reshape(batch, seq, d), y_s.reshape(n_seq, t_new, d),
            jnp.stack(outs["kp"]), jnp.stack(outs["vp"]), jnp.stack(outs["ks"]), jnp.stack(outs["vs"]),
            jnp.stack(outs["cp"]), jnp.stack(outs["cs"]), jnp.stack(outs["sv"]))
```

```python
import functools
import math

import numpy as np
import jax
import jax.numpy as jnp
from jax import lax
from jax.experimental import pallas as pl
from jax.experimental.pallas import tpu as pltpu

F32 = jnp.float32
BF16 = jnp.bfloat16

NORM_EPS = 1e-6
HEAD_DIM = 64
N_HEADS = 8
QKV = N_HEADS * HEAD_DIM
MOBA_BLOCK = 256
MOBA_TOPK = 3
Q_BLOCK = 128
SGU_CHUNK = 128
SGU_WIDTH = 256
SGU_GROUPS = 4
CONV_DIM = 256
CONV_WIDTH = 3
N_BRANCH = 4
BRANCH_DIM = 256
PAGE = 128
REL_BUCKETS = 32
REL_MAX_DIST = 128
TOP_K = 2
ATT_SCALE = HEAD_DIM ** -0.5
MASKED = -1e30
SB_DROP = -105.0
LANES = 128
ROW_TILE = 256
MOE_ROWS = 512
MOE_FF_TILE = 896
FFN_ROWS = 512
FFN_FF_TILE = 256
GATHER_ROWS = 256
V7X_VMEM_LIMIT = 56 * 1024 * 1024


def _params(*sem):
    return pltpu.CompilerParams(dimension_semantics=sem, vmem_limit_bytes=V7X_VMEM_LIMIT)


def _dot(a, b):
    return jnp.dot(a, b, preferred_element_type=F32)


def _dot_nt(a, b):
    return lax.dot_general(a, b, (((1,), (1,)), ((), ())), preferred_element_type=F32)


def _split_bf16(x):
    hi = x.astype(BF16)
    lo = (x - hi.astype(F32)).astype(BF16)
    return hi, lo


def _dot_nt_f32(a, b):
    ah, al = _split_bf16(a)
    bh, bl = _split_bf16(b)
    return _dot_nt(ah, bh) + _dot_nt(ah, bl) + _dot_nt(al, bh)


def _dot_f32(a, b):
    ah, al = _split_bf16(a)
    bh, bl = _split_bf16(b)
    return _dot(ah, bh) + _dot(ah, bl) + _dot(al, bh)


def _rms(x, g):
    return x * lax.rsqrt(jnp.mean(x * x, axis=-1, keepdims=True) + NORM_EPS) * g


def _gelu_tanh(x):
    return 0.5 * x * (1.0 + jnp.tanh(math.sqrt(2.0 / math.pi) * (x + 0.044715 * (x * x * x))))


def _sigmoid(x):
    return 1.0 / (1.0 + jnp.exp(-x))


def _softplus(z):
    return jnp.maximum(z, 0.0) + jnp.log1p(jnp.exp(-jnp.abs(z)))


def _suffix_sum(lk, u):
    n = lk.shape[0]
    hi, lo = _split_bf16(lk)
    r = _dot(jnp.concatenate([hi, lo], axis=0), u)
    return r[:n] + r[n:]


def _strict_upper(n):
    j = np.arange(n)[:, None]
    s = np.arange(n)[None, :]
    return jnp.asarray((j > s).astype(np.float32), dtype=BF16)


def _t5_bucket_table(max_dist):
    d = np.arange(max_dist + 1)
    max_exact = REL_BUCKETS // 2
    large = max_exact + (np.log(np.maximum(d, 1).astype(np.float32) / np.float32(max_exact))
                         / np.float32(math.log(REL_MAX_DIST / max_exact))
                         * np.float32(REL_BUCKETS - max_exact)).astype(np.int32)
    large = np.minimum(large, REL_BUCKETS - 1)
    return np.where(d < max_exact, d, large).astype(np.int32)


def _bias_by_dist(bt_rows, dist, valid):
    table = _t5_bucket_table(2 * REL_MAX_DIST)
    b = jnp.broadcast_to(bt_rows[:, 0:1], dist.shape)
    for k in range(1, REL_BUCKETS):
        first = int(np.argmax(table >= k))
        b = jnp.where(dist >= first, bt_rows[:, k:k + 1], b)
    return jnp.where(valid, b, MASKED)


def _proj_body(x_ref, g_ref, w_ref, q_ref, ks_ref, vs_ref, kt_ref, vt_ref, uv_ref, bch_ref, kb_ref, vb_ref, km_ref,
               *, n_prompt_tiles):
    i = pl.program_id(0)
    h = _rms(x_ref[...], g_ref[...]).astype(BF16)

    def mm(lo, hi):
        return _dot(h, w_ref[:, lo:hi])

    q_ref[...] = mm(0, QKV)
    k = mm(QKV, 2 * QKV)
    kb_ref[...] = k.astype(BF16)
    km_ref[...] = jnp.broadcast_to(jnp.mean(k, axis=0, keepdims=True), km_ref.shape)
    v = mm(2 * QKV, 3 * QKV)
    vb_ref[...] = v.astype(BF16)
    uv_ref[...] = mm(3 * QKV, 3 * QKV + 2 * SGU_WIDTH)
    bch_ref[...] = mm(3 * QKV + 2 * SGU_WIDTH, 3 * QKV + 2 * SGU_WIDTH + 3 * CONV_DIM)

    @pl.when(i < n_prompt_tiles)
    def _():
        kt_ref[...] = k.T
        vt_ref[...] = v.T

    @pl.when(i >= n_prompt_tiles)
    def _():
        ks_ref[...] = k
        vs_ref[...] = v


def _project(x, g, w_bf16, batch, seq):
    nt, d = x.shape
    in_dim = w_bf16.shape[1]
    tm = MOBA_BLOCK
    nb = nt // tm
    per_seq = seq // tm
    npt = batch * per_seq
    row = lambda i: (i, 0)
    srow = lambda i: (jnp.maximum(i - npt, 0), 0)
    tcol = lambda i: (jnp.minimum(i, npt - 1) // per_seq, 0, jnp.minimum(i, npt - 1) % per_seq)
    return pl.pallas_call(
        functools.partial(_proj_body, n_prompt_tiles=npt),
        grid=(nb,),
        in_specs=[pl.BlockSpec((tm, d), row),
                  pl.BlockSpec((1, d), lambda i: (0, 0)),
                  pl.BlockSpec((d, in_dim), lambda i: (0, 0))],
        out_specs=[pl.BlockSpec((tm, QKV), row), pl.BlockSpec((tm, QKV), srow), pl.BlockSpec((tm, QKV), srow),
                   pl.BlockSpec((None, QKV, tm), tcol), pl.BlockSpec((None, QKV, tm), tcol),
                   pl.BlockSpec((tm, 2 * SGU_WIDTH), row), pl.BlockSpec((tm, 3 * CONV_DIM), row),
                   pl.BlockSpec((tm, QKV), row), pl.BlockSpec((tm, QKV), row),
                   pl.BlockSpec((None, 8, QKV), lambda i: (i, 0, 0))],
        out_shape=[jax.ShapeDtypeStruct((nt, QKV), F32),
                   jax.ShapeDtypeStruct((nt - npt * tm, QKV), F32), jax.ShapeDtypeStruct((nt - npt * tm, QKV), F32),
                   jax.ShapeDtypeStruct((batch, QKV, seq), F32), jax.ShapeDtypeStruct((batch, QKV, seq), F32),
                   jax.ShapeDtypeStruct((nt, 2 * SGU_WIDTH), F32), jax.ShapeDtypeStruct((nt, 3 * CONV_DIM), F32),
                   jax.ShapeDtypeStruct((nt, QKV), BF16), jax.ShapeDtypeStruct((nt, QKV), BF16),
                   jax.ShapeDtypeStruct((nb, 8, QKV), F32)],
        compiler_params=_params("arbitrary"),
        name="proj",
    )(x, g.reshape(1, d), w_bf16)


def _top_blocks(gate, lane, lane_mask):
    lane = lane.astype(F32)
    g = jnp.where(lane_mask, gate, -jnp.inf)
    picks = []
    for _ in range(MOBA_TOPK):
        m = jnp.max(g, axis=-1, keepdims=True)
        idx = jnp.min(jnp.where(g == m, lane, 1e9), axis=-1, keepdims=True)
        picks.append((idx, m > -jnp.inf))
        g = jnp.where(lane == idx, -jnp.inf, g)
    return picks


def _is_picked(picks, j):
    j = jnp.asarray(j).astype(F32)
    c = (picks[0][0] == j) & picks[0][1]
    for idx, ok in picks[1:]:
        c = c | ((idx == j) & ok)
    return c


def _moba_prompt_body(fb_ref, q_ref, k_ref, v_ref, km_ref, bias_ref, spread_ref, o_ref, *, chunk, gate_rows):
    cur = pl.program_id(1)
    rows = 2 * MOBA_BLOCK
    lane = lax.broadcasted_iota(jnp.int32, (MOBA_BLOCK, LANES), 1)
    first_head = lax.broadcasted_iota(jnp.int32, (rows, 1), 0) < MOBA_BLOCK
    state = [_moba_pair_start(pr, cur, lane, first_head, fb_ref, q_ref, k_ref, v_ref, km_ref, bias_ref, gate_rows)
             for pr in range(2)]

    def chunk_step(c, carry):
        start = pl.multiple_of(c * (chunk * MOBA_BLOCK), chunk * MOBA_BLOCK)
        out = []
        for pr, (m, l, acc) in enumerate(carry):
            q_and_mask, far_bias = state[pr][:2]
            cols = slice(pr * LANES, (pr + 1) * LANES)
            keys = jnp.concatenate([k_ref[pl.ds(start, chunk * MOBA_BLOCK), cols], spread_ref[c]], axis=1)
            s = _dot_nt(q_and_mask, keys) + far_bias
            m_new = jnp.maximum(m, jnp.max(s, axis=-1, keepdims=True))
            alpha = jnp.exp(m - m_new)
            p = jnp.exp(s - m_new)
            l = alpha * l + jnp.sum(p, axis=-1, keepdims=True)
            acc = alpha * acc + _dot(p.astype(BF16), v_ref[pl.ds(start, chunk * MOBA_BLOCK), cols])
            out.append((m_new, l, acc))
        return tuple(out)

    n_older = jnp.maximum(cur - 1, 0)
    final = lax.fori_loop(0, (n_older + chunk - 1) // chunk, chunk_step, tuple(st[2] for st in state))
    for pr, (_, l, acc) in enumerate(final):
        out = acc / l
        o_ref[:, pr * LANES:(pr + 1) * LANES] = jnp.where(lane < HEAD_DIM, out[:MOBA_BLOCK], out[MOBA_BLOCK:])


def _moba_pair_start(pr, cur, lane, first_head, fb_ref, q_ref, k_ref, v_ref, km_ref, bias_ref, gate_rows):
    rows = 2 * MOBA_BLOCK
    cols = slice(pr * LANES, (pr + 1) * LANES)
    q = q_ref[:, cols] * ATT_SCALE
    qs = jnp.concatenate([jnp.where(lane < HEAD_DIM, q, 0.0), jnp.where(lane >= HEAD_DIM, q, 0.0)], axis=0)
    qsb = qs.astype(BF16)
    far_bias = jnp.where(first_head, fb_ref[2 * pr], fb_ref[2 * pr + 1])

    blk = lax.broadcasted_iota(jnp.int32, (gate_rows, rows), 0)
    blk_f = blk.astype(F32)
    g = jnp.where(blk < cur, _dot_nt_f32(km_ref[0:gate_rows, cols], qs), -jnp.inf)
    picked = None
    for _ in range(MOBA_TOPK):
        best = jnp.max(g, axis=0, keepdims=True)
        hit = blk_f == jnp.min(jnp.where(g == best, blk_f, 1e9), axis=0, keepdims=True)
        chosen = hit & (best > -jnp.inf)
        picked = chosen if picked is None else picked | chosen
        g = jnp.where(hit, -jnp.inf, g)
    fill = jnp.full((LANES // 2 - gate_rows, rows), MASKED, F32)
    older = jnp.where(picked & (blk < cur - 1), 0.0, MASKED)
    prev = jnp.where(picked & (blk == cur - 1), 0.0, MASKED)
    add_rows = jnp.concatenate([older, fill, prev, fill], axis=0).T.astype(BF16)

    start = pl.multiple_of(cur * MOBA_BLOCK, MOBA_BLOCK)
    start_p = pl.multiple_of(jnp.maximum(cur - 1, 0) * MOBA_BLOCK, MOBA_BLOCK)
    q_and_mask = jnp.concatenate([qsb, add_rows], axis=1)
    spread_prev = lax.broadcasted_iota(jnp.int32, (MOBA_BLOCK, LANES), 1) == LANES // 2 + cur - 1
    keys_prev = jnp.concatenate([k_ref[pl.ds(start_p, MOBA_BLOCK), cols], spread_prev.astype(BF16)], axis=1)
    s_own = _dot_nt(qsb, k_ref[pl.ds(start, MOBA_BLOCK), cols]) + bias_ref[pr, 0]
    s_prev = _dot_nt(q_and_mask, keys_prev) + bias_ref[pr, 1]
    m = jnp.maximum(jnp.max(s_own, axis=-1, keepdims=True), jnp.max(s_prev, axis=-1, keepdims=True))
    p_own = jnp.exp(s_own - m)
    p_prev = jnp.exp(s_prev - m)
    l = jnp.sum(p_own, axis=-1, keepdims=True) + jnp.sum(p_prev, axis=-1, keepdims=True)
    acc = (_dot(p_own.astype(BF16), v_ref[pl.ds(start, MOBA_BLOCK), cols])
           + _dot(p_prev.astype(BF16), v_ref[pl.ds(start_p, MOBA_BLOCK), cols]))
    return q_and_mask, far_bias, (m, l, acc)


def _moba_bias_tiles(bt):
    n_pairs = bt.shape[0] // 2
    shape = (n_pairs * 2 * MOBA_BLOCK, MOBA_BLOCK)
    i = lax.broadcasted_iota(jnp.int32, shape, 0) & (MOBA_BLOCK - 1)
    c = lax.broadcasted_iota(jnp.int32, shape, 1)
    bt_rows = jnp.repeat(bt, MOBA_BLOCK, axis=0)
    tiles = []
    for delta in (0, MOBA_BLOCK):
        dist = delta + i - c
        tiles.append(_bias_by_dist(bt_rows, dist, dist >= 0).reshape(n_pairs, 2 * MOBA_BLOCK, MOBA_BLOCK))
    return jnp.stack(tiles, axis=1)


def _moba_prompt(q, kb, vb, kmean, bias_tiles, far_bias, batch, seq):
    n_kb = seq // MOBA_BLOCK
    nq = n_kb
    chunk = next(c for c in (4, 2, 1) if n_kb % c == 0)
    gate_rows = -(-n_kb // 8) * 8
    assert gate_rows <= LANES // 2
    blk = np.arange(LANES)[None, None, :]
    key = np.arange(chunk * MOBA_BLOCK)[None, :, None]
    cid = np.arange(n_kb // chunk)[:, None, None]
    spread = jnp.asarray((blk == cid * chunk + key // MOBA_BLOCK).astype(np.float32), dtype=BF16)
    return pl.pallas_call(
        functools.partial(_moba_prompt_body, chunk=chunk, gate_rows=gate_rows),
        grid_spec=pltpu.PrefetchScalarGridSpec(
            num_scalar_prefetch=1,
            grid=(batch, nq),
            in_specs=[pl.BlockSpec((MOBA_BLOCK, 2 * LANES), lambda b, i, fb: (b * nq + i, 0)),
                      pl.BlockSpec((seq, 2 * LANES), lambda b, i, fb: (b, 0)),
                      pl.BlockSpec((seq, 2 * LANES), lambda b, i, fb: (b, 0)),
                      pl.BlockSpec((None, LANES, 2 * LANES), lambda b, i, fb: (b, 0, 0)),
                      pl.BlockSpec(bias_tiles.shape, lambda b, i, fb: (0, 0, 0, 0)),
                      pl.BlockSpec(spread.shape, lambda b, i, fb: (0, 0, 0))],
            out_specs=pl.BlockSpec((MOBA_BLOCK, 2 * LANES), lambda b, i, fb: (b * nq + i, 0)),
        ),
        out_shape=jax.ShapeDtypeStruct((batch * seq, 2 * LANES), F32),
        compiler_params=_params("parallel", "arbitrary"),
        name="moba_prompt",
    )(far_bias, q, kb, vb, kmean, bias_tiles, spread)


def _sb_prompt_body(q_ref, k_ref, v_ref, u_ref, o_ref):
    qi = pl.program_id(1)
    rows = 2 * Q_BLOCK
    lane = lax.broadcasted_iota(jnp.int32, (Q_BLOCK, LANES), 1)
    u = u_ref[...]
    col = lax.broadcasted_iota(jnp.int32, (rows, Q_BLOCK), 1)
    qrow = lax.broadcasted_iota(jnp.int32, (rows, Q_BLOCK), 0) & (Q_BLOCK - 1)

    def make_tile(pr):
        cols = slice(pr * LANES, (pr + 1) * LANES)
        q = q_ref[:, cols] * ATT_SCALE
        qsb = jnp.concatenate([jnp.where(lane < HEAD_DIM, q, 0.0), jnp.where(lane >= HEAD_DIM, q, 0.0)],
                              axis=0).astype(BF16)

        def tile(j, keep, r_acc, acc):
            start = pl.multiple_of(jnp.maximum(j, 0) * Q_BLOCK, Q_BLOCK)
            z = _dot_nt(qsb, k_ref[pl.ds(start, Q_BLOCK), cols])
            sp = _softplus(z)
            lk = -sp if keep is None else jnp.where(keep, -sp, 0.0)
            a = jnp.exp(z - sp + _suffix_sum(lk, u) + r_acc)
            if keep is not None:
                a = jnp.where(keep, a, 0.0)
            acc = acc + _dot(a.astype(BF16), v_ref[pl.ds(start, Q_BLOCK), cols])
            return r_acc + jnp.sum(lk, axis=-1, keepdims=True), acc

        return tile

    def exists(j):
        return jnp.broadcast_to(j >= 0, (rows, Q_BLOCK))

    tiles = [make_tile(0), make_tile(1)]
    states = []
    for tile in tiles:
        r_acc, acc = tile(qi, col < qrow, jnp.zeros((rows, 1), F32), jnp.zeros((rows, LANES), F32))
        r_acc, acc = tile(qi - 1, exists(qi - 1), r_acc, acc)
        states.append(tile(qi - 2, exists(qi - 2), r_acc, acc))

    def more(state):
        it, go, _, _ = state
        return (qi - 3 - 2 * it >= 0) & (go > 0)

    for pr, (tile, (r_acc, acc)) in enumerate(zip(tiles, states)):
        def older(state, tile=tile):
            it, _, r_acc, acc = state
            j = qi - 3 - 2 * it
            r_acc, acc = tile(j, None, r_acc, acc)
            r_acc, acc = tile(j - 1, exists(j - 1), r_acc, acc)
            return it + 1, (jnp.max(r_acc) > SB_DROP).astype(jnp.int32), r_acc, acc

        go = (jnp.max(r_acc) > SB_DROP).astype(jnp.int32)
        _, _, _, acc = lax.while_loop(more, older, (jnp.int32(0), go, r_acc, acc))
        o_ref[:, pr * LANES:(pr + 1) * LANES] = jnp.where(lane < HEAD_DIM, acc[:Q_BLOCK], acc[Q_BLOCK:])


def _sb_prompt(q, kb, vb, batch, seq):
    nq = seq // Q_BLOCK
    return pl.pallas_call(
        _sb_prompt_body,
        grid=(batch, nq),
        in_specs=[pl.BlockSpec((Q_BLOCK, 2 * LANES), lambda b, i: (b * nq + i, 1)),
                  pl.BlockSpec((seq, 2 * LANES), lambda b, i: (b, 1)),
                  pl.BlockSpec((seq, 2 * LANES), lambda b, i: (b, 1)),
                  pl.BlockSpec((Q_BLOCK, Q_BLOCK), lambda b, i: (0, 0))],
        out_specs=pl.BlockSpec((Q_BLOCK, 2 * LANES), lambda b, i: (b * nq + i, 0)),
        out_shape=jax.ShapeDtypeStruct((batch * seq, 2 * LANES), F32),
        compiler_params=_params("parallel", "arbitrary"),
        name="sb_prompt",
    )(q, kb, vb, _strict_upper(Q_BLOCK))


T_NEW = 4
SOFTMAX_ROWS = 4 * T_NEW


SB_STATIC_PAGES = 3
HALF = QKV // 2


def _own_head_fold(o):
    row_head = lax.broadcasted_iota(jnp.int32, o.shape, 0) >> 2
    col_head = lax.broadcasted_iota(jnp.int32, o.shape, 1) >> 6
    o = jnp.where(row_head == col_head, o, 0.0)
    out = o[:, 0:HEAD_DIM]
    for h in range(1, HALF // HEAD_DIM):
        out = out + o[:, h * HEAD_DIM:(h + 1) * HEAD_DIM]
    return out


def _attn_sample_body(pt_ref, qa_ref, qb_ref, knew_ref, vnew_ref, bias_ref, bias_new_ref, u_ref, ck_ref, cv_ref,
                      *rest, n_pages, n_static, layer):
    ka_refs = rest[:n_pages]
    va_refs = rest[n_pages:2 * n_pages]
    kb_refs = rest[2 * n_pages:2 * n_pages + n_static]
    vb_refs = rest[2 * n_pages + n_static:2 * n_pages + 2 * n_static]
    o_ref, kbuf, vbuf, sem = rest[2 * n_pages + 2 * n_static:]
    seq = pl.program_id(0)
    n_blocks = n_pages // 2
    sr = SOFTMAX_ROWS
    past = n_pages * PAGE

    qa = qa_ref[...]
    qab = qa.astype(BF16)
    qbb = qb_ref[...].astype(BF16)

    blk_lane = lax.broadcasted_iota(jnp.int32, (HALF, LANES), 1)
    km = jnp.zeros((HALF, LANES), F32)
    parts = []
    for p in range(n_pages):
        kp = ka_refs[p][...].reshape(HALF, PAGE)
        parts.append(_dot(qab, kp.astype(BF16)))
        if p % 2 == 0:
            ksum = kp
        else:
            mean = jnp.sum(ksum + kp, axis=1, keepdims=True) * (1.0 / MOBA_BLOCK)
            km = jnp.where(blk_lane == p // 2, mean, km)
    s = jnp.concatenate(parts, axis=1)

    pad_rows = jnp.zeros((LANES - 8, QKV), F32)
    knew = jnp.concatenate([knew_ref[...], pad_rows], axis=0).astype(BF16)
    vnew = jnp.concatenate([vnew_ref[...], pad_rows], axis=0).astype(BF16)

    lane_s = lax.broadcasted_iota(jnp.int32, (sr, LANES), 1)
    picks = _top_blocks(_dot_f32(qa, km), lane_s, lane_s < n_blocks)
    key_block = (lax.broadcasted_iota(jnp.int32, (sr, past), 1) >> 8).astype(F32)
    la = jnp.where(_is_picked(picks, key_block), s + bias_ref[...], MASKED)
    la_new = _dot_nt(qab, knew[:, :HALF]) + bias_new_ref[...]
    m = jnp.maximum(jnp.max(la, axis=-1, keepdims=True), jnp.max(la_new, axis=-1, keepdims=True))
    pa = jnp.exp(la - m)
    pa_new = jnp.exp(la_new - m)
    denom = jnp.sum(pa, axis=-1, keepdims=True) + jnp.sum(pa_new, axis=-1, keepdims=True)
    pa = pa.astype(BF16)
    o_a = _dot(pa_new.astype(BF16), vnew[:, :HALF])
    for p in range(n_pages):
        o_a = o_a + _dot_nt(pa[:, p * PAGE:(p + 1) * PAGE], va_refs[p][...].reshape(HALF, PAGE).astype(BF16))

    u = u_ref[...]

    def sticks(zb, r_acc, keep=None):
        sp = _softplus(zb)
        lk = -sp if keep is None else jnp.where(keep, -sp, 0.0)
        a = jnp.exp(zb - sp + _suffix_sum(lk, u) + r_acc)
        if keep is not None:
            a = jnp.where(keep, a, 0.0)
        return a.astype(BF16), r_acc + jnp.sum(lk, axis=-1, keepdims=True)

    keep_new = lane_s < (lax.broadcasted_iota(jnp.int32, (sr, LANES), 0) & (T_NEW - 1))
    zs = [_dot_nt(qbb, knew[:, HALF:])] + [_dot(qbb, kb_refs[t][...].reshape(HALF, PAGE).astype(BF16))
                                          for t in range(n_static)]
    sps = [_softplus(z) for z in zs]
    lks = [jnp.where(keep_new, -sps[0], 0.0)] + [-sp for sp in sps[1:]]
    within = _suffix_sum(jnp.concatenate(lks, axis=0), u)
    r_acc = jnp.zeros((sr, 1), F32)
    o_b = None
    for t, (z, sp, lk) in enumerate(zip(zs, sps, lks)):
        a = jnp.exp(z - sp + within[t * sr:(t + 1) * sr] + r_acc)
        r_acc = r_acc + jnp.sum(lk, axis=-1, keepdims=True)
        if t == 0:
            o_b = _dot(jnp.where(keep_new, a, 0.0).astype(BF16), vnew[:, HALF:])
        else:
            o_b = o_b + _dot_nt(a.astype(BF16), vb_refs[t - 1][...].reshape(HALF, PAGE).astype(BF16))

    def more(state):
        p, go, _, _ = state
        return (p >= 0) & (go > 0)

    def older(state):
        p, _, r_acc, o_b = state
        page = pt_ref[seq * n_pages + p]
        copies = (pltpu.make_async_copy(ck_ref.at[layer, page, pl.ds(N_HEADS // 2, N_HEADS // 2)], kbuf, sem.at[0]),
                  pltpu.make_async_copy(cv_ref.at[layer, page, pl.ds(N_HEADS // 2, N_HEADS // 2)], vbuf, sem.at[1]))
        for cp in copies:
            cp.start()
        for cp in copies:
            cp.wait()
        a, r_acc = sticks(_dot(qbb, kbuf[...].reshape(HALF, PAGE).astype(BF16)), r_acc)
        o_b = o_b + _dot_nt(a, vbuf[...].reshape(HALF, PAGE).astype(BF16))
        return p - 1, (jnp.max(r_acc) > SB_DROP).astype(jnp.int32), r_acc, o_b

    go = (jnp.max(r_acc) > SB_DROP).astype(jnp.int32)
    _, _, _, o_b = lax.while_loop(more, older, (jnp.int32(n_pages - n_static - 1), go, r_acc, o_b))

    o_ref[...] = jnp.concatenate([_own_head_fold(o_a) * (1.0 / denom), _own_head_fold(o_b)], axis=0)


def _attn_sample(q_s, k_s, v_s, cache_k, cache_v, layer, page_table, bt):
    n_seq, n_pages = page_table.shape
    assert q_s.shape[0] == n_seq * T_NEW and n_pages % 2 == 0
    past = n_pages * PAGE
    n_rows = N_HEADS * T_NEW
    n_static = min(SB_STATIC_PAGES, n_pages)
    half_heads = N_HEADS // 2
    q4 = (q_s * ATT_SCALE).reshape(n_seq, T_NEW, 2, half_heads, 1, HEAD_DIM).transpose(2, 0, 3, 1, 4, 5)
    eye = jnp.eye(half_heads, dtype=F32).reshape(1, 1, half_heads, 1, half_heads, 1)
    qx = (q4 * eye).reshape(2, n_seq, SOFTMAX_ROWS, HALF)
    pad = ((0, 0), (0, 8 - T_NEW), (0, 0))
    knew = jnp.pad(k_s.reshape(n_seq, T_NEW, QKV), pad)
    vnew = jnp.pad(v_s.reshape(n_seq, T_NEW, QKV), pad)
    ck = cache_k.transpose(0, 1, 3, 4, 2)
    cv = cache_v.transpose(0, 1, 3, 4, 2)

    bt_rows = jnp.repeat(bt[:4], T_NEW, axis=0)

    def key_bias(width, first_key_dist, new_keys):
        r = lax.broadcasted_iota(jnp.int32, (SOFTMAX_ROWS, width), 0)
        c = lax.broadcasted_iota(jnp.int32, (SOFTMAX_ROWS, width), 1)
        dist = first_key_dist + (r & (T_NEW - 1)) - c
        valid = (dist >= 0) & (c < T_NEW) if new_keys else dist >= 0
        return _bias_by_dist(bt_rows, dist, valid)

    bias_past = key_bias(past, past, False)
    bias_new = key_bias(LANES, 0, True)

    def page_spec(p, head_group):
        return pl.BlockSpec((None, None, half_heads, HEAD_DIM, PAGE),
                            lambda b, pt, p=p: (layer, pt[b * n_pages + p], head_group, 0, 0))

    softmax_pages = [page_spec(p, 0) for p in range(n_pages)]
    stick_pages = [page_spec(n_pages - 1 - t, 1) for t in range(n_static)]
    seq3 = lambda b, pt: (b, 0, 0)
    const2 = lambda b, pt: (0, 0)
    out = pl.pallas_call(
        functools.partial(_attn_sample_body, n_pages=n_pages, n_static=n_static, layer=layer),
        grid_spec=pltpu.PrefetchScalarGridSpec(
            num_scalar_prefetch=1,
            grid=(n_seq,),
            in_specs=[pl.BlockSpec((None, SOFTMAX_ROWS, HALF), seq3),
                      pl.BlockSpec((None, SOFTMAX_ROWS, HALF), seq3),
                      pl.BlockSpec((None, 8, QKV), seq3),
                      pl.BlockSpec((None, 8, QKV), seq3),
                      pl.BlockSpec((SOFTMAX_ROWS, past), const2),
                      pl.BlockSpec((SOFTMAX_ROWS, LANES), const2),
                      pl.BlockSpec((PAGE, PAGE), const2),
                      pl.BlockSpec(memory_space=pl.ANY),
                      pl.BlockSpec(memory_space=pl.ANY)]
                     + softmax_pages * 2 + stick_pages * 2,
            out_specs=pl.BlockSpec((None, n_rows, HEAD_DIM), seq3),
            scratch_shapes=[pltpu.VMEM((half_heads, HEAD_DIM, PAGE), F32),
                            pltpu.VMEM((half_heads, HEAD_DIM, PAGE), F32),
                            pltpu.SemaphoreType.DMA((2,))],
        ),
        out_shape=jax.ShapeDtypeStruct((n_seq, n_rows, HEAD_DIM), F32),
        compiler_params=_params("arbitrary"),
        name="attn_sample",
    )(page_table.reshape(-1), qx[0], qx[1], knew, vnew, bias_past, bias_new, _strict_upper(PAGE), ck, cv,
      *([ck] * n_pages), *([cv] * n_pages), *([ck] * n_static), *([cv] * n_static))
    return out.reshape(n_seq, N_HEADS, T_NEW, HEAD_DIM).transpose(0, 2, 1, 3).reshape(n_seq * T_NEW, QKV)


def _mix_body(*refs, steps_per_seq):
    if steps_per_seq is None:
        uv_ref, bch_ref, p1_ref, p2_ref, ws_ref, bs_ref, gs_ref, wc_ref, ob_ref, od_ref, z_ref, vn_ref = refs
    else:
        uv_ref, bch_ref, prev_ref, ws_ref, bs_ref, gs_ref, wc_ref, ob_ref, od_ref, z_ref, vn_ref = refs
    uv = uv_ref[...]
    x = _gelu_tanh(uv[:, SGU_WIDTH:])
    xc = x - jnp.mean(x, axis=-1, keepdims=True)
    vn = xc * lax.rsqrt(jnp.mean(xc * xc, axis=-1, keepdims=True) + NORM_EPS) * gs_ref[...]
    vn_ref[...] = vn
    vnb = vn.astype(BF16)
    n_rows = uv.shape[0]
    group = lax.broadcasted_iota(jnp.int32, (SGU_CHUNK, SGU_WIDTH), 1) >> 6
    gated = []
    for c in range(n_rows // SGU_CHUNK):
        vc = vnb[c * SGU_CHUNK:(c + 1) * SGU_CHUNK]
        s = jnp.zeros((SGU_CHUNK, SGU_WIDTH), F32)
        for g in range(SGU_GROUPS):
            s = jnp.where(group == g, _dot(ws_ref[g], vc), s)
        gated.append(s + bs_ref[...])
    ob_ref[...] = _gelu_tanh(uv[:, :SGU_WIDTH]) * jnp.concatenate(gated, axis=0)

    bch = bch_ref[...]
    z = bch[:, CONV_DIM:2 * CONV_DIM] * bch[:, 2 * CONV_DIM:]
    z_ref[...] = z
    row = lax.broadcasted_iota(jnp.int32, (n_rows, CONV_DIM), 0)
    r1 = pltpu.roll(z, 1, 0)
    r2 = pltpu.roll(z, 2, 0)
    if steps_per_seq is None:
        rr = row & (T_NEW - 1)
        z1 = jnp.where(rr >= 1, r1, p1_ref[...])
        z2 = jnp.where(rr >= 2, r2, p2_ref[...])
    else:
        pb = prev_ref[...]
        zp = pb[:, CONV_DIM:2 * CONV_DIM] * pb[:, 2 * CONV_DIM:]
        zp = jnp.where(pl.program_id(0) % steps_per_seq == 0, 0.0, zp)
        z1 = jnp.where(row >= 1, r1, zp[7:8, :])
        z2 = jnp.where(row >= 2, r2, jnp.where(row == 1, zp[7:8, :], zp[6:7, :]))
    wc = wc_ref[...]
    od_ref[...] = bch[:, :CONV_DIM] * (wc[0:1, :] * z2 + wc[1:2, :] * z1 + wc[2:3, :] * z)


def _mix(uv, bch, history, ws_bf16, bs, gs, wc, chunks_per_seq, n_rows):
    per_step = next(c for c in (4, 2, 1) if n_rows % (c * SGU_CHUNK) == 0
                    and (chunks_per_seq is None or chunks_per_seq % c == 0))
    tm = per_step * SGU_CHUNK
    row = lambda i: (i, 0)
    const = lambda i: (0, 0)
    if chunks_per_seq is None:
        steps_per_seq = None
        hist_specs = [pl.BlockSpec((tm, CONV_DIM), row)] * 2
        hist = list(history)
    else:
        steps_per_seq = chunks_per_seq // per_step
        sub = tm // 8
        hist_specs = [pl.BlockSpec((8, 3 * CONV_DIM), lambda i: (jnp.maximum(i * sub - 1, 0), 0))]
        hist = [bch]
    return pl.pallas_call(
        functools.partial(_mix_body, steps_per_seq=steps_per_seq),
        grid=(n_rows // tm,),
        in_specs=[pl.BlockSpec((tm, 2 * SGU_WIDTH), row), pl.BlockSpec((tm, 3 * CONV_DIM), row)]
                 + hist_specs
                 + [pl.BlockSpec((SGU_GROUPS, SGU_CHUNK, SGU_CHUNK), lambda i: (0, 0, 0)),
                    pl.BlockSpec((SGU_CHUNK, SGU_WIDTH), const),
                    pl.BlockSpec((1, SGU_WIDTH), const),
                    pl.BlockSpec((8, CONV_DIM), const)],
        out_specs=[pl.BlockSpec((tm, SGU_WIDTH), row)] * 4,
        out_shape=[jax.ShapeDtypeStruct((n_rows, SGU_WIDTH), F32)] * 4,
        compiler_params=_params("parallel"),
        name="mix_sample" if chunks_per_seq is None else "mix_prompt",
    )(uv, bch, *hist, ws_bf16, bs, gs.reshape(1, SGU_WIDTH), jnp.pad(wc, ((0, 8 - CONV_WIDTH), (0, 0))))


def _merge_body(x_ref, g_ref, pa_ref, pb_ref, pc_ref, pd_ref, sa_ref, sb_ref, sc_ref, sd_ref,
                wg_ref, wb_ref, wo_ref, o_ref, *, n_prompt_tiles):
    x = x_ref[...]
    d = x.shape[1]
    h = _rms(x, g_ref[...]).astype(BF16)
    is_prompt = pl.program_id(0) < n_prompt_tiles
    merged = jnp.zeros(x.shape, F32)
    for n, (p_ref, s_ref) in enumerate(((pa_ref, sa_ref), (pb_ref, sb_ref), (pc_ref, sc_ref), (pd_ref, sd_ref))):
        br = jnp.where(is_prompt, p_ref[...], s_ref[...]).astype(BF16)
        up = _dot(br, wb_ref[n])
        gate = _sigmoid(_dot(h, wg_ref[:, n * d:(n + 1) * d]))
        merged = merged + gate * up
    o_ref[...] = x + _dot(merged.astype(BF16), wo_ref[...])


def _merge(x, g, prompt_br, sample_br, wg_bf16, wb_bf16, wo_bf16, n_prompt):
    nt, d = x.shape
    tm = ROW_TILE
    npt = n_prompt // tm
    row = lambda i: (i, 0)
    p_spec = pl.BlockSpec((tm, BRANCH_DIM), lambda i: (jnp.minimum(i, npt - 1), 0))
    s_spec = pl.BlockSpec((tm, BRANCH_DIM), lambda i: (jnp.maximum(i - npt, 0), 0))
    return pl.pallas_call(
        functools.partial(_merge_body, n_prompt_tiles=npt),
        grid=(nt // tm,),
        in_specs=[pl.BlockSpec((tm, d), row), pl.BlockSpec((1, d), lambda i: (0, 0))]
                 + [p_spec] * N_BRANCH + [s_spec] * N_BRANCH
                 + [pl.BlockSpec((d, N_BRANCH * d), lambda i: (0, 0)),
                    pl.BlockSpec((N_BRANCH, BRANCH_DIM, d), lambda i: (0, 0, 0)),
                    pl.BlockSpec((d, d), lambda i: (0, 0))],
        out_specs=pl.BlockSpec((tm, d), row),
        out_shape=jax.ShapeDtypeStruct((nt, d), F32),
        compiler_params=_params("parallel"),
        name="merge",
    )(x, g.reshape(1, d), *prompt_br, *sample_br, wg_bf16, wb_bf16, wo_bf16)


def _silu(x):
    return x * _sigmoid(x)


def _ffn_body(x_ref, g_ref, wg_ref, wu_ref, wd_ref, o_ref):
    x = x_ref[...]
    h = _rms(x, g_ref[...]).astype(BF16)
    dff = wg_ref.shape[1]
    acc = x
    for c in range(dff // FFN_FF_TILE):
        cols = slice(c * FFN_FF_TILE, (c + 1) * FFN_FF_TILE)
        a = _silu(_dot(h, wg_ref[:, cols])) * _dot(h, wu_ref[:, cols])
        acc = acc + _dot(a.astype(BF16), wd_ref[cols, :])
    o_ref[...] = acc


def _ffn(x, g, wg_bf16, wu_bf16, wd_bf16):
    nt, d = x.shape
    dff = wg_bf16.shape[1]
    tm = FFN_ROWS
    assert dff % FFN_FF_TILE == 0
    resident = pl.Buffered(1)
    return pl.pallas_call(
        _ffn_body,
        grid=(nt // tm,),
        in_specs=[pl.BlockSpec((tm, d), lambda i: (i, 0)), pl.BlockSpec((1, d), lambda i: (0, 0)),
                  pl.BlockSpec((d, dff), lambda i: (0, 0), pipeline_mode=resident),
                  pl.BlockSpec((d, dff), lambda i: (0, 0), pipeline_mode=resident),
                  pl.BlockSpec((dff, d), lambda i: (0, 0), pipeline_mode=resident)],
        out_specs=pl.BlockSpec((tm, d), lambda i: (i, 0)),
        out_shape=jax.ShapeDtypeStruct((nt, d), F32),
        compiler_params=_params("parallel"),
        name="ffn",
    )(x, g.reshape(1, d), wg_bf16, wu_bf16, wd_bf16)


def _route_body(x_ref, g_ref, wr_ref, idx_ref, gate_ref, *, n_exp):
    h = _rms(x_ref[...], g_ref[...])
    logits = _dot_nt_f32(h, wr_ref[...])
    lane_i = lax.broadcasted_iota(jnp.int32, logits.shape, 1)
    lane = lane_i.astype(F32)
    lg = jnp.where(lane_i < n_exp, logits, -jnp.inf)
    m1 = jnp.max(lg, axis=-1, keepdims=True)
    i1 = jnp.min(jnp.where(lg == m1, lane, float(LANES)), axis=-1, keepdims=True)
    lg2 = jnp.where(lane == i1, -jnp.inf, lg)
    m2 = jnp.max(lg2, axis=-1, keepdims=True)
    i2 = jnp.min(jnp.where(lg2 == m2, lane, float(LANES)), axis=-1, keepdims=True)
    e2 = jnp.exp(m2 - m1)
    idx_ref[...] = jnp.where(lane_i == 0, i1, jnp.where(lane_i == 1, i2, 0.0)).astype(jnp.int32)
    gate_ref[...] = jnp.where(lane_i == 0, 1.0 / (1.0 + e2), jnp.where(lane_i == 1, e2 / (1.0 + e2), 0.0))


def _route(x, g, w_router):
    nt, d = x.shape
    n_exp = w_router.shape[1]
    assert n_exp <= LANES
    wr = jnp.pad(w_router.T, ((0, LANES - n_exp), (0, 0)))
    tm = ROW_TILE
    row = lambda i: (i, 0)
    return pl.pallas_call(
        functools.partial(_route_body, n_exp=n_exp),
        grid=(nt // tm,),
        in_specs=[pl.BlockSpec((tm, d), row), pl.BlockSpec((1, d), lambda i: (0, 0)),
                  pl.BlockSpec((LANES, d), lambda i: (0, 0))],
        out_specs=[pl.BlockSpec((tm, LANES), row)] * 2,
        out_shape=[jax.ShapeDtypeStruct((nt, LANES), jnp.int32), jax.ShapeDtypeStruct((nt, LANES), F32)],
        compiler_params=_params("parallel"),
        name="route",
    )(x, g.reshape(1, d), wr)


def _expert_body(be_ref, src_ref, nu_ref, x_hbm, g_ref, wg_ref, wu_ref, wd_ref, o_ref, xbuf, sem):
    del be_ref
    i = pl.program_id(0)
    nb = pl.num_programs(0)
    used = i < nu_ref[0]
    tm = xbuf.shape[1]
    dff = wg_ref.shape[1]
    tf = MOE_FF_TILE

    def row_copy(block, r, slot):
        tok = src_ref[block * tm + r]
        return pltpu.make_async_copy(x_hbm.at[pl.ds(tok, 1), :], xbuf.at[slot, pl.ds(r, 1), :], sem.at[slot])

    def wait_block(slot):
        def body(r, c):
            row_copy(0, r, slot).wait()
            return c
        lax.fori_loop(0, tm, body, 0, unroll=8)

    @pl.when(i == 0)
    def _():
        def body(r, c):
            row_copy(0, r, 0).start()
            row_copy(jnp.minimum(1, nb - 1), r, 1).start()
            return c
        lax.fori_loop(0, tm, body, 0, unroll=8)

    wait_block(i % 3)
    nxt = jnp.minimum(i + 2, nb - 1)
    nxt_slot = (i + 2) % 3

    @pl.when(jnp.logical_not(used))
    def _():
        def body(r, c):
            row_copy(nxt, r, nxt_slot).start()
            return c
        lax.fori_loop(0, tm, body, 0, unroll=8)
        o_ref[...] = jnp.zeros_like(o_ref)

    @pl.when(used)
    def _():
        for r in range(tm):
            row_copy(nxt, r, nxt_slot).start(priority=1)
        h = _rms(xbuf[i % 3], g_ref[...]).astype(BF16)
        acc = jnp.zeros(o_ref.shape, F32)
        for c in range(dff // tf):
            cols = slice(c * tf, (c + 1) * tf)
            a = _silu(_dot(h, wg_ref[:, cols])) * _dot(h, wu_ref[:, cols])
            acc = acc + _dot(a.astype(BF16), wd_ref[cols, :])
        o_ref[...] = acc

    @pl.when(i == nb - 1)
    def _():
        wait_block((i + 1) % 3)
        wait_block((i + 2) % 3)


def _experts(x, src_tok, g, n_used, block_expert, wg_bf16, wu_bf16, wd_bf16):
    r = src_tok.shape[0]
    d = x.shape[1]
    dff = wg_bf16.shape[2]
    tm = MOE_ROWS
    assert dff % MOE_FF_TILE == 0
    resident = pl.Buffered(1)
    wmap = lambda i, be, st, nu: (be[i], 0, 0)
    return pl.pallas_call(
        _expert_body,
        grid_spec=pltpu.PrefetchScalarGridSpec(
            num_scalar_prefetch=3,
            grid=(r // tm,),
            in_specs=[pl.BlockSpec(memory_space=pl.ANY),
                      pl.BlockSpec((1, d), lambda i, be, st, nu: (0, 0)),
                      pl.BlockSpec((None, d, dff), wmap, pipeline_mode=resident),
                      pl.BlockSpec((None, d, dff), wmap, pipeline_mode=resident),
                      pl.BlockSpec((None, dff, d), wmap, pipeline_mode=resident)],
            out_specs=pl.BlockSpec((tm, d), lambda i, be, st, nu: (i, 0)),
            scratch_shapes=[pltpu.VMEM((3, tm, d), F32), pltpu.SemaphoreType.DMA((3,))],
        ),
        out_shape=jax.ShapeDtypeStruct((r, d), F32),
        compiler_params=_params("arbitrary"),
        name="experts",
    )(block_expert, src_tok, n_used, x, g.reshape(1, d), wg_bf16, wu_bf16, wd_bf16)


def _combine_body(d0_ref, d1_ref, x_ref, gate_ref, g_ref, src_ref, *rest, n_prompt_tiles):
    i = pl.program_id(0)
    n = pl.num_programs(0)
    buf, sem = rest[-2:]

    def copies(block, r, slot):
        t = block * GATHER_ROWS + r
        return (pltpu.make_async_copy(src_ref.at[pl.ds(d0_ref[t], 1), :], buf.at[slot, 0, pl.ds(r, 1), :],
                                      sem.at[slot]),
                pltpu.make_async_copy(src_ref.at[pl.ds(d1_ref[t], 1), :], buf.at[slot, 1, pl.ds(r, 1), :],
                                      sem.at[slot]))

    def issue(block, slot):
        def body(r, c):
            for cp in copies(block, r, slot):
                cp.start()
            return c
        lax.fori_loop(0, GATHER_ROWS, body, 0, unroll=4)

    @pl.when(i == 0)
    def _():
        issue(0, 0)

    @pl.when(i + 1 < n)
    def _():
        issue(i + 1, (i + 1) % 2)

    def wait(r, c):
        for cp in copies(0, r, i % 2):
            cp.wait()
        return c

    lax.fori_loop(0, GATHER_ROWS, wait, 0, unroll=4)
    gate = gate_ref[...]
    y = x_ref[...] + gate[:, 0:1] * buf[i % 2, 0] + gate[:, 1:2] * buf[i % 2, 1]
    if n_prompt_tiles is None:
        rest[0][...] = y
    else:
        y = _rms(y, g_ref[...])

        @pl.when(i < n_prompt_tiles)
        def _():
            rest[0][...] = y

        @pl.when(i >= n_prompt_tiles)
        def _():
            rest[1][...] = y


def _combine(x, gate, expert_rows, d0, d1, final_g=None, n_prompt=None):
    nt, d = x.shape
    tm = GATHER_ROWS
    row = lambda i, a, b: (i, 0)
    if final_g is None:
        g = jnp.ones((d,), F32)
        npt = None
        out_specs = pl.BlockSpec((tm, d), row)
        out_shape = jax.ShapeDtypeStruct((nt, d), F32)
    else:
        g = final_g
        npt = n_prompt // tm
        out_specs = [pl.BlockSpec((tm, d), lambda i, a, b: (jnp.minimum(i, npt - 1), 0)),
                     pl.BlockSpec((tm, d), lambda i, a, b: (jnp.maximum(i - npt, 0), 0))]
        out_shape = [jax.ShapeDtypeStruct((n_prompt, d), F32), jax.ShapeDtypeStruct((nt - n_prompt, d), F32)]
    return pl.pallas_call(
        functools.partial(_combine_body, n_prompt_tiles=npt),
        grid_spec=pltpu.PrefetchScalarGridSpec(
            num_scalar_prefetch=2,
            grid=(nt // tm,),
            in_specs=[pl.BlockSpec((tm, d), row),
                      pl.BlockSpec((tm, LANES), row),
                      pl.BlockSpec((1, d), lambda i, a, b: (0, 0)),
                      pl.BlockSpec(memory_space=pl.ANY)],
            out_specs=out_specs,
            scratch_shapes=[pltpu.VMEM((2, 2, tm, d), F32), pltpu.SemaphoreType.DMA((2,))],
        ),
        out_shape=out_shape,
        compiler_params=_params("arbitrary"),
        name="combine",
    )(d0, d1, x, gate, g.reshape(1, d), expert_rows)


def _moe(x, g, w_router, wg_bf16, wu_bf16, wd_bf16, final_g=None, n_prompt=None):
    nt, d = x.shape
    n_exp = w_router.shape[1]
    tm = MOE_ROWS
    idx, gate = _route(x, g, w_router)
    e_flat = idx[:, :TOP_K].reshape(-1)
    n_assign = nt * TOP_K
    onehot = (e_flat[:, None] == jnp.arange(n_exp, dtype=jnp.int32)[None, :]).astype(jnp.int32)
    rank = jnp.cumsum(onehot, axis=0) - onehot
    pos = jnp.sum(rank * onehot, axis=1)
    counts = jnp.sum(onehot, axis=0)
    padded = (counts + tm - 1) // tm * tm
    pend = jnp.cumsum(padded)
    dest = ((pend - padded)[e_flat] + pos).astype(jnp.int32)
    n_rows = (n_assign // tm + n_exp) * tm
    tok = jnp.arange(n_assign, dtype=jnp.int32) // TOP_K
    src_tok = jnp.zeros((n_rows,), jnp.int32).at[dest].set(tok, unique_indices=True)
    block_expert = jnp.minimum(
        jnp.searchsorted(pend, jnp.arange(n_rows // tm, dtype=jnp.int32) * tm, side='right'), n_exp - 1
    ).astype(jnp.int32)
    n_used = (pend[n_exp - 1:] // tm).astype(jnp.int32)
    y_rows = _experts(x, src_tok, g, n_used, block_expert, wg_bf16, wu_bf16, wd_bf16)
    dest2 = dest.reshape(nt, TOP_K)
    return _combine(x, gate, y_rows, dest2[:, 0], dest2[:, 1], final_g, n_prompt)


def _norm_body(x_ref, g_ref, o_ref):
    o_ref[...] = _rms(x_ref[...], g_ref[...])


def _final_norm(x, g):
    nt, d = x.shape
    tm = ROW_TILE
    return pl.pallas_call(
        _norm_body,
        grid=(nt // tm,),
        in_specs=[pl.BlockSpec((tm, d), lambda i: (i, 0)), pl.BlockSpec((1, d), lambda i: (0, 0))],
        out_specs=pl.BlockSpec((tm, d), lambda i: (i, 0)),
        out_shape=jax.ShapeDtypeStruct((nt, d), F32),
        compiler_params=_params("parallel"),
        name="final_norm",
    )(x, g.reshape(1, d))


def kernel(x_prompt, x_sample, cache_k, cache_v, state_conv, page_table, rel_bias, norm_mix, w_in, w_gate,
           w_sgu, b_sgu, g_sgu, w_conv, w_branch, w_out, norm_ffn, w_ff_gate, w_ff_up, w_ff_down,
           w_router, w_exp_gate, w_exp_up, w_exp_down, norm_final):
    batch, seq, d = x_prompt.shape
    n_seq, t_new, _ = x_sample.shape
    depth = w_in.shape[0]
    n_p = batch * seq
    n_s = n_seq * t_new
    assert t_new == T_NEW and seq % MOBA_BLOCK == 0 and seq // MOBA_BLOCK <= LANES
    assert n_p % FFN_ROWS == 0 and n_s % ROW_TILE == 0 and (n_p + n_s) % FFN_ROWS == 0
    n_kb = seq // MOBA_BLOCK
    group_w = SGU_WIDTH // SGU_GROUPS

    x = jnp.concatenate([x_prompt.reshape(n_p, d), x_sample.reshape(n_s, d)], axis=0)
    bt = rel_bias.T.astype(F32)
    moba_tiles = _moba_bias_tiles(bt)
    far_bias = bt[:, REL_BUCKETS - 1]

    outs = {k: [] for k in ("kp", "vp", "ks", "vs", "cp", "cs", "sv")}
    for l in range(depth):
        q, k_s, v_s, kt, vt, uv, bch, kb, vb, km = _project(x, norm_mix[l], w_in[l].astype(BF16), batch, seq)

        kmean = jnp.pad(km[:n_p // MOBA_BLOCK, 0, :].reshape(batch, n_kb, QKV), ((0, 0), (0, LANES - n_kb), (0, 0)))
        o_a = _moba_prompt(q, kb, vb, kmean, moba_tiles, far_bias, batch, seq)
        o_c = _sb_prompt(q, kb, vb, batch, seq)
        ws = jnp.tril(w_sgu[l]).astype(BF16)
        bs = jnp.repeat(b_sgu[l].T, group_w, axis=1)
        o_b, o_d, z_p, _ = _mix(uv, bch, None, ws, bs, g_sgu[l], w_conv[l], seq // SGU_CHUNK, n_p)

        o_att = _attn_sample(q[n_p:], k_s, v_s, cache_k, cache_v, l, page_table, bt)
        per_chunk = SGU_CHUNK // T_NEW
        w4 = jnp.tril(w_sgu[l][:, :T_NEW, :T_NEW])
        ws_s = jnp.einsum('ab,gts->gatbs', jnp.eye(per_chunk, dtype=F32), w4)
        ws_s = ws_s.reshape(SGU_GROUPS, SGU_CHUNK, SGU_CHUNK).astype(BF16)
        bs_s = jnp.tile(jnp.repeat(b_sgu[l][:, :T_NEW].T, group_w, axis=1), (per_chunk, 1))
        prev = state_conv[l]
        zero = jnp.zeros((n_seq, 1, CONV_DIM), F32)
        p1 = jnp.concatenate([prev[:, 1:2], zero, zero, zero], axis=1).reshape(n_s, CONV_DIM)
        p2 = jnp.concatenate([prev[:, 0:1], prev[:, 1:2], zero, zero], axis=1).reshape(n_s, CONV_DIM)
        s_b, s_d, z_s, vn_s = _mix(uv[n_p:], bch[n_p:], (p1, p2), ws_s, bs_s, g_sgu[l], w_conv[l], None, n_s)

        prompt_br = (o_a, o_b, o_c, o_d)
        sample_br = (o_att[:, :QKV // 2], s_b, o_att[:, QKV // 2:], s_d)
        x = _merge(x, norm_mix[l], prompt_br, sample_br, w_gate[l].astype(BF16), w_branch[l].astype(BF16),
                   w_out[l].astype(BF16), n_p)

        i = l // 2
        if l % 2 == 0:
            x = _ffn(x, norm_ffn[l], w_ff_gate[i].astype(BF16), w_ff_up[i].astype(BF16), w_ff_down[i].astype(BF16))
        else:
            x = _moe(x, norm_ffn[l], w_router[i], w_exp_gate[i].astype(BF16), w_exp_up[i].astype(BF16),
                     w_exp_down[i].astype(BF16), norm_final if l == depth - 1 else None, n_p)

        outs["kp"].append(kt.reshape(batch, N_HEADS, HEAD_DIM, seq).transpose(0, 3, 1, 2))
        outs["vp"].append(vt.reshape(batch, N_HEADS, HEAD_DIM, seq).transpose(0, 3, 1, 2))
        outs["ks"].append(k_s.reshape(n_seq, t_new, N_HEADS, HEAD_DIM))
        outs["vs"].append(v_s.reshape(n_seq, t_new, N_HEADS, HEAD_DIM))
        outs["cp"].append(z_p.reshape(batch, seq, CONV_DIM)[:, seq - (CONV_WIDTH - 1):])
        outs["cs"].append(z_s.reshape(n_seq, t_new, CONV_DIM)[:, t_new - (CONV_WIDTH - 1):])
        outs["sv"].append(vn_s.reshape(n_seq, t_new, SGU_WIDTH))

    if depth % 2 == 0:
        y_p, y_s = x
    else:
        y = _final_norm(x, norm_final)
        y_p, y_s = y[:n_p], y[n_p:]
    return (y_p.reshape(batch, seq, d), y_s.reshape(n_seq, t_new, d),
            jnp.stack(outs["kp"]), jnp.stack(outs["vp"]), jnp.stack(outs["ks"]), jnp.stack(outs["vs"]),
            jnp.stack(outs["cp"]), jnp.stack(outs["cs"]), jnp.stack(outs["sv"]))
```
